```python
import math
import jax
import jax.numpy as jnp
from jax import lax
import numpy as np

D_MODEL = 1024
BATCH = 8
SEQ = 2048
DEPTH = 1

N_ATT_HEADS = 16
ATT_HEAD_DIM = 64
N_KV_HEADS = 4
ATT_DIM = N_ATT_HEADS * ATT_HEAD_DIM
KV_DIM = N_KV_HEADS * ATT_HEAD_DIM
Q_LORA = 256
IDX_HEADS = 8
IDX_DIM = 64
TOPK_MAX = 256
QBLK = 128
ROT_DIM = ATT_HEAD_DIM // 4
ROPE_THETA = 500000.0
D_INNER = 2 * D_MODEL
SSD_HEAD_DIM = 64
SSD_HEADS = D_INNER // SSD_HEAD_DIM
SSD_GROUPS = 4
D_STATE = 128
CONV_W = 4
CHUNK = 128
XBC_DIM = D_INNER + 2 * SSD_GROUPS * D_STATE
D_FF = 4 * D_MODEL
IN_SIZES = (Q_LORA, KV_DIM, KV_DIM, IDX_DIM, IDX_HEADS, D_INNER, XBC_DIM, SSD_HEADS, 2 * D_MODEL)
IN_COLS = Q_LORA + 2 * KV_DIM + IDX_DIM + IDX_HEADS + D_INNER + XBC_DIM + SSD_HEADS + 2 * D_MODEL
ALPHA = (2 * DEPTH) ** 0.25
BETA = (8 * DEPTH) ** -0.25
LN_EPS = 1e-5

kernel_name = "hybrid_dsa_ssd_gated_deepnorm"

F32 = jnp.float32


def _split_cols(p, sizes):
    out, o = [], 0
    for s in sizes:
        out.append(p[..., o:o + s])
        o += s
    return out


def _layer_norm(x, g, b):
    xf = x.astype(F32)
    mu = jnp.mean(xf, -1, keepdims=True)
    var = jnp.mean(jnp.square(xf - mu), -1, keepdims=True)
    return ((xf - mu) * lax.rsqrt(var + LN_EPS) * g.astype(F32) + b.astype(F32)).astype(x.dtype)


def _rms_norm(x, w):
    xf = x.astype(F32)
    return (xf * lax.rsqrt(jnp.mean(xf * xf, -1, keepdims=True) + LN_EPS) * w.astype(F32)).astype(x.dtype)


def _rope_tables(seq):
    inv = ROPE_THETA ** (-jnp.arange(0, ROT_DIM, 2, dtype=F32) / ROT_DIM)
    ang = jnp.arange(seq, dtype=F32)[:, None] * inv[None, :]
    return jnp.cos(ang), jnp.sin(ang)


def _partial_rope(x, cos, sin):
    half = ROT_DIM // 2
    c = cos[None, :, None, :]
    s = sin[None, :, None, :]
    xr = x[..., :ROT_DIM].astype(F32)
    x1, x2 = xr[..., :half], xr[..., half:]
    rot = jnp.concatenate([x1 * c - x2 * s, x2 * c + x1 * s], -1).astype(x.dtype)
    return jnp.concatenate([rot, x[..., ROT_DIM:]], -1)


def _causal_dwconv(u, w, b):
    ch = u.shape[-1]
    y = lax.conv_general_dilated(u, w[:, None, :].astype(u.dtype), window_strides=(1,),
                                 padding=[(CONV_W - 1, 0)],
                                 dimension_numbers=('NWC', 'WIO', 'NWC'),
                                 feature_group_count=ch)
    return y + b.astype(u.dtype)


def _dsa_attention(q, k, v, qi, ki, wi, topk):
    bsz, seq = q.shape[0], q.shape[1]
    nb = seq // QBLK
    grp = N_ATT_HEADS // N_KV_HEADS
    spos = jnp.arange(seq)

    def to_blocks(t):
        return jnp.swapaxes(t.reshape((bsz, nb, QBLK) + t.shape[2:]), 0, 1)

    def block(args):
        qb, qib, wb, blk = args
        tpos = blk * QBLK + jnp.arange(QBLK)
        sc = jnp.einsum('bqhd,bsd->bqhs', qib, ki).astype(F32) * (IDX_DIM ** -0.5)
        idx_score = jnp.einsum('bqh,bqhs->bqs', wb.astype(F32), jax.nn.relu(sc))
        causal = spos[None, :] <= tpos[:, None]
        idx_score = jnp.where(causal[None], idx_score, -jnp.inf)
        _, sel = lax.top_k(idx_score, topk)
        valid = sel <= tpos[None, :, None]
        kg = jax.vmap(lambda kb, ib: kb[ib])(k, sel)
        vg = jax.vmap(lambda vb, ib: vb[ib])(v, sel)
        qg = qb.reshape(bsz, QBLK, N_KV_HEADS, grp, ATT_HEAD_DIM)
        s = jnp.einsum('bqkgd,bqnkd->bqkgn', qg, kg).astype(F32) * (ATT_HEAD_DIM ** -0.5)
        s = jnp.where(valid[:, :, None, None, :], s, -jnp.inf)
        p = jax.nn.softmax(s, axis=-1).astype(v.dtype)
        o = jnp.einsum('bqkgn,bqnkd->bqkgd', p, vg)
        return o.reshape(bsz, QBLK, ATT_DIM)

    outs = lax.map(block, (to_blocks(q), to_blocks(qi), to_blocks(wi), jnp.arange(nb)))
    return jnp.swapaxes(outs, 0, 1).reshape(bsz, seq, ATT_DIM)


def _ssd(xh, dt, a, bm, cm):
    bsz, seq, nh, hp = xh.shape
    ng, ns = bm.shape[2], bm.shape[3]
    r = nh // ng
    nc, l = seq // CHUNK, CHUNK
    xf = (xh.astype(F32) * dt[..., None]).reshape(bsz, nc, l, ng, r, hp)
    adt = (dt * a).reshape(bsz, nc, l, ng, r).transpose(0, 3, 4, 1, 2)
    a_cum = jnp.cumsum(adt, axis=-1)
    bmc = bm.astype(F32).reshape(bsz, nc, l, ng, ns)
    cmc = cm.astype(F32).reshape(bsz, nc, l, ng, ns)
    tri = jnp.tril(jnp.ones((l, l), dtype=bool))
    seg = a_cum[..., :, None] - a_cum[..., None, :]
    lmat = jnp.exp(jnp.where(tri, seg, -jnp.inf))
    cb = jnp.einsum('bclgn,bcsgn->bcgls', cmc, bmc)
    y_diag = jnp.einsum('bcgls,bgrcls,bcsgrp->bclgrp', cb, lmat, xf)
    decay_states = jnp.exp(a_cum[..., -1:] - a_cum)
    states = jnp.einsum('bclgn,bgrcl,bclgrp->cbgrpn', bmc, decay_states, xf)
    chunk_decay = jnp.moveaxis(jnp.exp(a_cum[..., -1]), 3, 0)

    def step(h, inp):
        dec, st = inp
        return h * dec[..., None, None] + st, h

    h0 = jnp.zeros((bsz, ng, r, hp, ns), F32)
    _, prev = lax.scan(step, h0, (chunk_decay, states))
    y_off = jnp.einsum('bclgn,cbgrpn,bgrcl->bclgrp', cmc, prev, jnp.exp(a_cum))
    return (y_diag + y_off).reshape(bsz, seq, nh, hp)


def _layer(x, cos, sin, topk, w_in, q_norm_w, w_uq, w_iq, k_idx_norm_g, k_idx_norm_b,
           conv_w, conv_b, dt_bias, a_log, d_skip, ssd_norm_w, w_attn_branch,
           w_ssd_branch, w_out, ln1_g, ln1_b, w_up, w_down, ln2_g, ln2_b):
    bsz, seq, _ = x.shape
    c_q, k, v, k_idx, w_idx, z, xbc, dt_raw, gates = _split_cols(x @ w_in, IN_SIZES)

    cq = _rms_norm(c_q, q_norm_w)
    q = _partial_rope((cq @ w_uq).reshape(bsz, seq, N_ATT_HEADS, ATT_HEAD_DIM), cos, sin)
    qi = _partial_rope((cq @ w_iq).reshape(bsz, seq, IDX_HEADS, IDX_DIM), cos, sin)
    k = _partial_rope(k.reshape(bsz, seq, N_KV_HEADS, ATT_HEAD_DIM), cos, sin)
    v = v.reshape(bsz, seq, N_KV_HEADS, ATT_HEAD_DIM)
    ki = _partial_rope(_layer_norm(k_idx, k_idx_norm_g, k_idx_norm_b)[:, :, None, :], cos, sin)[:, :, 0, :]
    wi = w_idx * (IDX_HEADS ** -0.5)
    att = _dsa_attention(q, k, v, qi, ki, wi, topk)

    xbc = jax.nn.silu(_causal_dwconv(xbc, conv_w, conv_b))
    xs, bm, cm = _split_cols(xbc, (D_INNER, SSD_GROUPS * D_STATE, SSD_GROUPS * D_STATE))
    dt = jax.nn.softplus(dt_raw.astype(F32) + dt_bias.astype(F32))
    a = -jnp.exp(a_log.astype(F32))
    xh = xs.reshape(bsz, seq, SSD_HEADS, SSD_HEAD_DIM)
    y = _ssd(xh, dt, a, bm.reshape(bsz, seq, SSD_GROUPS, D_STATE), cm.reshape(bsz, seq, SSD_GROUPS, D_STATE))
    y = y + d_skip.astype(F32)[:, None] * xh.astype(F32)
    yg = (y.reshape(bsz, seq, D_INNER) * jax.nn.silu(z.astype(F32))).reshape(bsz, seq, SSD_GROUPS, D_INNER // SSD_GROUPS)
    yg = yg * lax.rsqrt(jnp.mean(yg * yg, -1, keepdims=True) + LN_EPS)
    ssd_out = (yg.reshape(bsz, seq, D_INNER) * ssd_norm_w.astype(F32)).astype(x.dtype)

    g_a, g_s = _split_cols(gates, (D_MODEL, D_MODEL))
    mixed = jax.nn.sigmoid(g_a) * (att @ w_attn_branch) + jax.nn.sigmoid(g_s) * (ssd_out @ w_ssd_branch)
    h = _layer_norm(ALPHA * x + mixed @ w_out, ln1_g, ln1_b)

    ff = jnp.square(jax.nn.relu(h @ w_up)) @ w_down
    return _layer_norm(ALPHA * h + ff, ln2_g, ln2_b)


def setup_inputs(seed: int = 0) -> dict:
    key = jax.random.key(seed)
    ks = jax.random.split(key, 24)

    def nrm(k, shape, scale):
        return jax.random.normal(k, shape, F32) * scale

    L = DEPTH
    u_dt = jax.random.uniform(ks[10], (L, SSD_HEADS), F32)
    dt0 = jnp.exp(u_dt * (math.log(0.1) - math.log(0.001)) + math.log(0.001))
    return {
        "x": nrm(ks[0], (BATCH, SEQ, D_MODEL), 1.0),
        "w_in": nrm(ks[1], (L, D_MODEL, IN_COLS), D_MODEL ** -0.5),
        "q_norm_w": 1.0 + nrm(ks[2], (L, Q_LORA), 0.02),
        "w_uq": nrm(ks[3], (L, Q_LORA, ATT_DIM), Q_LORA ** -0.5),
        "w_iq": nrm(ks[4], (L, Q_LORA, IDX_HEADS * IDX_DIM), Q_LORA ** -0.5),
        "k_idx_norm_g": 1.0 + nrm(ks[5], (L, IDX_DIM), 0.02),
        "k_idx_norm_b": nrm(ks[6], (L, IDX_DIM), 0.02),
        "conv_w": nrm(ks[7], (L, CONV_W, XBC_DIM), CONV_W ** -0.5),
        "conv_b": nrm(ks[8], (L, XBC_DIM), 0.02),
        "dt_bias": dt0 + jnp.log(-jnp.expm1(-dt0)),
        "a_log": jnp.log(jax.random.uniform(ks[11], (L, SSD_HEADS), F32, 1.0, 16.0)),
        "d_skip": 1.0 + nrm(ks[12], (L, SSD_HEADS), 0.1),
        "ssd_norm_w": 1.0 + nrm(ks[13], (L, D_INNER), 0.02),
        "w_attn_branch": nrm(ks[14], (L, ATT_DIM, D_MODEL), ATT_DIM ** -0.5),
        "w_ssd_branch": nrm(ks[15], (L, D_INNER, D_MODEL), D_INNER ** -0.5),
        "w_out": nrm(ks[16], (L, D_MODEL, D_MODEL), BETA * D_MODEL ** -0.5),
        "ln1_g": 1.0 + nrm(ks[17], (L, D_MODEL), 0.02),
        "ln1_b": nrm(ks[18], (L, D_MODEL), 0.02),
        "w_up": nrm(ks[19], (L, D_MODEL, D_FF), D_MODEL ** -0.5),
        "w_down": nrm(ks[20], (L, D_FF, D_MODEL), BETA * D_FF ** -0.5),
        "ln2_g": 1.0 + nrm(ks[21], (L, D_MODEL), 0.02),
        "ln2_b": nrm(ks[22], (L, D_MODEL), 0.02),
    }


def reference(x, w_in, q_norm_w, w_uq, w_iq, k_idx_norm_g, k_idx_norm_b, conv_w, conv_b,
              dt_bias, a_log, d_skip, ssd_norm_w, w_attn_branch, w_ssd_branch, w_out,
              ln1_g, ln1_b, w_up, w_down, ln2_g, ln2_b):
    seq = x.shape[1]
    topk = min(TOPK_MAX, seq // 4)
    cos, sin = _rope_tables(seq)
    h = x
    for l in range(DEPTH):
        h = _layer(h, cos, sin, topk, w_in[l], q_norm_w[l], w_uq[l], w_iq[l],
                   k_idx_norm_g[l], k_idx_norm_b[l], conv_w[l], conv_b[l], dt_bias[l],
                   a_log[l], d_skip[l], ssd_norm_w[l], w_attn_branch[l], w_ssd_branch[l],
                   w_out[l], ln1_g[l], ln1_b[l], w_up[l], w_down[l], ln2_g[l], ln2_b[l])
    return h
```

```python
import functools

import jax
import jax.numpy as jnp
from jax import lax
from jax.experimental import pallas as pl
from jax.experimental.pallas import tpu as pltpu

F32 = jnp.float32
BF16 = jnp.bfloat16
I32 = jnp.int32

D_MODEL = 1024
N_ATT_HEADS = 16
ATT_HEAD_DIM = 64
N_KV_HEADS = 4
ATT_DIM = N_ATT_HEADS * ATT_HEAD_DIM
KV_DIM = N_KV_HEADS * ATT_HEAD_DIM
Q_LORA = 256
IDX_HEADS = 8
IDX_DIM = 64
TOPK_MAX = 256
ROT_DIM = ATT_HEAD_DIM // 4
ROPE_THETA = 500000.0
D_INNER = 2 * D_MODEL
SSD_HEAD_DIM = 64
SSD_HEADS = D_INNER // SSD_HEAD_DIM
SSD_GROUPS = 4
D_STATE = 128
CONV_W = 4
CHUNK = 128
BC_DIM = SSD_GROUPS * D_STATE
XBC_DIM = D_INNER + 2 * BC_DIM
D_FF = 4 * D_MODEL
IN_SIZES = (Q_LORA, KV_DIM, KV_DIM, IDX_DIM, IDX_HEADS, D_INNER, XBC_DIM, SSD_HEADS, 2 * D_MODEL)
LN_EPS = 1e-5

LANES = 128
SUBLANES = 8
VMEM_LIMIT_BYTES = 56 * 1024 * 1024

SMALL_COLS = Q_LORA + 2 * KV_DIM + LANES
TAIL_OFF = Q_LORA + 2 * KV_DIM
WI_OFF = IDX_DIM
DT_OFF = IDX_DIM + IDX_HEADS
TAIL_END = DT_OFF + SSD_HEADS

INT_MIN = -(2 ** 31)
KEY_NEG_INF = -2139095041


def _sigmoid(v):
    return 1.0 / (1.0 + jnp.exp(-v))


def _const_spec(shape):
    zeros = (0,) * len(shape)
    return pl.BlockSpec(shape, lambda *_: zeros, pipeline_mode=pl.Buffered(1))


def _proj_kernel(x_ref, wa_ref, wz_ref, wx_ref, wg_ref, qnw_ref, wuq_ref, wiq_ref,
                 kng_ref, knb_ref, dtb_ref, rc_ref, rs1_ref, rs2_ref,
                 q_ref, k_ref, v_ref, qi_ref, misc_ref, z_ref, xbc_ref, g_ref):
    xb = x_ref[...].astype(BF16)
    rc, rs1, rs2 = rc_ref[...], rs1_ref[...], rs2_ref[...]

    def rope(t):
        half = ROT_DIM // 2
        return (t * rc + pltpu.roll(t, LANES - half, 1) * rs1 + pltpu.roll(t, half, 1) * rs2)

    pa = jnp.dot(xb, wa_ref[...], preferred_element_type=F32)

    c_q = pa[:, :Q_LORA]
    ms = jnp.mean(c_q * c_q, axis=-1, keepdims=True)
    cq = (c_q * lax.rsqrt(ms + LN_EPS) * qnw_ref[...]).astype(BF16)

    scale = ATT_HEAD_DIM ** -0.5
    qf = jnp.dot(cq, wuq_ref[...], preferred_element_type=F32)
    for j in range(ATT_DIM // LANES):
        sl = slice(j * LANES, (j + 1) * LANES)
        q_ref[:, sl] = (rope(qf[:, sl]) * scale).astype(BF16)
    qif = jnp.dot(cq, wiq_ref[...], preferred_element_type=F32)
    iscale = IDX_DIM ** -0.5
    for j in range(IDX_HEADS * IDX_DIM // LANES):
        sl = slice(j * LANES, (j + 1) * LANES)
        qi_ref[:, sl] = (rope(qif[:, sl]) * iscale).astype(BF16)
    for j in range(KV_DIM // LANES):
        sl = slice(j * LANES, (j + 1) * LANES)
        k_ref[:, sl] = rope(pa[:, Q_LORA + j * LANES:Q_LORA + (j + 1) * LANES]).astype(BF16)
    v_ref[...] = pa[:, Q_LORA + KV_DIM:Q_LORA + 2 * KV_DIM].astype(BF16)

    tail = pa[:, TAIL_OFF:TAIL_OFF + LANES]
    lane = lax.broadcasted_iota(I32, tail.shape, 1)
    is_ki = lane < IDX_DIM
    mu = jnp.sum(jnp.where(is_ki, tail, 0.0), axis=-1, keepdims=True) * (1.0 / IDX_DIM)
    dv = jnp.where(is_ki, tail - mu, 0.0)
    var = jnp.sum(dv * dv, axis=-1, keepdims=True) * (1.0 / IDX_DIM)
    ki = rope(dv * lax.rsqrt(var + LN_EPS) * kng_ref[...] + knb_ref[...])
    wi = tail * (IDX_HEADS ** -0.5)
    dtr = tail + dtb_ref[...]
    dt = jnp.maximum(dtr, 0.0) + jnp.log1p(jnp.exp(-jnp.abs(dtr)))
    misc_ref[...] = jnp.where(is_ki, ki, jnp.where(lane < DT_OFF, wi,
                                                   jnp.where(lane < TAIL_END, dt, 0.0)))

    z_ref[...] = jnp.dot(xb, wz_ref[...], preferred_element_type=F32).astype(BF16)
    xbc_ref[...] = jnp.dot(xb, wx_ref[...], preferred_element_type=F32)
    g_ref[...] = jnp.dot(xb, wg_ref[...], preferred_element_type=F32).astype(BF16)


def _proj_call(x2, wa, wz, wx, wg, qnw, wuq, wiq, kng, knb, dtb, rc, rs1, rs2, seq, tm):
    m = x2.shape[0]
    nsb = seq // tm
    row = lambda n: pl.BlockSpec((tm, n), lambda i: (i, 0))
    tab = pl.BlockSpec((tm, LANES), lambda i: (i % nsb, 0))
    out_shapes = (
        jax.ShapeDtypeStruct((m, ATT_DIM), BF16),
        jax.ShapeDtypeStruct((m, KV_DIM), BF16),
        jax.ShapeDtypeStruct((m, KV_DIM), BF16),
        jax.ShapeDtypeStruct((m, IDX_HEADS * IDX_DIM), BF16),
        jax.ShapeDtypeStruct((m, LANES), F32),
        jax.ShapeDtypeStruct((m, D_INNER), BF16),
        jax.ShapeDtypeStruct((m, XBC_DIM), F32),
        jax.ShapeDtypeStruct((m, 2 * D_MODEL), BF16),
    )
    return pl.pallas_call(
        _proj_kernel,
        grid=(m // tm,),
        in_specs=[row(D_MODEL), _const_spec(wa.shape), _const_spec(wz.shape), _const_spec(wx.shape),
                  _const_spec(wg.shape), _const_spec(qnw.shape), _const_spec(wuq.shape),
                  _const_spec(wiq.shape), _const_spec(kng.shape), _const_spec(knb.shape),
                  _const_spec(dtb.shape), tab, tab, tab],
        out_specs=[row(ATT_DIM), row(KV_DIM), row(KV_DIM), row(IDX_HEADS * IDX_DIM), row(LANES),
                   row(D_INNER), row(XBC_DIM), row(2 * D_MODEL)],
        out_shape=out_shapes,
        compiler_params=pltpu.CompilerParams(dimension_semantics=("parallel",),
                                             vmem_limit_bytes=VMEM_LIMIT_BYTES),
        name="proj",
    )(x2, wa, wz, wx, wg, qnw, wuq, wiq, kng, knb, dtb, rc, rs1, rs2)


def _attn_kernel(ki_ref, qit_ref, wit_ref, k_ref, vt_ref, qt_ref, o_ref,
                 key_ref, bias_ref, cut_ref, ot_ref, *, topk):
    seq = ki_ref.shape[1]
    tq = qt_ref.shape[2]
    qblk = pl.program_id(1)

    ki = ki_ref[0]
    acc = jnp.zeros((seq, tq), F32)
    for hh in range(IDX_HEADS):
        sc = jnp.dot(ki, qit_ref[0, hh * IDX_DIM:(hh + 1) * IDX_DIM, :], preferred_element_type=F32)
        acc = acc + wit_ref[0, hh:hh + 1, :] * jnp.maximum(sc, 0.0)

    spos = lax.broadcasted_iota(I32, (seq, tq), 0)
    tpos = qblk * tq + lax.broadcasted_iota(I32, (seq, tq), 1)
    causal = spos <= tpos

    bits = pltpu.bitcast(jnp.where(acc == 0.0, 0.0, acc), I32)
    key = jnp.where(bits < 0, bits ^ jnp.int32(0x7FFFFFFF), bits)
    key_ref[...] = jnp.where(causal, key, jnp.int32(KEY_NEG_INF))

    def search(i, thr_u):
        cand = thr_u | lax.shift_left(jnp.int32(1), 31 - i)
        cnt = jnp.sum(jnp.where(key_ref[...] >= (cand ^ jnp.int32(INT_MIN)), 1, 0),
                      axis=0, keepdims=True)
        return jnp.where(cnt >= topk, cand, thr_u)

    thr_u = lax.fori_loop(0, 32, search, jnp.zeros((1, tq), I32))
    thr = thr_u ^ jnp.int32(INT_MIN)

    key = key_ref[...]
    n_gt = jnp.sum(jnp.where(key > thr, 1, 0), axis=0, keepdims=True)
    n_eq = jnp.sum(jnp.where(key == thr, 1, 0), axis=0, keepdims=True)
    need = topk - n_gt
    cut_ref[...] = jnp.full((1, tq), seq, I32)
    excess = jnp.where((n_eq > need) & (thr != jnp.int32(KEY_NEG_INF)), 1, 0)

    @pl.when(jnp.max(excess) > 0)
    def _():
        nbits = (seq - 1).bit_length()

        def tie_search(i, cut):
            cand = cut | lax.shift_left(jnp.int32(1), nbits - 1 - i)
            kk = key_ref[...]
            cnt = jnp.sum(jnp.where((kk == thr) & (spos < cand), 1, 0), axis=0, keepdims=True)
            return jnp.where(cnt < need, cand, cut)

        cut_ref[...] = lax.fori_loop(0, nbits, tie_search, jnp.zeros((1, tq), I32))

    cut = cut_ref[...]
    sel = jnp.where(key > thr, 1, jnp.where(key == thr, jnp.where(spos <= cut, 1, 0), 0))
    sel = jnp.where(causal, sel, 0)
    bias_ref[...] = jnp.where(sel > 0, 0.0, -jnp.inf)

    grp = N_ATT_HEADS // N_KV_HEADS

    def head(h, carry):
        g = h // grp
        qh = qt_ref[0, pl.ds(pl.multiple_of(h * ATT_HEAD_DIM, ATT_HEAD_DIM), ATT_HEAD_DIM), :]
        s = jnp.dot(k_ref[0, g], qh, preferred_element_type=F32) + bias_ref[...]
        mx = jnp.max(s, axis=0, keepdims=True)
        p = jnp.exp(s - mx)
        den = jnp.sum(p, axis=0, keepdims=True)
        o = jnp.dot(vt_ref[0, g], p.astype(BF16), preferred_element_type=F32)
        ot_ref[pl.ds(pl.multiple_of(h * ATT_HEAD_DIM, ATT_HEAD_DIM), ATT_HEAD_DIM), :] = o / den
        return carry

    lax.fori_loop(0, N_ATT_HEADS, head, 0)
    o_ref[0] = ot_ref[...].T.astype(BF16)


def _attn_call(ki, qit, wit, k4, vt4, qt, topk, tq):
    bsz, seq, _ = ki.shape
    return pl.pallas_call(
        functools.partial(_attn_kernel, topk=topk),
        grid=(bsz, seq // tq),
        in_specs=[
            pl.BlockSpec((1, seq, IDX_DIM), lambda b, j: (b, 0, 0)),
            pl.BlockSpec((1, IDX_HEADS * IDX_DIM, tq), lambda b, j: (b, 0, j)),
            pl.BlockSpec((1, IDX_HEADS, tq), lambda b, j: (b, 0, j)),
            pl.BlockSpec((1, N_KV_HEADS, seq, ATT_HEAD_DIM), lambda b, j: (b, 0, 0, 0)),
            pl.BlockSpec((1, N_KV_HEADS, ATT_HEAD_DIM, seq), lambda b, j: (b, 0, 0, 0)),
            pl.BlockSpec((1, ATT_DIM, tq), lambda b, j: (b, 0, j)),
        ],
        out_specs=pl.BlockSpec((1, tq, ATT_DIM), lambda b, j: (b, j, 0)),
        out_shape=jax.ShapeDtypeStruct((bsz, seq, ATT_DIM), BF16),
        scratch_shapes=[pltpu.VMEM((seq, tq), I32), pltpu.VMEM((seq, tq), F32),
                        pltpu.VMEM((1, tq), I32), pltpu.VMEM((ATT_DIM, tq), F32)],
        compiler_params=pltpu.CompilerParams(dimension_semantics=("parallel", "parallel"),
                                             vmem_limit_bytes=VMEM_LIMIT_BYTES),
        name="attn",
    )(ki, qit, wit, k4, vt4, qt)


def _ssd_kernel(xbc_ref, z_ref, dt_ref, dtt_ref, cw_ref, cb_ref, alog_ref, alogt_ref, e_ref,
                dx_ref, nw_ref, out_ref, ubuf, state, ybuf):
    L = CHUNK
    hi = lax.Precision.HIGHEST
    hpg = SSD_HEADS // SSD_GROUPS
    gw = hpg * SSD_HEAD_DIM

    @pl.when(pl.program_id(1) == 0)
    def _():
        ubuf[0:SUBLANES, :] = jnp.zeros((SUBLANES, XBC_DIM), F32)
        state[...] = jnp.zeros_like(state)

    ubuf[SUBLANES:SUBLANES + L, :] = xbc_ref[0]
    conv = jnp.broadcast_to(cb_ref[...], (L, XBC_DIM))
    for j in range(CONV_W):
        conv = conv + cw_ref[j:j + 1, :] * ubuf[pl.ds(SUBLANES - (CONV_W - 1) + j, L), :]
    ubuf[0:SUBLANES, :] = ubuf[L:L + SUBLANES, :]
    act = conv * _sigmoid(conv)
    xs = act[:, :D_INNER]
    bm = act[:, D_INNER:D_INNER + BC_DIM].astype(BF16)
    cm = act[:, D_INNER + BC_DIM:].astype(BF16)

    row = lax.broadcasted_iota(I32, (L, L), 0)
    col = lax.broadcasted_iota(I32, (L, L), 1)
    lower = row >= col
    dt = dt_ref[0]
    adt = dt * (-jnp.exp(alog_ref[...]))
    adt_t = dtt_ref[0] * (-jnp.exp(alogt_ref[...]))
    a_cum = jnp.dot(jnp.where(lower, 1.0, 0.0), adt, precision=hi, preferred_element_type=F32)
    a_cum_t = jnp.dot(adt_t, jnp.where(row <= col, 1.0, 0.0), precision=hi,
                      preferred_element_type=F32)
    e = e_ref[...]
    a_cum_x = jnp.dot(a_cum, e, precision=hi, preferred_element_type=F32)
    dt_x = jnp.dot(dt, e, precision=hi, preferred_element_type=F32)
    a_last_x = a_cum_x[L - 1:L, :]
    xf = xs * dt_x
    xf_b = xf.astype(BF16)
    xd_b = (xf * jnp.exp(a_last_x - a_cum_x)).astype(BF16)
    ea_x = jnp.exp(a_cum_x)
    chunk_decay_x = jnp.exp(a_last_x)

    lane = lax.broadcasted_iota(I32, (L, LANES), 1)
    for g in range(SSD_GROUPS):
        bg = bm[:, g * D_STATE:(g + 1) * D_STATE]
        cg = cm[:, g * D_STATE:(g + 1) * D_STATE]
        gs = slice(g * gw, (g + 1) * gw)
        cb = lax.dot_general(cg, bg, (((1,), (1,)), ((), ())), preferred_element_type=F32)
        st_prev = state[:, gs]
        y_off = jnp.dot(cg, st_prev.astype(BF16), preferred_element_type=F32) * ea_x[:, gs]
        st_new = lax.dot_general(bg, xd_b[:, gs], (((0,), (0,)), ((), ())),
                                 preferred_element_type=F32)
        state[:, gs] = st_prev * chunk_decay_x[:, gs] + st_new
        for pr in range(hpg // 2):
            h0 = g * hpg + 2 * pr
            xs_pair = xf_b[:, h0 * SSD_HEAD_DIM:(h0 + 2) * SSD_HEAD_DIM]
            ys = []
            for h in (h0, h0 + 1):
                seg = a_cum[:, h:h + 1] - a_cum_t[h:h + 1, :]
                lmat = jnp.exp(jnp.where(lower, seg, -jnp.inf))
                ys.append(jnp.dot((cb * lmat).astype(BF16), xs_pair, preferred_element_type=F32))
            ybuf[:, h0 * SSD_HEAD_DIM:(h0 + 2) * SSD_HEAD_DIM] = (
                jnp.where(lane < SSD_HEAD_DIM, ys[0], ys[1])
                + y_off[:, 2 * pr * SSD_HEAD_DIM:(2 * pr + 2) * SSD_HEAD_DIM])

    y = ybuf[...] + dx_ref[...] * xs
    zf = z_ref[0].astype(F32)
    yz = y * (zf * _sigmoid(zf))
    for g in range(SSD_GROUPS):
        gs = slice(g * gw, (g + 1) * gw)
        yg = yz[:, gs]
        msq = jnp.mean(yg * yg, axis=-1, keepdims=True)
        out_ref[0, :, gs] = (yg * lax.rsqrt(msq + LN_EPS) * nw_ref[:, gs]).astype(BF16)


def _ssd_call(xbc, z, dt, dtt, cw, cb, alog, alogt, e, dx, nw):
    bsz, seq, _ = xbc.shape
    L = CHUNK
    return pl.pallas_call(
        _ssd_kernel,
        grid=(bsz, seq // L),
        in_specs=[
            pl.BlockSpec((1, L, XBC_DIM), lambda b, c: (b, c, 0)),
            pl.BlockSpec((1, L, D_INNER), lambda b, c: (b, c, 0)),
            pl.BlockSpec((1, L, SSD_HEADS), lambda b, c: (b, c, 0)),
            pl.BlockSpec((1, SSD_HEADS, L), lambda b, c: (b, 0, c)),
            _const_spec(cw.shape), _const_spec(cb.shape), _const_spec(alog.shape),
            _const_spec(alogt.shape), _const_spec(e.shape), _const_spec(dx.shape),
            _const_spec(nw.shape),
        ],
        out_specs=pl.BlockSpec((1, L, D_INNER), lambda b, c: (b, c, 0)),
        out_shape=jax.ShapeDtypeStruct((bsz, seq, D_INNER), BF16),
        scratch_shapes=[pltpu.VMEM((L + SUBLANES, XBC_DIM), F32),
                        pltpu.VMEM((D_STATE, D_INNER), F32),
                        pltpu.VMEM((L, D_INNER), F32)],
        compiler_params=pltpu.CompilerParams(dimension_semantics=("parallel", "arbitrary"),
                                             vmem_limit_bytes=VMEM_LIMIT_BYTES),
        name="ssd",
    )(xbc, z, dt, dtt, cw, cb, alog, alogt, e, dx, nw)


def _layer_norm(v, g, b):
    mu = jnp.mean(v, axis=-1, keepdims=True)
    d = v - mu
    var = jnp.mean(d * d, axis=-1, keepdims=True)
    return d * lax.rsqrt(var + LN_EPS) * g + b


def _mix_kernel(att_ref, ssd_ref, g_ref, x_ref, wa_ref, ws_ref, wo_ref, l1g_ref, l1b_ref,
                wu_ref, wd_ref, l2g_ref, l2b_ref, o_ref, *, alpha, ff_chunk):
    ga = g_ref[:, :D_MODEL].astype(F32)
    gs = g_ref[:, D_MODEL:].astype(F32)
    mixed = (_sigmoid(ga) * jnp.dot(att_ref[...], wa_ref[...], preferred_element_type=F32)
             + _sigmoid(gs) * jnp.dot(ssd_ref[...], ws_ref[...], preferred_element_type=F32))
    proj = jnp.dot(mixed.astype(BF16), wo_ref[...], preferred_element_type=F32)
    h = _layer_norm(alpha * x_ref[...] + proj, l1g_ref[...], l1b_ref[...])
    hb = h.astype(BF16)
    ff = jnp.zeros_like(h)
    for c in range(D_FF // ff_chunk):
        cs = slice(c * ff_chunk, (c + 1) * ff_chunk)
        u = jnp.maximum(jnp.dot(hb, wu_ref[:, cs], preferred_element_type=F32), 0.0)
        ff = ff + jnp.dot((u * u).astype(BF16), wd_ref[cs, :], preferred_element_type=F32)
    o_ref[...] = _layer_norm(alpha * h + ff, l2g_ref[...], l2b_ref[...])


def _mix_call(att, ssd, gates, x2, wa, ws, wo, l1g, l1b, wu, wd, l2g, l2b, alpha, tm):
    m = x2.shape[0]
    row = lambda n: pl.BlockSpec((tm, n), lambda i: (i, 0))
    consts = (wa, ws, wo, l1g, l1b, wu, wd, l2g, l2b)
    return pl.pallas_call(
        functools.partial(_mix_kernel, alpha=alpha, ff_chunk=1024),
        grid=(m // tm,),
        in_specs=[row(ATT_DIM), row(D_INNER), row(2 * D_MODEL), row(D_MODEL)]
                 + [_const_spec(c.shape) for c in consts],
        out_specs=row(D_MODEL),
        out_shape=jax.ShapeDtypeStruct((m, D_MODEL), F32),
        compiler_params=pltpu.CompilerParams(dimension_semantics=("parallel",),
                                             vmem_limit_bytes=VMEM_LIMIT_BYTES),
        name="mix",
    )(att, ssd, gates, x2, *consts)


def _rope_tables(seq):
    half = ROT_DIM // 2
    inv = ROPE_THETA ** (-jnp.arange(0, ROT_DIM, 2, dtype=F32) / ROT_DIM)
    ang = jnp.arange(seq, dtype=F32)[:, None] * inv[None, :]
    cos, sin = jnp.cos(ang), jnp.sin(ang)
    pad = jnp.zeros((seq, ATT_HEAD_DIM - ROT_DIM), F32)
    zero = jnp.zeros((seq, half), F32)
    rc = jnp.concatenate([cos, cos, pad + 1.0], -1)
    rs1 = jnp.concatenate([-sin, zero, pad], -1)
    rs2 = jnp.concatenate([zero, sin, pad], -1)
    rep = LANES // ATT_HEAD_DIM
    return tuple(jnp.tile(t, (1, rep)) for t in (rc, rs1, rs2))


def _layer(x, tables, topk, alpha, w_in, q_norm_w, w_uq, w_iq, k_idx_norm_g, k_idx_norm_b,
           conv_w, conv_b, dt_bias, a_log, d_skip, ssd_norm_w, w_attn_branch, w_ssd_branch,
           w_out, ln1_g, ln1_b, w_up, w_down, ln2_g, ln2_b):
    bsz, seq, _ = x.shape
    m = bsz * seq
    x2 = x.reshape(m, D_MODEL)

    offs = [0]
    for s in IN_SIZES:
        offs.append(offs[-1] + s)
    seg = lambda i: w_in[:, offs[i]:offs[i + 1]]
    pad = jnp.zeros((D_MODEL, SMALL_COLS - TAIL_OFF - TAIL_END), F32)
    wa = jnp.concatenate([seg(0), seg(1), seg(2), seg(3), seg(4), seg(7), pad], 1).astype(BF16)
    wz, wx, wg = seg(5).astype(BF16), seg(6).astype(BF16), seg(8).astype(BF16)

    def lane_row(v, off):
        return jnp.zeros((1, LANES), F32).at[0, off:off + v.shape[0]].set(v)

    q, k, v, qi, misc, z, xbc, gates = _proj_call(
        x2, wa, wz, wx, wg, q_norm_w.reshape(1, Q_LORA), w_uq.astype(BF16), w_iq.astype(BF16),
        lane_row(k_idx_norm_g, 0), lane_row(k_idx_norm_b, 0), lane_row(dt_bias, DT_OFF),
        *tables, seq=seq, tm=256)

    qt = q.reshape(bsz, seq, ATT_DIM).transpose(0, 2, 1)
    qit = qi.reshape(bsz, seq, IDX_HEADS * IDX_DIM).transpose(0, 2, 1)
    k4 = k.reshape(bsz, seq, N_KV_HEADS, ATT_HEAD_DIM).transpose(0, 2, 1, 3)
    vt4 = v.reshape(bsz, seq, N_KV_HEADS, ATT_HEAD_DIM).transpose(0, 2, 3, 1)
    misc3 = misc.reshape(bsz, seq, LANES)
    ki = misc3[:, :, :IDX_DIM].astype(BF16)
    wit = misc3[:, :, WI_OFF:DT_OFF].transpose(0, 2, 1)
    att = _attn_call(ki, qit, wit, k4, vt4, qt, topk, tq=256)

    dt = misc3[:, :, DT_OFF:TAIL_END]
    expand = (jnp.arange(D_INNER)[None, :] // SSD_HEAD_DIM == jnp.arange(SSD_HEADS)[:, None]).astype(F32)
    ssd = _ssd_call(xbc.reshape(bsz, seq, XBC_DIM), z.reshape(bsz, seq, D_INNER), dt,
                    dt.transpose(0, 2, 1), conv_w, conv_b.reshape(1, XBC_DIM),
                    a_log.reshape(1, SSD_HEADS), a_log.reshape(SSD_HEADS, 1), expand,
                    jnp.repeat(d_skip, SSD_HEAD_DIM).reshape(1, D_INNER),
                    ssd_norm_w.reshape(1, D_INNER))

    r = lambda t: t.reshape(1, -1)
    out = _mix_call(att.reshape(m, ATT_DIM), ssd.reshape(m, D_INNER), gates, x2,
                    w_attn_branch.astype(BF16), w_ssd_branch.astype(BF16), w_out.astype(BF16),
                    r(ln1_g), r(ln1_b), w_up.astype(BF16), w_down.astype(BF16), r(ln2_g), r(ln2_b),
                    alpha, tm=256)
    return out.reshape(bsz, seq, D_MODEL)


def kernel(x, w_in, q_norm_w, w_uq, w_iq, k_idx_norm_g, k_idx_norm_b, conv_w, conv_b, dt_bias,
           a_log, d_skip, ssd_norm_w, w_attn_branch, w_ssd_branch, w_out, ln1_g, ln1_b, w_up,
           w_down, ln2_g, ln2_b):
    depth = w_in.shape[0]
    seq = x.shape[1]
    topk = min(TOPK_MAX, seq // 4)
    alpha = (2 * depth) ** 0.25
    tables = _rope_tables(seq)
    h = x
    for l in range(depth):
        h = _layer(h, tables, topk, alpha, w_in[l], q_norm_w[l], w_uq[l], w_iq[l], k_idx_norm_g[l],
                   k_idx_norm_b[l], conv_w[l], conv_b[l], dt_bias[l], a_log[l], d_skip[l],
                   ssd_norm_w[l], w_attn_branch[l], w_ssd_branch[l], w_out[l], ln1_g[l], ln1_b[l],
                   w_up[l], w_down[l], ln2_g[l], ln2_b[l])
    return h
```

```python
import functools

import jax
import jax.numpy as jnp
from jax import lax
from jax.experimental import pallas as pl
from jax.experimental.pallas import tpu as pltpu

F32 = jnp.float32
BF16 = jnp.bfloat16
I32 = jnp.int32

D_MODEL = 1024
N_ATT_HEADS = 16
ATT_HEAD_DIM = 64
N_KV_HEADS = 4
ATT_DIM = N_ATT_HEADS * ATT_HEAD_DIM
KV_DIM = N_KV_HEADS * ATT_HEAD_DIM
Q_LORA = 256
IDX_HEADS = 8
IDX_DIM = 64
TOPK_MAX = 256
ROT_DIM = ATT_HEAD_DIM // 4
ROPE_THETA = 500000.0
D_INNER = 2 * D_MODEL
SSD_HEAD_DIM = 64
SSD_HEADS = D_INNER // SSD_HEAD_DIM
SSD_GROUPS = 4
D_STATE = 128
CONV_W = 4
CHUNK = 128
BC_DIM = SSD_GROUPS * D_STATE
XBC_DIM = D_INNER + 2 * BC_DIM
D_FF = 4 * D_MODEL
IN_SIZES = (Q_LORA, KV_DIM, KV_DIM, IDX_DIM, IDX_HEADS, D_INNER, XBC_DIM, SSD_HEADS, 2 * D_MODEL)
LN_EPS = 1e-5

LANES = 128
SUBLANES = 8
VMEM_LIMIT_BYTES = 56 * 1024 * 1024

SEQ_TILE = 256

SMALL_COLS = Q_LORA + 2 * KV_DIM + LANES
TAIL_OFF = Q_LORA + 2 * KV_DIM
WI_OFF = IDX_DIM
DT_OFF = IDX_DIM + IDX_HEADS
TAIL_END = DT_OFF + SSD_HEADS

PERM_STRIDE = CHUNK // SUBLANES
BF16_SUBLANES = 16
V_ROWS = ATT_HEAD_DIM + BF16_SUBLANES
SUM_ROW = ATT_HEAD_DIM
LOG2_E = 1.4426950408889634
INT_MIN = -(2 ** 31)


def _sigmoid(v):
    return 0.5 + 0.5 * jnp.tanh(0.5 * v)


def _silu(v):
    h = 0.5 * v
    return h + h * jnp.tanh(h)


def _const_spec(shape):
    zeros = (0,) * len(shape)
    return pl.BlockSpec(shape, lambda *_: zeros, pipeline_mode=pl.Buffered(1))


def _proj_kernel(x_ref, wa_ref, wz_ref, wx_ref, wg_ref, qnw_ref, wuq_ref, wiq_ref,
                 kng_ref, knb_ref, dtb_ref, rc_ref, rs1_ref, rs2_ref,
                 qt_ref, qit_ref, k4_ref, vt_ref, ki_ref, tail_ref, tailt_ref,
                 z_ref, xbc_ref, g_ref, perm_ref):
    xb = x_ref[0].astype(BF16)
    rc, rs1, rs2 = rc_ref[...], rs1_ref[...], rs2_ref[...]
    hd = ATT_HEAD_DIM

    def rope(t):
        half = ROT_DIM // 2
        return (t * rc + pltpu.roll(t, LANES - half, 1) * rs1 + pltpu.roll(t, half, 1) * rs2)

    pa = jnp.dot(xb, wa_ref[...], preferred_element_type=F32)

    c_q = pa[:, :Q_LORA]
    ms = jnp.mean(c_q * c_q, axis=-1, keepdims=True)
    cq = (c_q * lax.rsqrt(ms + LN_EPS) * qnw_ref[...]).astype(BF16)

    scale = ATT_HEAD_DIM ** -0.5 * LOG2_E
    qf = jnp.dot(cq, wuq_ref[...], preferred_element_type=F32)
    for j in range(ATT_DIM // LANES):
        sl = slice(j * LANES, (j + 1) * LANES)
        qt_ref[0, sl, :] = (rope(qf[:, sl]) * scale).T.astype(BF16)
    qif = jnp.dot(cq, wiq_ref[...], preferred_element_type=F32)
    iscale = IDX_DIM ** -0.5
    for j in range(IDX_HEADS * IDX_DIM // LANES):
        sl = slice(j * LANES, (j + 1) * LANES)
        qit_ref[0, sl, :] = (rope(qif[:, sl]) * iscale).T.astype(BF16)
    for j in range(KV_DIM // LANES):
        kr = rope(pa[:, Q_LORA + j * LANES:Q_LORA + (j + 1) * LANES])
        k4_ref[0, 2 * j] = kr[:, :hd].astype(BF16)
        k4_ref[0, 2 * j + 1] = kr[:, hd:].astype(BF16)
        vtr = pa[:, Q_LORA + KV_DIM + j * LANES:Q_LORA + KV_DIM + (j + 1) * LANES].T
        sum_rows = jnp.where(lax.broadcasted_iota(I32, (V_ROWS - hd, vtr.shape[1]), 0) == 0,
                             1.0, 0.0).astype(BF16)
        vt_ref[0, 2 * j, 0] = jnp.concatenate([vtr[:hd, :].astype(BF16), sum_rows], axis=0)
        vt_ref[0, 2 * j + 1, 0] = jnp.concatenate([vtr[hd:, :].astype(BF16), sum_rows], axis=0)

    tail = pa[:, TAIL_OFF:TAIL_OFF + LANES]
    lane = lax.broadcasted_iota(I32, tail.shape, 1)
    is_ki = lane < IDX_DIM
    mu = jnp.sum(jnp.where(is_ki, tail, 0.0), axis=-1, keepdims=True) * (1.0 / IDX_DIM)
    dv = jnp.where(is_ki, tail - mu, 0.0)
    var = jnp.sum(dv * dv, axis=-1, keepdims=True) * (1.0 / IDX_DIM)
    ki = rope(dv * lax.rsqrt(var + LN_EPS) * kng_ref[...] + knb_ref[...])
    wi = tail * (IDX_HEADS ** -0.5)
    dtr = tail + dtb_ref[...]
    dt = jnp.maximum(dtr, 0.0) + jnp.log1p(jnp.exp(-jnp.abs(dtr)))
    ki_ref[0] = ki[:, :IDX_DIM].astype(BF16)
    packed = jnp.where(is_ki, ki, jnp.where(lane < DT_OFF, wi, jnp.where(lane < TAIL_END, dt, 0.0)))
    tailt_ref[0] = packed.T

    g_ref[0] = jnp.dot(xb, wg_ref[...], preferred_element_type=F32).astype(BF16)

    xv = x_ref[0]
    ntile = D_MODEL // LANES
    for j in range(ntile):
        perm_ref[j] = xv[:, j * LANES:(j + 1) * LANES]
    xp = jnp.concatenate(
        [jnp.concatenate([perm_ref[j, pl.ds(k * CHUNK + i, SUBLANES, stride=PERM_STRIDE), :]
                          for j in range(ntile)], axis=1)
         for k in range(xv.shape[0] // CHUNK) for i in range(PERM_STRIDE)], axis=0).astype(BF16)
    dtr_p = jnp.dot(xp, wa_ref[:, TAIL_OFF:TAIL_OFF + LANES], preferred_element_type=F32) + dtb_ref[...]
    dt_p = jnp.maximum(dtr_p, 0.0) + jnp.log1p(jnp.exp(-jnp.abs(dtr_p)))
    tail_ref[0] = jnp.where((lane >= DT_OFF) & (lane < TAIL_END), dt_p, 0.0)
    z_ref[0] = jnp.dot(xp, wz_ref[...], preferred_element_type=F32).astype(BF16)
    xbc_ref[0] = jnp.dot(xp, wx_ref[...], preferred_element_type=F32)


def _proj_call(x, wa, wz, wx, wg, qnw, wuq, wiq, kng, knb, dtb, rc, rs1, rs2):
    bsz, seq, _ = x.shape
    tm = SEQ_TILE
    nsb = seq // tm
    row = lambda n: pl.BlockSpec((1, tm, n), lambda b, i: (b, i, 0))
    col = lambda n: pl.BlockSpec((1, n, tm), lambda b, i: (b, 0, i))
    tab = pl.BlockSpec((tm, LANES), lambda b, i: (i, 0))
    sds = jax.ShapeDtypeStruct
    out_shapes = (
        sds((bsz, ATT_DIM, seq), BF16),
        sds((bsz, IDX_HEADS * IDX_DIM, seq), BF16),
        sds((bsz, N_KV_HEADS, seq, ATT_HEAD_DIM), BF16),
        sds((bsz, N_KV_HEADS, nsb, V_ROWS, tm), BF16),
        sds((bsz, seq, IDX_DIM), BF16),
        sds((bsz, seq, LANES), F32),
        sds((bsz, LANES, seq), F32),
        sds((bsz, seq, D_INNER), BF16),
        sds((bsz, seq, XBC_DIM), F32),
        sds((bsz, seq, 2 * D_MODEL), BF16),
    )
    out_specs = [
        col(ATT_DIM), col(IDX_HEADS * IDX_DIM),
        pl.BlockSpec((1, N_KV_HEADS, tm, ATT_HEAD_DIM), lambda b, i: (b, 0, i, 0)),
        pl.BlockSpec((1, N_KV_HEADS, 1, V_ROWS, tm), lambda b, i: (b, 0, i, 0, 0)),
        row(IDX_DIM), row(LANES), col(LANES), row(D_INNER), row(XBC_DIM), row(2 * D_MODEL),
    ]
    consts = (wa, wz, wx, wg, qnw, wuq, wiq, kng, knb, dtb)
    return pl.pallas_call(
        _proj_kernel,
        grid=(bsz, nsb),
        in_specs=[row(D_MODEL)] + [_const_spec(c.shape) for c in consts] + [tab, tab, tab],
        out_specs=out_specs,
        out_shape=out_shapes,
        scratch_shapes=[pltpu.VMEM((D_MODEL // LANES, tm, LANES), F32)],
        compiler_params=pltpu.CompilerParams(dimension_semantics=("parallel", "parallel"),
                                             vmem_limit_bytes=VMEM_LIMIT_BYTES),
        name="proj",
    )(x, *consts, rc, rs1, rs2)


COUNT_ROWS = 32


def _fold_rows(v, op, rows=32):
    while v.shape[0] > rows:
        half = v.shape[0] // 2
        v = op(v[:half], v[half:])
    return v


def _attn_kernel(ki_ref, qit_ref, wit_ref, k_ref, vt_ref, qt_ref, o_ref,
                 score_ref, bias_ref, cut_ref, ot_ref, m_ref, acc_ref, *, topk):
    tq = qt_ref.shape[2]
    tk = tq
    qblk = pl.program_id(1)
    nch = qblk + 1
    sub = lax.broadcasted_iota(I32, (tk, tq), 0)
    tpos = qblk * tq + lax.broadcasted_iota(I32, (tk, tq), 1)

    def score_chunk(c, carry):
        off = pl.multiple_of(c * tk, tk)
        kic = ki_ref[0, pl.ds(off, tk), :]
        acc = jnp.zeros((tk, tq), F32)
        for hh in range(IDX_HEADS):
            sc = jnp.dot(kic, qit_ref[0, hh * IDX_DIM:(hh + 1) * IDX_DIM, :],
                         preferred_element_type=F32)
            acc = acc + wit_ref[0, WI_OFF + hh:WI_OFF + hh + 1, :] * jnp.maximum(sc, 0.0)
        score_ref[c] = jnp.where(off + sub <= tpos, acc, -jnp.inf)
        return carry

    lax.fori_loop(0, nch, score_chunk, 0)

    def count(pred):
        def body(c, part):
            hit = jnp.where(pred(score_ref[c], c), 1, 0)
            return part + jnp.sum(hit.reshape(tk // COUNT_ROWS, COUNT_ROWS, tq), axis=0)

        part = lax.fori_loop(0, nch, body, jnp.zeros((COUNT_ROWS, tq), I32))
        return jnp.sum(part, axis=0, keepdims=True)

    def ordered_float(u):
        key = u ^ jnp.int32(INT_MIN)
        return pltpu.bitcast(jnp.where(key < 0, key ^ jnp.int32(0x7FFFFFFF), key), F32)

    def search(i, thr_u):
        cand = thr_u | lax.shift_left(jnp.int32(1), 31 - i)
        cand_f = ordered_float(cand)
        return jnp.where(count(lambda sc, c: sc >= cand_f) >= topk, cand, thr_u)

    thr = ordered_float(lax.fori_loop(0, 32, search, jnp.zeros((1, tq), I32)))
    thr = jnp.where(tpos[0:1, :] < topk, -jnp.inf, thr)

    need = topk - count(lambda sc, c: sc > thr)
    n_eq = count(lambda sc, c: sc == thr)
    nbits = (k_ref.shape[2] - 1).bit_length()
    cut_ref[...] = jnp.full((1, tq), 2 ** nbits, I32)
    excess = jnp.where((n_eq > need) & (thr != -jnp.inf), 1, 0)

    @pl.when(jnp.max(excess) > 0)
    def _():
        def tie_search(i, cut):
            cand = cut | lax.shift_left(jnp.int32(1), nbits - 1 - i)
            cnt = count(lambda sc, c: (sc == thr) & (c * tk + sub < cand))
            return jnp.where(cnt < need, cand, cut)

        cut_ref[...] = lax.fori_loop(0, nbits, tie_search, jnp.zeros((1, tq), I32))

    cut = cut_ref[...]

    def bias_chunk(c, carry):
        sc = score_ref[c]
        spos = c * tk + sub
        sel = jnp.where(sc > thr, 1, jnp.where(sc == thr, jnp.where(spos <= cut, 1, 0), 0))
        sel = jnp.where(spos <= tpos, sel, 0)
        bias_ref[c] = jnp.where(sel > 0, 0.0, -jnp.inf)
        return carry

    lax.fori_loop(0, nch, bias_chunk, 0)

    grp = N_ATT_HEADS // N_KV_HEADS
    hd = ATT_HEAD_DIM
    m_ref[...] = jnp.full(m_ref.shape, -jnp.inf, F32)
    acc_ref[...] = jnp.zeros(acc_ref.shape, F32)

    def chunk(c, carry):
        off = pl.multiple_of(c * tk, tk)

        def scores_of(g):
            kc = k_ref[0, g, pl.ds(off, tk), :]
            return [jnp.dot(kc, qt_ref[0, (g * grp + i) * hd:(g * grp + i + 1) * hd, :],
                            preferred_element_type=F32) for i in range(grp)]

        def softmax_of(g, scores):
            probs, alphas = [], []
            for i in range(grp):
                h = g * grp + i
                m = m_ref[h:h + 1, :]
                s = scores[i] + bias_ref[c]
                m_new = jnp.maximum(m, jnp.max(_fold_rows(s, jnp.maximum), axis=0, keepdims=True))
                m_safe = jnp.where(m_new == -jnp.inf, 0.0, m_new)
                alphas.append(jnp.exp2(m - m_safe))
                probs.append(jnp.exp2(s - m_safe).astype(BF16))
                m_ref[h:h + 1, :] = m_new
            return probs, alphas

        def pv_of(g, probs, alphas):
            vc = vt_ref[0, g, c]
            for i in range(grp):
                h = g * grp + i
                acc_ref[h] = alphas[i] * acc_ref[h] + jnp.dot(vc, probs[i],
                                                              preferred_element_type=F32)

        scores = scores_of(0)
        pending = None
        for g in range(N_KV_HEADS):
            nxt = scores_of(g + 1) if g + 1 < N_KV_HEADS else None
            cur = softmax_of(g, scores)
            if pending is not None:
                pv_of(g - 1, *pending)
            scores, pending = nxt, cur
        pv_of(N_KV_HEADS - 1, *pending)
        return carry

    lax.fori_loop(0, nch, chunk, 0)
    for h in range(N_ATT_HEADS):
        ot_ref[h * hd:(h + 1) * hd, :] = acc_ref[h, :hd, :] / acc_ref[h, SUM_ROW:SUM_ROW + 1, :]
    o_ref[0] = ot_ref[...].T.astype(BF16)


def _attn_call(ki, qit, tailt, k4, vt5, qt, topk):
    bsz, seq, _ = ki.shape
    tq = vt5.shape[-1]
    nch = seq // tq
    assert tq >= topk and vt5.shape[2] == nch
    return pl.pallas_call(
        functools.partial(_attn_kernel, topk=topk),
        grid=(bsz, nch),
        in_specs=[
            pl.BlockSpec((1, seq, IDX_DIM), lambda b, j: (b, 0, 0)),
            pl.BlockSpec((1, IDX_HEADS * IDX_DIM, tq), lambda b, j: (b, 0, j)),
            pl.BlockSpec((1, LANES, tq), lambda b, j: (b, 0, j)),
            pl.BlockSpec((1, N_KV_HEADS, seq, ATT_HEAD_DIM), lambda b, j: (b, 0, 0, 0)),
            pl.BlockSpec((1, N_KV_HEADS, nch, V_ROWS, tq), lambda b, j: (b, 0, 0, 0, 0)),
            pl.BlockSpec((1, ATT_DIM, tq), lambda b, j: (b, 0, j)),
        ],
        out_specs=pl.BlockSpec((1, tq, ATT_DIM), lambda b, j: (b, j, 0)),
        out_shape=jax.ShapeDtypeStruct((bsz, seq, ATT_DIM), BF16),
        scratch_shapes=[pltpu.VMEM((nch, tq, tq), F32), pltpu.VMEM((nch, tq, tq), F32),
                        pltpu.VMEM((1, tq), I32), pltpu.VMEM((ATT_DIM, tq), F32),
                        pltpu.VMEM((N_ATT_HEADS, tq), F32),
                        pltpu.VMEM((N_ATT_HEADS, V_ROWS, tq), F32)],
        compiler_params=pltpu.CompilerParams(dimension_semantics=("parallel", "parallel"),
                                             vmem_limit_bytes=VMEM_LIMIT_BYTES),
        name="attn",
    )(ki, qit, tailt, k4, vt5, qt)


def _split2(v):
    hi = v.astype(BF16)
    mid = (v - hi.astype(F32)).astype(BF16)
    return jnp.concatenate([hi, mid], axis=1)


def _ssd_kernel(xbc_ref, z_ref, tail_ref, cw_ref, cb_ref, alog_ref, e2_ref,
                dx_ref, nw_ref, out_ref, carry, state, ybuf):
    L = CHUNK
    hpg = SSD_HEADS // SSD_GROUPS
    gw = hpg * SSD_HEAD_DIM

    nshift = CONV_W - 1
    carry_rows = nshift * SUBLANES

    @pl.when(pl.program_id(1) == 0)
    def _():
        carry[...] = jnp.zeros_like(carry)
        state[...] = jnp.zeros_like(state)

    u = xbc_ref[0]
    sub8 = lax.broadcasted_iota(I32, (SUBLANES, XBC_DIM), 0)
    wrapped = []
    for k in range(nshift):
        rows = slice(L - carry_rows + k * SUBLANES, L - carry_rows + (k + 1) * SUBLANES)
        merged = jnp.where(sub8 == SUBLANES - 1, carry[k * SUBLANES:(k + 1) * SUBLANES, :], u[rows, :])
        wrapped.append(pltpu.roll(merged, 1, 0))
    carry[...] = u[L - carry_rows:, :]
    conv = cb_ref[...] + cw_ref[CONV_W - 1:CONV_W, :] * u
    for d in range(1, CONV_W):
        shifted = jnp.concatenate(wrapped[nshift - d:] + [u[:L - d * SUBLANES, :]], axis=0)
        conv = conv + cw_ref[CONV_W - 1 - d:CONV_W - d, :] * shifted
    act = _silu(conv)
    xs = act[:, :D_INNER]

    row = lax.broadcasted_iota(I32, (L, L), 0)
    col = lax.broadcasted_iota(I32, (L, L), 1)
    time_of = lambda r: PERM_STRIDE * (r & (SUBLANES - 1)) + (r >> (SUBLANES.bit_length() - 1))
    lower = time_of(row) >= time_of(col)
    tail = tail_ref[0]
    adt = tail * (-jnp.exp(alog_ref[...]) * LOG2_E)
    hi = adt.astype(BF16)
    r1 = adt - hi.astype(F32)
    mid = r1.astype(BF16)
    lo = (r1 - mid.astype(F32)).astype(BF16)
    ones_l = jnp.where(lower, 1.0, 0.0).astype(BF16)
    a_cum = jnp.dot(jnp.concatenate([ones_l, ones_l, ones_l], axis=1),
                    jnp.concatenate([hi, mid, lo], axis=0), preferred_element_type=F32)
    a_cum_t = a_cum.T
    e2 = e2_ref[...]
    a_cum_x = jnp.dot(_split2(a_cum), e2, preferred_element_type=F32)
    dt_x = jnp.dot(_split2(tail), e2, preferred_element_type=F32)
    a_last_x = a_cum_x[L - 1:L, :]
    xf = xs * dt_x
    xf_b = xf.astype(BF16)
    xd_b = (xf * jnp.exp2(a_last_x - a_cum_x)).astype(BF16)
    ea_x = jnp.exp2(a_cum_x)
    chunk_decay_x = jnp.exp2(a_last_x)

    lane = lax.broadcasted_iota(I32, (L, LANES), 1)
    for g in range(SSD_GROUPS):
        b_f = act[:, D_INNER + g * D_STATE:D_INNER + (g + 1) * D_STATE]
        bg = b_f.astype(BF16)
        bg_t = b_f.T.astype(BF16)
        cg = act[:, D_INNER + BC_DIM + g * D_STATE:D_INNER + BC_DIM + (g + 1) * D_STATE].astype(BF16)
        gs = slice(g * gw, (g + 1) * gw)
        cb = lax.dot_general(cg, bg, (((1,), (1,)), ((), ())), preferred_element_type=F32)
        st_prev = state[:, gs]
        y_off = jnp.dot(cg, st_prev.astype(BF16), preferred_element_type=F32) * ea_x[:, gs]
        st_new = jnp.dot(bg_t, xd_b[:, gs], preferred_element_type=F32)
        state[:, gs] = st_prev * chunk_decay_x[:, gs] + st_new
        for pr in range(hpg // 2):
            h0 = g * hpg + 2 * pr
            xs_pair = xf_b[:, h0 * SSD_HEAD_DIM:(h0 + 2) * SSD_HEAD_DIM]
            ys = []
            for h in (h0, h0 + 1):
                seg = a_cum[:, DT_OFF + h:DT_OFF + h + 1] - a_cum_t[DT_OFF + h:DT_OFF + h + 1, :]
                lmat = jnp.exp2(jnp.where(lower, seg, -jnp.inf))
                ys.append(jnp.dot((cb * lmat).astype(BF16), xs_pair, preferred_element_type=F32))
            ybuf[h0 // 2] = (jnp.where(lane < SSD_HEAD_DIM, ys[0], ys[1])
                             + y_off[:, 2 * pr * SSD_HEAD_DIM:(2 * pr + 2) * SSD_HEAD_DIM])

    ntile = D_INNER // LANES
    y = jnp.concatenate([ybuf[j] for j in range(ntile)], axis=1) + dx_ref[...] * xs
    zf = z_ref[0].astype(F32)
    yz = y * _silu(zf)
    for g in range(SSD_GROUPS):
        gs = slice(g * gw, (g + 1) * gw)
        yg = yz[:, gs]
        msq = jnp.mean(yg * yg, axis=-1, keepdims=True)
        og = yg * lax.rsqrt(msq + LN_EPS) * nw_ref[:, gs]
        for j in range(gw // LANES):
            ybuf[g * (gw // LANES) + j] = og[:, j * LANES:(j + 1) * LANES]
    half = L // 2
    out_ref[0] = jnp.concatenate(
        [jnp.concatenate([ybuf[j, pl.ds(half * (t % 2) + t // 2, SUBLANES, stride=SUBLANES), :]
                          for j in range(ntile)], axis=1)
         for t in range(PERM_STRIDE)], axis=0).astype(BF16)


def _ssd_call(xbc, z, tail, cw, cb, alog, e2, dx, nw):
    bsz, seq, _ = xbc.shape
    L = CHUNK
    blk = lambda n: pl.BlockSpec((1, L, n), lambda b, c: (b, c, 0))
    consts = (cw, cb, alog, e2, dx, nw)
    return pl.pallas_call(
        _ssd_kernel,
        grid=(bsz, seq // L),
        in_specs=[blk(XBC_DIM), blk(D_INNER), blk(LANES)] + [_const_spec(c.shape) for c in consts],
        out_specs=blk(D_INNER),
        out_shape=jax.ShapeDtypeStruct((bsz, seq, D_INNER), BF16),
        scratch_shapes=[pltpu.VMEM(((CONV_W - 1) * SUBLANES, XBC_DIM), F32),
                        pltpu.VMEM((D_STATE, D_INNER), F32),
                        pltpu.VMEM((D_INNER // LANES, L, LANES), F32)],
        compiler_params=pltpu.CompilerParams(dimension_semantics=("parallel", "arbitrary"),
                                             vmem_limit_bytes=VMEM_LIMIT_BYTES),
        name="ssd",
    )(xbc, z, tail, *consts)


def _layer_norm(v, g, b):
    mu = jnp.mean(v, axis=-1, keepdims=True)
    d = v - mu
    var = jnp.mean(d * d, axis=-1, keepdims=True)
    return d * lax.rsqrt(var + LN_EPS) * g + b


def _mix_kernel(att_ref, ssd_ref, g_ref, x_ref, wa_ref, ws_ref, wo_ref, l1g_ref, l1b_ref,
                wu_ref, wd_ref, l2g_ref, l2b_ref, o_ref, *, alpha, ff_chunk):
    ga = g_ref[:, :D_MODEL].astype(F32)
    gs = g_ref[:, D_MODEL:].astype(F32)
    mixed = (_sigmoid(ga) * jnp.dot(att_ref[...], wa_ref[...], preferred_element_type=F32)
             + _sigmoid(gs) * jnp.dot(ssd_ref[...], ws_ref[...], preferred_element_type=F32))
    proj = jnp.dot(mixed.astype(BF16), wo_ref[...], preferred_element_type=F32)
    h = _layer_norm(alpha * x_ref[...] + proj, l1g_ref[...], l1b_ref[...])
    hb = h.astype(BF16)
    ff = jnp.zeros_like(h)
    for c in range(D_FF // ff_chunk):
        cs = slice(c * ff_chunk, (c + 1) * ff_chunk)
        u = jnp.maximum(jnp.dot(hb, wu_ref[:, cs], preferred_element_type=F32), 0.0)
        ff = ff + jnp.dot((u * u).astype(BF16), wd_ref[cs, :], preferred_element_type=F32)
    o_ref[...] = _layer_norm(alpha * h + ff, l2g_ref[...], l2b_ref[...])


def _mix_call(att, ssd, gates, x2, wa, ws, wo, l1g, l1b, wu, wd, l2g, l2b, alpha, tm):
    m = x2.shape[0]
    row = lambda n: pl.BlockSpec((tm, n), lambda i: (i, 0))
    consts = (wa, ws, wo, l1g, l1b, wu, wd, l2g, l2b)
    return pl.pallas_call(
        functools.partial(_mix_kernel, alpha=alpha, ff_chunk=1024),
        grid=(m // tm,),
        in_specs=[row(ATT_DIM), row(D_INNER), row(2 * D_MODEL), row(D_MODEL)]
                 + [_const_spec(c.shape) for c in consts],
        out_specs=row(D_MODEL),
        out_shape=jax.ShapeDtypeStruct((m, D_MODEL), F32),
        compiler_params=pltpu.CompilerParams(dimension_semantics=("parallel",),
                                             vmem_limit_bytes=VMEM_LIMIT_BYTES),
        name="mix",
    )(att, ssd, gates, x2, *consts)


def _rope_tables(seq):
    half = ROT_DIM // 2
    inv = ROPE_THETA ** (-jnp.arange(0, ROT_DIM, 2, dtype=F32) / ROT_DIM)
    ang = jnp.arange(seq, dtype=F32)[:, None] * inv[None, :]
    cos, sin = jnp.cos(ang), jnp.sin(ang)
    pad = jnp.zeros((seq, ATT_HEAD_DIM - ROT_DIM), F32)
    zero = jnp.zeros((seq, half), F32)
    rc = jnp.concatenate([cos, cos, pad + 1.0], -1)
    rs1 = jnp.concatenate([-sin, zero, pad], -1)
    rs2 = jnp.concatenate([zero, sin, pad], -1)
    rep = LANES // ATT_HEAD_DIM
    return tuple(jnp.tile(t, (1, rep)) for t in (rc, rs1, rs2))


def _lane_row(v, off):
    return jnp.zeros((1, LANES), F32).at[0, off:off + v.shape[0]].set(v)


def _layer(x, tables, topk, alpha, w_in, q_norm_w, w_uq, w_iq, k_idx_norm_g, k_idx_norm_b,
           conv_w, conv_b, dt_bias, a_log, d_skip, ssd_norm_w, w_attn_branch, w_ssd_branch,
           w_out, ln1_g, ln1_b, w_up, w_down, ln2_g, ln2_b):
    bsz, seq, _ = x.shape
    m = bsz * seq

    offs = [0]
    for s in IN_SIZES:
        offs.append(offs[-1] + s)
    seg = lambda i: w_in[:, offs[i]:offs[i + 1]]
    pad = jnp.zeros((D_MODEL, LANES - TAIL_END), F32)
    wa = jnp.concatenate([seg(0), seg(1), seg(2), seg(3), seg(4), seg(7), pad], 1).astype(BF16)
    wz, wx, wg = seg(5).astype(BF16), seg(6).astype(BF16), seg(8).astype(BF16)

    qt, qit, k4, vt5, ki, tail, tailt, z, xbc, gates = _proj_call(
        x, wa, wz, wx, wg, q_norm_w.reshape(1, Q_LORA), w_uq.astype(BF16), w_iq.astype(BF16),
        _lane_row(k_idx_norm_g, 0), _lane_row(k_idx_norm_b, 0), _lane_row(dt_bias, DT_OFF), *tables)

    att = _attn_call(ki, qit, tailt, k4, vt5, qt, topk)

    slot = jnp.arange(LANES)[:, None] - DT_OFF
    expand = (slot == jnp.arange(D_INNER)[None, :] // SSD_HEAD_DIM).astype(BF16)
    e2 = jnp.concatenate([expand, expand], axis=0)
    ssd = _ssd_call(xbc, z, tail, conv_w, conv_b.reshape(1, XBC_DIM), _lane_row(a_log, DT_OFF), e2,
                    jnp.repeat(d_skip, SSD_HEAD_DIM).reshape(1, D_INNER),
                    ssd_norm_w.reshape(1, D_INNER))

    r = lambda t: t.reshape(1, -1)
    out = _mix_call(att.reshape(m, ATT_DIM), ssd.reshape(m, D_INNER), gates.reshape(m, 2 * D_MODEL),
                    x.reshape(m, D_MODEL), w_attn_branch.astype(BF16), w_ssd_branch.astype(BF16),
                    w_out.astype(BF16), r(ln1_g), r(ln1_b), w_up.astype(BF16), w_down.astype(BF16),
                    r(ln2_g), r(ln2_b), alpha, tm=256)
    return out.reshape(bsz, seq, D_MODEL)


def kernel(x, w_in, q_norm_w, w_uq, w_iq, k_idx_norm_g, k_idx_norm_b, conv_w, conv_b, dt_bias,
           a_log, d_skip, ssd_norm_w, w_attn_branch, w_ssd_branch, w_out, ln1_g, ln1_b, w_up,
           w_down, ln2_g, ln2_b):
    depth = w_in.shape[0]
    seq = x.shape[1]
    topk = min(TOPK_MAX, seq // 4)
    alpha = (2 * depth) ** 0.25
    tables = _rope_tables(seq)
    h = x
    for l in range(depth):
        h = _layer(h, tables, topk, alpha, w_in[l], q_norm_w[l], w_uq[l], w_iq[l], k_idx_norm_g[l],
                   k_idx_norm_b[l], conv_w[l], conv_b[l], dt_bias[l], a_log[l], d_skip[l],
                   ssd_norm_w[l], w_attn_branch[l], w_ssd_branch[l], w_out[l], ln1_g[l], ln1_b[l],
                   w_up[l], w_down[l], ln2_g[l], ln2_b[l])
    return h
```

```python
import functools

import jax
import jax.numpy as jnp
from jax import lax
from jax.experimental import pallas as pl
from jax.experimental.pallas import tpu as pltpu

F32 = jnp.float32
BF16 = jnp.bfloat16
I32 = jnp.int32

D_MODEL = 1024
N_ATT_HEADS = 16
ATT_HEAD_DIM = 64
N_KV_HEADS = 4
ATT_DIM = N_ATT_HEADS * ATT_HEAD_DIM
KV_DIM = N_KV_HEADS * ATT_HEAD_DIM
Q_LORA = 256
IDX_HEADS = 8
IDX_DIM = 64
TOPK_MAX = 256
ROT_DIM = ATT_HEAD_DIM // 4
ROPE_THETA = 500000.0
D_INNER = 2 * D_MODEL
SSD_HEAD_DIM = 64
SSD_HEADS = D_INNER // SSD_HEAD_DIM
SSD_GROUPS = 4
D_STATE = 128
CONV_W = 4
CHUNK = 128
BC_DIM = SSD_GROUPS * D_STATE
XBC_DIM = D_INNER + 2 * BC_DIM
D_FF = 4 * D_MODEL
IN_SIZES = (Q_LORA, KV_DIM, KV_DIM, IDX_DIM, IDX_HEADS, D_INNER, XBC_DIM, SSD_HEADS, 2 * D_MODEL)
LN_EPS = 1e-5

LANES = 128
SUBLANES = 8
VMEM_LIMIT_BYTES = 56 * 1024 * 1024

SEQ_TILE = 256
MIX_ROWS = 512
MIX_SUB_ROWS = 256

SMALL_COLS = Q_LORA + 2 * KV_DIM + LANES
TAIL_OFF = Q_LORA + 2 * KV_DIM
WI_OFF = IDX_DIM
DT_OFF = IDX_DIM + IDX_HEADS
TAIL_END = DT_OFF + SSD_HEADS

PERM_STRIDE = CHUNK // SUBLANES
BF16_SUBLANES = 16
V_ROWS = ATT_HEAD_DIM + BF16_SUBLANES
SUM_ROW = ATT_HEAD_DIM
LOG2_E = 1.4426950408889634
INT_MIN = -(2 ** 31)


def _sigmoid(v):
    return 0.5 + 0.5 * jnp.tanh(0.5 * v)


def _silu(v):
    h = 0.5 * v
    return h + h * jnp.tanh(h)


def _const_spec(shape):
    zeros = (0,) * len(shape)
    return pl.BlockSpec(shape, lambda *_: zeros, pipeline_mode=pl.Buffered(1))


def _proj_kernel(x_ref, wa_ref, wz_ref, wx_ref, wg_ref, qnw_ref, wuq_ref, wiq_ref,
                 kng_ref, knb_ref, dtb_ref, rc_ref, rs1_ref, rs2_ref,
                 qt_ref, qit_ref, k4_ref, vt_ref, ki_ref, tail_ref, tailt_ref,
                 z_ref, xbc_ref, g_ref, perm_ref):
    xv = x_ref[0]
    xb = xv.astype(BF16)
    rc, rs1, rs2 = rc_ref[...], rs1_ref[...], rs2_ref[...]
    hd = ATT_HEAD_DIM

    ntile = D_MODEL // LANES
    for j in range(ntile):
        perm_ref[j] = xv[:, j * LANES:(j + 1) * LANES]
    xp = jnp.concatenate(
        [jnp.concatenate([perm_ref[j, pl.ds(k * CHUNK + i, SUBLANES, stride=PERM_STRIDE), :]
                          for j in range(ntile)], axis=1)
         for k in range(xv.shape[0] // CHUNK) for i in range(PERM_STRIDE)], axis=0).astype(BF16)

    def rope(t):
        half = ROT_DIM // 2
        return (t * rc + pltpu.roll(t, LANES - half, 1) * rs1 + pltpu.roll(t, half, 1) * rs2)

    pa = jnp.dot(xb, wa_ref[...], preferred_element_type=F32)

    lane = lax.broadcasted_iota(I32, (xv.shape[0], LANES), 1)
    dtr_p = jnp.dot(xp, wa_ref[:, TAIL_OFF:TAIL_OFF + LANES], preferred_element_type=F32) + dtb_ref[...]
    dt_p = jnp.maximum(dtr_p, 0.0) + jnp.log1p(jnp.exp(-jnp.abs(dtr_p)))
    tail_ref[0] = jnp.where((lane >= DT_OFF) & (lane < TAIL_END), dt_p, 0.0)
    z_ref[0] = jnp.dot(xp, wz_ref[...], preferred_element_type=F32).astype(BF16)
    xbc_ref[0] = jnp.dot(xp, wx_ref[...], preferred_element_type=F32)

    c_q = pa[:, :Q_LORA]
    ms = jnp.mean(c_q * c_q, axis=-1, keepdims=True)
    cq = (c_q * lax.rsqrt(ms + LN_EPS) * qnw_ref[...]).astype(BF16)

    scale = ATT_HEAD_DIM ** -0.5 * LOG2_E
    qf = jnp.dot(cq, wuq_ref[...], preferred_element_type=F32)
    for j in range(ATT_DIM // LANES):
        sl = slice(j * LANES, (j + 1) * LANES)
        qt_ref[0, sl, :] = (rope(qf[:, sl]) * scale).T.astype(BF16)
    qif = jnp.dot(cq, wiq_ref[...], preferred_element_type=F32)
    iscale = IDX_DIM ** -0.5
    for j in range(IDX_HEADS * IDX_DIM // LANES):
        sl = slice(j * LANES, (j + 1) * LANES)
        qit_ref[0, sl, :] = (rope(qif[:, sl]) * iscale).T.astype(BF16)
    for j in range(KV_DIM // LANES):
        kr = rope(pa[:, Q_LORA + j * LANES:Q_LORA + (j + 1) * LANES])
        k4_ref[0, 2 * j] = kr[:, :hd].astype(BF16)
        k4_ref[0, 2 * j + 1] = kr[:, hd:].astype(BF16)
        vtr = pa[:, Q_LORA + KV_DIM + j * LANES:Q_LORA + KV_DIM + (j + 1) * LANES].T
        sum_rows = jnp.where(lax.broadcasted_iota(I32, (V_ROWS - hd, vtr.shape[1]), 0) == 0,
                             1.0, 0.0).astype(BF16)
        vt_ref[0, 2 * j, 0] = jnp.concatenate([vtr[:hd, :].astype(BF16), sum_rows], axis=0)
        vt_ref[0, 2 * j + 1, 0] = jnp.concatenate([vtr[hd:, :].astype(BF16), sum_rows], axis=0)

    tail = pa[:, TAIL_OFF:TAIL_OFF + LANES]
    is_ki = lane < IDX_DIM
    mu = jnp.sum(jnp.where(is_ki, tail, 0.0), axis=-1, keepdims=True) * (1.0 / IDX_DIM)
    dv = jnp.where(is_ki, tail - mu, 0.0)
    var = jnp.sum(dv * dv, axis=-1, keepdims=True) * (1.0 / IDX_DIM)
    ki = rope(dv * lax.rsqrt(var + LN_EPS) * kng_ref[...] + knb_ref[...])
    wi = tail * (IDX_HEADS ** -0.5)
    ki_ref[0] = ki[:, :IDX_DIM].astype(BF16)
    tailt_ref[0] = jnp.where(is_ki, ki, jnp.where(lane < DT_OFF, wi, 0.0)).T

    g_ref[0] = jnp.dot(xb, wg_ref[...], preferred_element_type=F32).astype(BF16)


def _proj_call(x, wa, wz, wx, wg, qnw, wuq, wiq, kng, knb, dtb, rc, rs1, rs2):
    bsz, seq, _ = x.shape
    tm = SEQ_TILE
    nsb = seq // tm
    row = lambda n: pl.BlockSpec((1, tm, n), lambda b, i: (b, i, 0))
    col = lambda n: pl.BlockSpec((1, n, tm), lambda b, i: (b, 0, i))
    tab = pl.BlockSpec((tm, LANES), lambda b, i: (i, 0))
    sds = jax.ShapeDtypeStruct
    out_shapes = (
        sds((bsz, ATT_DIM, seq), BF16),
        sds((bsz, IDX_HEADS * IDX_DIM, seq), BF16),
        sds((bsz, N_KV_HEADS, seq, ATT_HEAD_DIM), BF16),
        sds((bsz, N_KV_HEADS, nsb, V_ROWS, tm), BF16),
        sds((bsz, seq, IDX_DIM), BF16),
        sds((bsz, seq, LANES), F32),
        sds((bsz, LANES, seq), F32),
        sds((bsz, seq, D_INNER), BF16),
        sds((bsz, seq, XBC_DIM), F32),
        sds((bsz, seq, 2 * D_MODEL), BF16),
    )
    out_specs = [
        col(ATT_DIM), col(IDX_HEADS * IDX_DIM),
        pl.BlockSpec((1, N_KV_HEADS, tm, ATT_HEAD_DIM), lambda b, i: (b, 0, i, 0)),
        pl.BlockSpec((1, N_KV_HEADS, 1, V_ROWS, tm), lambda b, i: (b, 0, i, 0, 0)),
        row(IDX_DIM), row(LANES), col(LANES), row(D_INNER), row(XBC_DIM), row(2 * D_MODEL),
    ]
    consts = (wa, wz, wx, wg, qnw, wuq, wiq, kng, knb, dtb)
    return pl.pallas_call(
        _proj_kernel,
        grid=(bsz, nsb),
        in_specs=[row(D_MODEL)] + [_const_spec(c.shape) for c in consts] + [tab, tab, tab],
        out_specs=out_specs,
        out_shape=out_shapes,
        scratch_shapes=[pltpu.VMEM((D_MODEL // LANES, tm, LANES), F32)],
        compiler_params=pltpu.CompilerParams(dimension_semantics=("parallel", "parallel"),
                                             vmem_limit_bytes=VMEM_LIMIT_BYTES),
        name="proj",
    )(x, *consts, rc, rs1, rs2)


COUNT_ROWS = 32


def _fold_rows(v, op, rows=32):
    while v.shape[0] > rows:
        half = v.shape[0] // 2
        v = op(v[:half], v[half:])
    return v


def _attn_kernel(ki_ref, qit_ref, wit_ref, k_ref, vt_ref, qt_ref, o_ref,
                 score_ref, coarse_ref, bias_ref, cut_ref, ot_ref, m_ref, acc_ref, *, topk):
    tq = qt_ref.shape[2]
    tk = tq
    qblk = pl.program_id(1)
    nch = qblk + 1
    sub = lax.broadcasted_iota(I32, (tk, tq), 0)
    tpos = qblk * tq + lax.broadcasted_iota(I32, (tk, tq), 1)

    def score_chunk(c, carry):
        off = pl.multiple_of(c * tk, tk)
        kic = ki_ref[0, pl.ds(off, tk), :]
        acc = jnp.zeros((tk, tq), F32)
        for hh in range(IDX_HEADS):
            sc = jnp.dot(kic, qit_ref[0, hh * IDX_DIM:(hh + 1) * IDX_DIM, :],
                         preferred_element_type=F32)
            acc = acc + wit_ref[0, WI_OFF + hh:WI_OFF + hh + 1, :] * jnp.maximum(sc, 0.0)
        masked = jnp.where(off + sub <= tpos, acc, -jnp.inf)
        score_ref[c] = masked
        coarse_ref[c] = masked.astype(BF16)
        return carry

    lax.fori_loop(0, nch, score_chunk, 0)

    def count(pred, ref=score_ref, dtype=I32):
        def body(c, part):
            hit = jnp.where(pred(ref[c], c), jnp.ones((), dtype), jnp.zeros((), dtype))
            return part + _fold_rows(hit, jnp.add, COUNT_ROWS)

        part = lax.fori_loop(0, nch, body, jnp.zeros((COUNT_ROWS, tq), dtype))
        return jnp.sum(part.astype(I32), axis=0, keepdims=True)

    def ordered_float(u):
        key = u ^ jnp.int32(INT_MIN)
        return pltpu.bitcast(jnp.where(key < 0, key ^ jnp.int32(0x7FFFFFFF), key), F32)

    half_bits = 16
    neg_fill = jnp.int32(2 ** half_bits - 1)

    def coarse_key(u16):
        return lax.shift_left(u16, half_bits) | jnp.where(u16 >= 2 ** (half_bits - 1), 0, neg_fill)

    def coarse_search(i, u16):
        cand = u16 | lax.shift_left(jnp.int32(1), half_bits - 1 - i)
        cand_b = ordered_float(coarse_key(cand)).astype(BF16)
        n = count(lambda sc, c: sc >= cand_b, coarse_ref, BF16)
        return jnp.where(n >= topk, cand, u16)

    u16 = lax.fori_loop(0, half_bits, coarse_search, jnp.zeros((1, tq), I32))
    base_u = coarse_key(u16) - jnp.int32(2 ** half_bits)

    def fine_search(i, d):
        cand = d | lax.shift_left(jnp.int32(1), half_bits - i)
        cand_f = ordered_float(base_u + cand)
        return jnp.where(count(lambda sc, c: sc >= cand_f) >= topk, cand, d)

    thr = ordered_float(base_u + lax.fori_loop(0, half_bits + 1, fine_search,
                                               jnp.zeros((1, tq), I32)))
    thr = jnp.where(tpos[0:1, :] < topk, -jnp.inf, thr)

    need = topk - count(lambda sc, c: sc > thr)
    n_eq = count(lambda sc, c: sc == thr)
    nbits = (k_ref.shape[2] - 1).bit_length()
    cut_ref[...] = jnp.full((1, tq), 2 ** nbits, I32)
    excess = jnp.where((n_eq > need) & (thr != -jnp.inf), 1, 0)

    @pl.when(jnp.max(excess) > 0)
    def _():
        def tie_search(i, cut):
            cand = cut | lax.shift_left(jnp.int32(1), nbits - 1 - i)
            cnt = count(lambda sc, c: (sc == thr) & (c * tk + sub < cand))
            return jnp.where(cnt < need, cand, cut)

        cut_ref[...] = lax.fori_loop(0, nbits, tie_search, jnp.zeros((1, tq), I32))

    cut = cut_ref[...]

    def bias_chunk(c, carry):
        sc = score_ref[c]
        spos = c * tk + sub
        sel = jnp.where(sc > thr, 1, jnp.where(sc == thr, jnp.where(spos <= cut, 1, 0), 0))
        sel = jnp.where(spos <= tpos, sel, 0)
        bias_ref[c] = jnp.where(sel > 0, 0.0, -jnp.inf)
        return carry

    lax.fori_loop(0, nch, bias_chunk, 0)

    grp = N_ATT_HEADS // N_KV_HEADS
    hd = ATT_HEAD_DIM
    m_ref[...] = jnp.full(m_ref.shape, -jnp.inf, F32)
    acc_ref[...] = jnp.zeros(acc_ref.shape, F32)

    def scores_of(c, g):
        kc = k_ref[0, g, pl.ds(pl.multiple_of(c * tk, tk), tk), :]
        return [jnp.dot(kc, qt_ref[0, (g * grp + i) * hd:(g * grp + i + 1) * hd, :],
                        preferred_element_type=F32) for i in range(grp)]

    def softmax_of(c, g, scores):
        probs, alphas = [], []
        for i in range(grp):
            h = g * grp + i
            m = m_ref[h:h + 1, :]
            s = scores[i] + bias_ref[c]
            m_new = jnp.maximum(m, jnp.max(_fold_rows(s, jnp.maximum), axis=0, keepdims=True))
            m_safe = jnp.where(m_new == -jnp.inf, 0.0, m_new)
            alphas.append(jnp.exp2(m - m_safe))
            probs.append(jnp.exp2(s - m_safe).astype(BF16))
            m_ref[h:h + 1, :] = m_new
        return probs, alphas

    def pv_of(c, g, probs, alphas):
        vc = vt_ref[0, g, c]
        for i in range(grp):
            h = g * grp + i
            acc_ref[h] = alphas[i] * acc_ref[h] + jnp.dot(vc, probs[i], preferred_element_type=F32)

    def attend(chunks):
        units = [(c, g) for c in chunks for g in range(N_KV_HEADS)]
        scores = scores_of(*units[0])
        pending = None
        for n, unit in enumerate(units):
            nxt = scores_of(*units[n + 1]) if n + 1 < len(units) else None
            cur = softmax_of(*unit, scores)
            if pending is not None:
                pv_of(*units[n - 1], *pending)
            scores, pending = nxt, cur
        pv_of(*units[-1], *pending)

    def chunk_pair(n, carry):
        attend([2 * n, 2 * n + 1])
        return carry

    lax.fori_loop(0, nch // 2, chunk_pair, 0)

    @pl.when(nch % 2 == 1)
    def _():
        attend([nch - 1])

    for h in range(N_ATT_HEADS):
        ot_ref[h * hd:(h + 1) * hd, :] = acc_ref[h, :hd, :] / acc_ref[h, SUM_ROW:SUM_ROW + 1, :]
    o_ref[0] = ot_ref[...].T.astype(BF16)


def _attn_call(ki, qit, tailt, k4, vt5, qt, topk):
    bsz, seq, _ = ki.shape
    tq = vt5.shape[-1]
    nch = seq // tq
    assert tq >= topk and vt5.shape[2] == nch
    return pl.pallas_call(
        functools.partial(_attn_kernel, topk=topk),
        grid=(bsz, nch),
        in_specs=[
            pl.BlockSpec((1, seq, IDX_DIM), lambda b, j: (b, 0, 0)),
            pl.BlockSpec((1, IDX_HEADS * IDX_DIM, tq), lambda b, j: (b, 0, j)),
            pl.BlockSpec((1, LANES, tq), lambda b, j: (b, 0, j)),
            pl.BlockSpec((1, N_KV_HEADS, seq, ATT_HEAD_DIM), lambda b, j: (b, 0, 0, 0)),
            pl.BlockSpec((1, N_KV_HEADS, nch, V_ROWS, tq), lambda b, j: (b, 0, 0, 0, 0)),
            pl.BlockSpec((1, ATT_DIM, tq), lambda b, j: (b, 0, j)),
        ],
        out_specs=pl.BlockSpec((1, tq, ATT_DIM), lambda b, j: (b, j, 0)),
        out_shape=jax.ShapeDtypeStruct((bsz, seq, ATT_DIM), BF16),
        scratch_shapes=[pltpu.VMEM((nch, tq, tq), F32), pltpu.VMEM((nch, tq, tq), BF16),
                        pltpu.VMEM((nch, tq, tq), F32),
                        pltpu.VMEM((1, tq), I32), pltpu.VMEM((ATT_DIM, tq), F32),
                        pltpu.VMEM((N_ATT_HEADS, tq), F32),
                        pltpu.VMEM((N_ATT_HEADS, V_ROWS, tq), F32)],
        compiler_params=pltpu.CompilerParams(dimension_semantics=("parallel", "parallel"),
                                             vmem_limit_bytes=VMEM_LIMIT_BYTES),
        name="attn",
    )(ki, qit, tailt, k4, vt5, qt)


def _split2(v):
    hi = v.astype(BF16)
    mid = (v - hi.astype(F32)).astype(BF16)
    return jnp.concatenate([hi, mid], axis=1)


def _ssd_kernel(xbc_ref, z_ref, tail_ref, cw_ref, cb_ref, alog_ref, e2_ref,
                dx_ref, nw_ref, out_ref, carry, state, ybuf):
    L = CHUNK
    hpg = SSD_HEADS // SSD_GROUPS
    gw = hpg * SSD_HEAD_DIM

    nshift = CONV_W - 1
    carry_rows = nshift * SUBLANES

    @pl.when(pl.program_id(1) == 0)
    def _():
        carry[...] = jnp.zeros_like(carry)
        state[...] = jnp.zeros_like(state)

    u = xbc_ref[0]
    sub8 = lax.broadcasted_iota(I32, (SUBLANES, XBC_DIM), 0)
    wrapped = []
    for k in range(nshift):
        rows = slice(L - carry_rows + k * SUBLANES, L - carry_rows + (k + 1) * SUBLANES)
        merged = jnp.where(sub8 == SUBLANES - 1, carry[k * SUBLANES:(k + 1) * SUBLANES, :], u[rows, :])
        wrapped.append(pltpu.roll(merged, 1, 0))
    carry[...] = u[L - carry_rows:, :]
    conv = cb_ref[...] + cw_ref[CONV_W - 1:CONV_W, :] * u
    for d in range(1, CONV_W):
        shifted = jnp.concatenate(wrapped[nshift - d:] + [u[:L - d * SUBLANES, :]], axis=0)
        conv = conv + cw_ref[CONV_W - 1 - d:CONV_W - d, :] * shifted
    act = _silu(conv)
    xs = act[:, :D_INNER]

    row = lax.broadcasted_iota(I32, (L, L), 0)
    col = lax.broadcasted_iota(I32, (L, L), 1)
    time_of = lambda r: PERM_STRIDE * (r & (SUBLANES - 1)) + (r >> (SUBLANES.bit_length() - 1))
    lower = time_of(row) >= time_of(col)
    tail = tail_ref[0]
    adt = tail * (-jnp.exp(alog_ref[...]) * LOG2_E)
    hi = adt.astype(BF16)
    r1 = adt - hi.astype(F32)
    mid = r1.astype(BF16)
    lo = (r1 - mid.astype(F32)).astype(BF16)
    ones_l = jnp.where(lower, 1.0, 0.0).astype(BF16)
    a_cum = jnp.dot(jnp.concatenate([ones_l, ones_l, ones_l], axis=1),
                    jnp.concatenate([hi, mid, lo], axis=0), preferred_element_type=F32)
    a_cum_t = a_cum.T
    e2 = e2_ref[...]
    a_cum_x = jnp.dot(_split2(a_cum), e2, preferred_element_type=F32)
    dt_x = jnp.dot(_split2(tail), e2, preferred_element_type=F32)
    a_last_x = a_cum_x[L - 1:L, :]
    xf = xs * dt_x
    xf_b = xf.astype(BF16)
    xd_b = (xf * jnp.exp2(a_last_x - a_cum_x)).astype(BF16)
    ea_x = jnp.exp2(a_cum_x)
    chunk_decay_x = jnp.exp2(a_last_x)

    lane = lax.broadcasted_iota(I32, (L, LANES), 1)
    for g in range(SSD_GROUPS):
        b_f = act[:, D_INNER + g * D_STATE:D_INNER + (g + 1) * D_STATE]
        bg = b_f.astype(BF16)
        bg_t = b_f.T.astype(BF16)
        cg = act[:, D_INNER + BC_DIM + g * D_STATE:D_INNER + BC_DIM + (g + 1) * D_STATE].astype(BF16)
        gs = slice(g * gw, (g + 1) * gw)
        cb = lax.dot_general(cg, bg, (((1,), (1,)), ((), ())), preferred_element_type=F32)
        st_prev = state[:, gs]
        y_off = jnp.dot(cg, st_prev.astype(BF16), preferred_element_type=F32) * ea_x[:, gs]
        st_new = jnp.dot(bg_t, xd_b[:, gs], preferred_element_type=F32)
        state[:, gs] = st_prev * chunk_decay_x[:, gs] + st_new
        for pr in range(hpg // 2):
            h0 = g * hpg + 2 * pr
            xs_pair = xf_b[:, h0 * SSD_HEAD_DIM:(h0 + 2) * SSD_HEAD_DIM]
            ys = []
            for h in (h0, h0 + 1):
                seg = a_cum[:, DT_OFF + h:DT_OFF + h + 1] - a_cum_t[DT_OFF + h:DT_OFF + h + 1, :]
                lmat = jnp.exp2(jnp.where(lower, seg, -jnp.inf))
                ys.append(jnp.dot((cb * lmat).astype(BF16), xs_pair, preferred_element_type=F32))
            ybuf[h0 // 2] = (jnp.where(lane < SSD_HEAD_DIM, ys[0], ys[1])
                             + y_off[:, 2 * pr * SSD_HEAD_DIM:(2 * pr + 2) * SSD_HEAD_DIM])

    ntile = D_INNER // LANES
    y = jnp.concatenate([ybuf[j] for j in range(ntile)], axis=1) + dx_ref[...] * xs
    zf = z_ref[0].astype(F32)
    yz = y * _silu(zf)
    for g in range(SSD_GROUPS):
        gs = slice(g * gw, (g + 1) * gw)
        yg = yz[:, gs]
        msq = jnp.mean(yg * yg, axis=-1, keepdims=True)
        og = yg * lax.rsqrt(msq + LN_EPS) * nw_ref[:, gs]
        for j in range(gw // LANES):
            ybuf[g * (gw // LANES) + j] = og[:, j * LANES:(j + 1) * LANES]
    half = L // 2
    out_ref[0] = jnp.concatenate(
        [jnp.concatenate([ybuf[j, pl.ds(half * (t % 2) + t // 2, SUBLANES, stride=SUBLANES), :]
                          for j in range(ntile)], axis=1)
         for t in range(PERM_STRIDE)], axis=0).astype(BF16)


def _ssd_call(xbc, z, tail, cw, cb, alog, e2, dx, nw):
    bsz, seq, _ = xbc.shape
    L = CHUNK
    blk = lambda n: pl.BlockSpec((1, L, n), lambda b, c: (b, c, 0))
    consts = (cw, cb, alog, e2, dx, nw)
    return pl.pallas_call(
        _ssd_kernel,
        grid=(bsz, seq // L),
        in_specs=[blk(XBC_DIM), blk(D_INNER), blk(LANES)] + [_const_spec(c.shape) for c in consts],
        out_specs=blk(D_INNER),
        out_shape=jax.ShapeDtypeStruct((bsz, seq, D_INNER), BF16),
        scratch_shapes=[pltpu.VMEM(((CONV_W - 1) * SUBLANES, XBC_DIM), F32),
                        pltpu.VMEM((D_STATE, D_INNER), F32),
                        pltpu.VMEM((D_INNER // LANES, L, LANES), F32)],
        compiler_params=pltpu.CompilerParams(dimension_semantics=("parallel", "arbitrary"),
                                             vmem_limit_bytes=VMEM_LIMIT_BYTES),
        name="ssd",
    )(xbc, z, tail, *consts)


def _layer_norm(v, g, b):
    mu = jnp.mean(v, axis=-1, keepdims=True)
    d = v - mu
    var = jnp.mean(d * d, axis=-1, keepdims=True)
    return d * lax.rsqrt(var + LN_EPS) * g + b


def _mix_kernel(att_ref, ssd_ref, g_ref, x_ref, wa_ref, ws_ref, wo_ref, l1g_ref, l1b_ref,
                wu_ref, wd_ref, l2g_ref, l2b_ref, o_ref, *, alpha, ff_chunk):
    tm = x_ref.shape[0]
    subs = [slice(s * MIX_SUB_ROWS, (s + 1) * MIX_SUB_ROWS) for s in range(tm // MIX_SUB_ROWS)]
    mixed = []
    for r in subs:
        ga = g_ref[r, :D_MODEL].astype(F32)
        gs = g_ref[r, D_MODEL:].astype(F32)
        mixed.append(
            _sigmoid(ga) * jnp.dot(att_ref[r, :], wa_ref[...], preferred_element_type=F32)
            + _sigmoid(gs) * jnp.dot(ssd_ref[r, :], ws_ref[...], preferred_element_type=F32))
    hs = []
    for r, mx in zip(subs, mixed):
        proj = jnp.dot(mx.astype(BF16), wo_ref[...], preferred_element_type=F32)
        hs.append(_layer_norm(alpha * x_ref[r, :] + proj, l1g_ref[...], l1b_ref[...]))
    hbs = [h.astype(BF16) for h in hs]
    ffs = [jnp.zeros_like(h) for h in hs]
    for c in range(D_FF // ff_chunk):
        cs = slice(c * ff_chunk, (c + 1) * ff_chunk)
        for s in range(len(subs)):
            u = jnp.maximum(jnp.dot(hbs[s], wu_ref[:, cs], preferred_element_type=F32), 0.0)
            ffs[s] = ffs[s] + jnp.dot((u * u).astype(BF16), wd_ref[cs, :],
                                      preferred_element_type=F32)
    for r, h, ff in zip(subs, hs, ffs):
        o_ref[r, :] = _layer_norm(alpha * h + ff, l2g_ref[...], l2b_ref[...])


def _mix_call(att, ssd, gates, x2, wa, ws, wo, l1g, l1b, wu, wd, l2g, l2b, alpha, tm):
    m = x2.shape[0]
    row = lambda n: pl.BlockSpec((tm, n), lambda i: (i, 0))
    consts = (wa, ws, wo, l1g, l1b, wu, wd, l2g, l2b)
    return pl.pallas_call(
        functools.partial(_mix_kernel, alpha=alpha, ff_chunk=1024),
        grid=(m // tm,),
        in_specs=[row(ATT_DIM), row(D_INNER), row(2 * D_MODEL), row(D_MODEL)]
                 + [_const_spec(c.shape) for c in consts],
        out_specs=row(D_MODEL),
        out_shape=jax.ShapeDtypeStruct((m, D_MODEL), F32),
        compiler_params=pltpu.CompilerParams(dimension_semantics=("parallel",),
                                             vmem_limit_bytes=VMEM_LIMIT_BYTES),
        name="mix",
    )(att, ssd, gates, x2, *consts)


def _rope_tables(seq):
    half = ROT_DIM // 2
    inv = ROPE_THETA ** (-jnp.arange(0, ROT_DIM, 2, dtype=F32) / ROT_DIM)
    ang = jnp.arange(seq, dtype=F32)[:, None] * inv[None, :]
    cos, sin = jnp.cos(ang), jnp.sin(ang)
    pad = jnp.zeros((seq, ATT_HEAD_DIM - ROT_DIM), F32)
    zero = jnp.zeros((seq, half), F32)
    rc = jnp.concatenate([cos, cos, pad + 1.0], -1)
    rs1 = jnp.concatenate([-sin, zero, pad], -1)
    rs2 = jnp.concatenate([zero, sin, pad], -1)
    rep = LANES // ATT_HEAD_DIM
    return tuple(jnp.tile(t, (1, rep)) for t in (rc, rs1, rs2))


def _lane_row(v, off):
    return jnp.zeros((1, LANES), F32).at[0, off:off + v.shape[0]].set(v)


def _layer(x, tables, topk, alpha, w_in, q_norm_w, w_uq, w_iq, k_idx_norm_g, k_idx_norm_b,
           conv_w, conv_b, dt_bias, a_log, d_skip, ssd_norm_w, w_attn_branch, w_ssd_branch,
           w_out, ln1_g, ln1_b, w_up, w_down, ln2_g, ln2_b):
    bsz, seq, _ = x.shape
    m = bsz * seq

    offs = [0]
    for s in IN_SIZES:
        offs.append(offs[-1] + s)
    seg = lambda i: w_in[:, offs[i]:offs[i + 1]]
    pad = jnp.zeros((D_MODEL, LANES - TAIL_END), F32)
    wa = jnp.concatenate([seg(0), seg(1), seg(2), seg(3), seg(4), seg(7), pad], 1).astype(BF16)
    wz, wx, wg = seg(5).astype(BF16), seg(6).astype(BF16), seg(8).astype(BF16)

    qt, qit, k4, vt5, ki, tail, tailt, z, xbc, gates = _proj_call(
        x, wa, wz, wx, wg, q_norm_w.reshape(1, Q_LORA), w_uq.astype(BF16), w_iq.astype(BF16),
        _lane_row(k_idx_norm_g, 0), _lane_row(k_idx_norm_b, 0), _lane_row(dt_bias, DT_OFF), *tables)

    att = _attn_call(ki, qit, tailt, k4, vt5, qt, topk)

    slot = jnp.arange(LANES)[:, None] - DT_OFF
    expand = (slot == jnp.arange(D_INNER)[None, :] // SSD_HEAD_DIM).astype(BF16)
    e2 = jnp.concatenate([expand, expand], axis=0)
    ssd = _ssd_call(xbc, z, tail, conv_w, conv_b.reshape(1, XBC_DIM), _lane_row(a_log, DT_OFF), e2,
                    jnp.repeat(d_skip, SSD_HEAD_DIM).reshape(1, D_INNER),
                    ssd_norm_w.reshape(1, D_INNER))

    r = lambda t: t.reshape(1, -1)
    out = _mix_call(att.reshape(m, ATT_DIM), ssd.reshape(m, D_INNER), gates.reshape(m, 2 * D_MODEL),
                    x.reshape(m, D_MODEL), w_attn_branch.astype(BF16), w_ssd_branch.astype(BF16),
                    w_out.astype(BF16), r(ln1_g), r(ln1_b), w_up.astype(BF16), w_down.astype(BF16),
                    r(ln2_g), r(ln2_b), alpha, tm=MIX_ROWS)
    return out.reshape(bsz, seq, D_MODEL)


def kernel(x, w_in, q_norm_w, w_uq, w_iq, k_idx_norm_g, k_idx_norm_b, conv_w, conv_b, dt_bias,
           a_log, d_skip, ssd_norm_w, w_attn_branch, w_ssd_branch, w_out, ln1_g, ln1_b, w_up,
           w_down, ln2_g, ln2_b):
    depth = w_in.shape[0]
    seq = x.shape[1]
    topk = min(TOPK_MAX, seq // 4)
    alpha = (2 * depth) ** 0.25
    tables = _rope_tables(seq)
    h = x
    for l in range(depth):
        h = _layer(h, tables, topk, alpha, w_in[l], q_norm_w[l], w_uq[l], w_iq[l], k_idx_norm_g[l],
                   k_idx_norm_b[l], conv_w[l], conv_b[l], dt_bias[l], a_log[l], d_skip[l],
                   ssd_norm_w[l], w_attn_branch[l], w_ssd_branch[l], w_out[l], ln1_g[l], ln1_b[l],
                   w_up[l], w_down[l], ln2_g[l], ln2_b[l])
    return h
```

```python
import functools

import jax
import jax.numpy as jnp
from jax import lax
from jax.experimental import pallas as pl
from jax.experimental.pallas import tpu as pltpu

F32 = jnp.float32
BF16 = jnp.bfloat16
I32 = jnp.int32

D_MODEL = 1024
N_ATT_HEADS = 16
ATT_HEAD_DIM = 64
N_KV_HEADS = 4
ATT_DIM = N_ATT_HEADS * ATT_HEAD_DIM
KV_DIM = N_KV_HEADS * ATT_HEAD_DIM
Q_LORA = 256
IDX_HEADS = 8
IDX_DIM = 64
TOPK_MAX = 256
ROT_DIM = ATT_HEAD_DIM // 4
ROPE_THETA = 500000.0
D_INNER = 2 * D_MODEL
SSD_HEAD_DIM = 64
SSD_HEADS = D_INNER // SSD_HEAD_DIM
SSD_GROUPS = 4
D_STATE = 128
CONV_W = 4
CHUNK = 128
BC_DIM = SSD_GROUPS * D_STATE
XBC_DIM = D_INNER + 2 * BC_DIM
D_FF = 4 * D_MODEL
IN_SIZES = (Q_LORA, KV_DIM, KV_DIM, IDX_DIM, IDX_HEADS, D_INNER, XBC_DIM, SSD_HEADS, 2 * D_MODEL)
LN_EPS = 1e-5

LANES = 128
SUBLANES = 8
VMEM_LIMIT_BYTES = 56 * 1024 * 1024

SEQ_TILE = 256
MIX_ROWS = 512
MIX_SUB_ROWS = 256

SMALL_COLS = Q_LORA + 2 * KV_DIM + LANES
TAIL_OFF = Q_LORA + 2 * KV_DIM
WI_OFF = IDX_DIM
DT_OFF = IDX_DIM + IDX_HEADS
TAIL_END = DT_OFF + SSD_HEADS

PERM_STRIDE = CHUNK // SUBLANES
BF16_SUBLANES = 16
V_ROWS = ATT_HEAD_DIM + BF16_SUBLANES
SUM_ROW = ATT_HEAD_DIM
SHIFT_MARGIN = 1.0 + 2.0 ** -6
MIN_SOFTMAX_SUM = 2.0 ** -40
LOG2_E = 1.4426950408889634
INT_MIN = -(2 ** 31)


def _sigmoid(v):
    return 0.5 + 0.5 * jnp.tanh(0.5 * v)


def _silu(v):
    h = 0.5 * v
    return h + h * jnp.tanh(h)


def _const_spec(shape):
    zeros = (0,) * len(shape)
    return pl.BlockSpec(shape, lambda *_: zeros, pipeline_mode=pl.Buffered(1))


def _proj_kernel(x_ref, wa_ref, wz_ref, wx_ref, wg_ref, qnw_ref, wuq_ref, wiq_ref,
                 kng_ref, knb_ref, dtb_ref, rc_ref, rs1_ref, rs2_ref,
                 qt_ref, qit_ref, k4_ref, vt_ref, ki_ref, tail_ref, tailt_ref,
                 z_ref, xbc_ref, g_ref, perm_ref):
    xv = x_ref[0]
    xb = xv.astype(BF16)
    rc, rs1, rs2 = rc_ref[...], rs1_ref[...], rs2_ref[...]
    hd = ATT_HEAD_DIM

    ntile = D_MODEL // LANES
    for j in range(ntile):
        perm_ref[j] = xv[:, j * LANES:(j + 1) * LANES]
    xp = jnp.concatenate(
        [jnp.concatenate([perm_ref[j, pl.ds(k * CHUNK + i, SUBLANES, stride=PERM_STRIDE), :]
                          for j in range(ntile)], axis=1)
         for k in range(xv.shape[0] // CHUNK) for i in range(PERM_STRIDE)], axis=0).astype(BF16)

    def rope(t):
        half = ROT_DIM // 2
        return (t * rc + pltpu.roll(t, LANES - half, 1) * rs1 + pltpu.roll(t, half, 1) * rs2)

    pa = jnp.dot(xb, wa_ref[...], preferred_element_type=F32)

    lane = lax.broadcasted_iota(I32, (xv.shape[0], LANES), 1)
    dtr_p = jnp.dot(xp, wa_ref[:, TAIL_OFF:TAIL_OFF + LANES], preferred_element_type=F32) + dtb_ref[...]
    dt_p = jnp.maximum(dtr_p, 0.0) + jnp.log1p(jnp.exp(-jnp.abs(dtr_p)))
    tail_ref[0] = jnp.where((lane >= DT_OFF) & (lane < TAIL_END), dt_p, 0.0)
    z_ref[0] = jnp.dot(xp, wz_ref[...], preferred_element_type=F32).astype(BF16)
    xbc_ref[0] = jnp.dot(xp, wx_ref[...], preferred_element_type=F32)

    c_q = pa[:, :Q_LORA]
    ms = jnp.mean(c_q * c_q, axis=-1, keepdims=True)
    cq = (c_q * lax.rsqrt(ms + LN_EPS) * qnw_ref[...]).astype(BF16)

    scale = ATT_HEAD_DIM ** -0.5 * LOG2_E
    qf = jnp.dot(cq, wuq_ref[...], preferred_element_type=F32)
    for j in range(ATT_DIM // LANES):
        sl = slice(j * LANES, (j + 1) * LANES)
        qt_ref[0, sl, :] = (rope(qf[:, sl]) * scale).T.astype(BF16)
    qif = jnp.dot(cq, wiq_ref[...], preferred_element_type=F32)
    iscale = IDX_DIM ** -0.5
    for j in range(IDX_HEADS * IDX_DIM // LANES):
        sl = slice(j * LANES, (j + 1) * LANES)
        qit_ref[0, sl, :] = (rope(qif[:, sl]) * iscale).T.astype(BF16)
    for j in range(KV_DIM // LANES):
        kr = rope(pa[:, Q_LORA + j * LANES:Q_LORA + (j + 1) * LANES])
        for i, kh in enumerate((kr, pltpu.roll(kr, hd, 1))):
            k4_ref[0, 2 * j + i] = jnp.where(lane < hd, kh, jnp.where(lane == hd, 1.0, 0.0)).astype(BF16)
        vtr = pa[:, Q_LORA + KV_DIM + j * LANES:Q_LORA + KV_DIM + (j + 1) * LANES].T
        sum_rows = jnp.where(lax.broadcasted_iota(I32, (V_ROWS - hd, vtr.shape[1]), 0) == 0,
                             1.0, 0.0).astype(BF16)
        vt_ref[0, 2 * j, 0] = jnp.concatenate([vtr[:hd, :].astype(BF16), sum_rows], axis=0)
        vt_ref[0, 2 * j + 1, 0] = jnp.concatenate([vtr[hd:, :].astype(BF16), sum_rows], axis=0)

    tail = pa[:, TAIL_OFF:TAIL_OFF + LANES]
    is_ki = lane < IDX_DIM
    mu = jnp.sum(jnp.where(is_ki, tail, 0.0), axis=-1, keepdims=True) * (1.0 / IDX_DIM)
    dv = jnp.where(is_ki, tail - mu, 0.0)
    var = jnp.sum(dv * dv, axis=-1, keepdims=True) * (1.0 / IDX_DIM)
    ki = rope(dv * lax.rsqrt(var + LN_EPS) * kng_ref[...] + knb_ref[...])
    wi = tail * (IDX_HEADS ** -0.5)
    ki_ref[0] = ki[:, :IDX_DIM].astype(BF16)
    tailt_ref[0] = jnp.where(is_ki, ki, jnp.where(lane < DT_OFF, wi, 0.0)).T

    g_ref[0] = jnp.dot(xb, wg_ref[...], preferred_element_type=F32).astype(BF16)


def _proj_call(x, wa, wz, wx, wg, qnw, wuq, wiq, kng, knb, dtb, rc, rs1, rs2):
    bsz, seq, _ = x.shape
    tm = SEQ_TILE
    nsb = seq // tm
    row = lambda n: pl.BlockSpec((1, tm, n), lambda b, i: (b, i, 0))
    col = lambda n: pl.BlockSpec((1, n, tm), lambda b, i: (b, 0, i))
    tab = pl.BlockSpec((tm, LANES), lambda b, i: (i, 0))
    sds = jax.ShapeDtypeStruct
    out_shapes = (
        sds((bsz, ATT_DIM, seq), BF16),
        sds((bsz, IDX_HEADS * IDX_DIM, seq), BF16),
        sds((bsz, N_KV_HEADS, seq, LANES), BF16),
        sds((bsz, N_KV_HEADS, nsb, V_ROWS, tm), BF16),
        sds((bsz, seq, IDX_DIM), BF16),
        sds((bsz, seq, LANES), F32),
        sds((bsz, LANES, seq), F32),
        sds((bsz, seq, D_INNER), BF16),
        sds((bsz, seq, XBC_DIM), F32),
        sds((bsz, seq, 2 * D_MODEL), BF16),
    )
    out_specs = [
        col(ATT_DIM), col(IDX_HEADS * IDX_DIM),
        pl.BlockSpec((1, N_KV_HEADS, tm, LANES), lambda b, i: (b, 0, i, 0)),
        pl.BlockSpec((1, N_KV_HEADS, 1, V_ROWS, tm), lambda b, i: (b, 0, i, 0, 0)),
        row(IDX_DIM), row(LANES), col(LANES), row(D_INNER), row(XBC_DIM), row(2 * D_MODEL),
    ]
    consts = (wa, wz, wx, wg, qnw, wuq, wiq, kng, knb, dtb)
    return pl.pallas_call(
        _proj_kernel,
        grid=(bsz, nsb),
        in_specs=[row(D_MODEL)] + [_const_spec(c.shape) for c in consts] + [tab, tab, tab],
        out_specs=out_specs,
        out_shape=out_shapes,
        scratch_shapes=[pltpu.VMEM((D_MODEL // LANES, tm, LANES), F32)],
        compiler_params=pltpu.CompilerParams(dimension_semantics=("parallel", "parallel"),
                                             vmem_limit_bytes=VMEM_LIMIT_BYTES),
        name="proj",
    )(x, *consts, rc, rs1, rs2)


COUNT_ROWS = 32


def _fold_rows(v, op, rows=32):
    while v.shape[0] > rows:
        half = v.shape[0] // 2
        v = op(v[:half], v[half:])
    return v


def _attn_kernel(ki_ref, qit_ref, wit_ref, k_ref, vt_ref, qt_ref, o_ref,
                 score_ref, coarse_ref, mask_ref, cut_ref, ot_ref, m_ref, acc_ref, qa_ref, kmax_ref,
                 *, topk):
    tq = qt_ref.shape[2]
    tk = tq
    qblk = pl.program_id(1)
    nch = qblk + 1
    sub = lax.broadcasted_iota(I32, (tk, tq), 0)
    tpos = qblk * tq + lax.broadcasted_iota(I32, (tk, tq), 1)

    def score_chunk(c, carry):
        off = pl.multiple_of(c * tk, tk)
        kic = ki_ref[0, pl.ds(off, tk), :]
        acc = jnp.zeros((tk, tq), F32)
        for hh in range(IDX_HEADS):
            sc = jnp.dot(kic, qit_ref[0, hh * IDX_DIM:(hh + 1) * IDX_DIM, :],
                         preferred_element_type=F32)
            acc = acc + wit_ref[0, WI_OFF + hh:WI_OFF + hh + 1, :] * jnp.maximum(sc, 0.0)
        masked = jnp.where(off + sub <= tpos, acc, -jnp.inf)
        score_ref[c] = masked
        coarse_ref[c] = masked.astype(BF16)
        return carry

    lax.fori_loop(0, nch, score_chunk, 0)

    def count(pred, ref=score_ref, dtype=I32):
        def body(c, part):
            hit = jnp.where(pred(ref[c], c), jnp.ones((), dtype), jnp.zeros((), dtype))
            return part + _fold_rows(hit, jnp.add, COUNT_ROWS)

        part = lax.fori_loop(0, nch, body, jnp.zeros((COUNT_ROWS, tq), dtype))
        return jnp.sum(part.astype(I32), axis=0, keepdims=True)

    def ordered_float(u):
        key = u ^ jnp.int32(INT_MIN)
        return pltpu.bitcast(jnp.where(key < 0, key ^ jnp.int32(0x7FFFFFFF), key), F32)

    half_bits = 16
    neg_fill = jnp.int32(2 ** half_bits - 1)

    def coarse_key(u16):
        return lax.shift_left(u16, half_bits) | jnp.where(u16 >= 2 ** (half_bits - 1), 0, neg_fill)

    def coarse_search(i, u16):
        cand = u16 | lax.shift_left(jnp.int32(1), half_bits - 1 - i)
        cand_b = ordered_float(coarse_key(cand)).astype(BF16)
        n = count(lambda sc, c: sc >= cand_b, coarse_ref, BF16)
        return jnp.where(n >= topk, cand, u16)

    u16 = lax.fori_loop(0, half_bits, coarse_search, jnp.zeros((1, tq), I32))
    base_u = coarse_key(u16) - jnp.int32(2 ** half_bits)

    def fine_search(i, d):
        cand = d | lax.shift_left(jnp.int32(1), half_bits - i)
        cand_f = ordered_float(base_u + cand)
        return jnp.where(count(lambda sc, c: sc >= cand_f) >= topk, cand, d)

    thr = ordered_float(base_u + lax.fori_loop(0, half_bits + 1, fine_search,
                                               jnp.zeros((1, tq), I32)))
    thr = jnp.where(tpos[0:1, :] < topk, -jnp.inf, thr)

    need = topk - count(lambda sc, c: sc > thr)
    n_eq = count(lambda sc, c: sc == thr)
    nbits = (k_ref.shape[2] - 1).bit_length()
    cut_ref[...] = jnp.full((1, tq), 2 ** nbits, I32)
    excess = jnp.where((n_eq > need) & (thr != -jnp.inf), 1, 0)

    @pl.when(jnp.max(excess) > 0)
    def _():
        def tie_search(i, cut):
            cand = cut | lax.shift_left(jnp.int32(1), nbits - 1 - i)
            cnt = count(lambda sc, c: (sc == thr) & (c * tk + sub < cand))
            return jnp.where(cnt < need, cand, cut)

        cut_ref[...] = lax.fori_loop(0, nbits, tie_search, jnp.zeros((1, tq), I32))

    cut = cut_ref[...]

    def mask_chunk(c, carry):
        sc = score_ref[c]
        spos = c * tk + sub
        sel = jnp.where(sc > thr, 1, jnp.where(sc == thr, jnp.where(spos <= cut, 1, 0), 0))
        sel = jnp.where(spos <= tpos, sel, 0)
        mask_ref[c] = jnp.where(sel > 0, 1.0, 0.0).astype(BF16)
        return carry

    lax.fori_loop(0, nch, mask_chunk, 0)

    grp = N_ATT_HEADS // N_KV_HEADS
    hd = ATT_HEAD_DIM
    klane = lax.broadcasted_iota(I32, (k_ref.shape[2], LANES), 1)

    @pl.when(qblk == 0)
    def _():
        for g in range(N_KV_HEADS):
            kf = k_ref[0, g].astype(F32)
            norm2 = jnp.sum(jnp.where(klane < hd, kf * kf, 0.0), axis=1, keepdims=True)
            kmax_ref[g:g + 1, :] = jnp.broadcast_to(jnp.max(norm2, axis=0, keepdims=True), (1, tq))

    shift_row = lax.broadcasted_iota(I32, (LANES - hd, tq), 0) == 0
    for h in range(N_ATT_HEADS):
        qs = qt_ref[0, h * hd:(h + 1) * hd, :]
        qf = qs.astype(F32)
        r = jnp.sqrt(jnp.sum(qf * qf, axis=0, keepdims=True) * kmax_ref[h // grp:h // grp + 1, :])
        r = r * SHIFT_MARGIN
        qa_ref[h] = jnp.concatenate([qs, jnp.where(shift_row, -r, 0.0).astype(BF16)], axis=0)

    def scores_of(c, g):
        kc = k_ref[0, g, pl.ds(pl.multiple_of(c * tk, tk), tk), :]
        return [jnp.dot(kc, qa_ref[g * grp + i], preferred_element_type=F32) for i in range(grp)]

    def probs_of(c, g, scores):
        mk = mask_ref[c]
        return ([jnp.exp2(s).astype(BF16) * mk for s in scores],)

    def pv_of(c, g, probs):
        vc = vt_ref[0, g, c]
        for i in range(grp):
            h = g * grp + i
            acc_ref[h] = acc_ref[h] + jnp.dot(vc, probs[i], preferred_element_type=F32)

    def online_probs_of(c, g, scores):
        bias = jnp.where(mask_ref[c].astype(F32) > 0.0, 0.0, -jnp.inf)
        probs, alphas = [], []
        for i in range(grp):
            h = g * grp + i
            m = m_ref[h:h + 1, :]
            s = scores[i] + bias
            m_new = jnp.maximum(m, jnp.max(_fold_rows(s, jnp.maximum), axis=0, keepdims=True))
            m_safe = jnp.where(m_new == -jnp.inf, 0.0, m_new)
            alphas.append(jnp.exp2(m - m_safe))
            probs.append(jnp.exp2(s - m_safe).astype(BF16))
            m_ref[h:h + 1, :] = m_new
        return probs, alphas

    def online_pv_of(c, g, probs, alphas):
        vc = vt_ref[0, g, c]
        for i in range(grp):
            h = g * grp + i
            acc_ref[h] = alphas[i] * acc_ref[h] + jnp.dot(vc, probs[i], preferred_element_type=F32)

    def attend(chunks, probs_fn, pv_fn):
        units = [(c, g) for c in chunks for g in range(N_KV_HEADS)]
        scores = scores_of(*units[0])
        pending = None
        for n, unit in enumerate(units):
            nxt = scores_of(*units[n + 1]) if n + 1 < len(units) else None
            cur = probs_fn(*unit, scores)
            if pending is not None:
                pv_fn(*units[n - 1], *pending)
            scores, pending = nxt, cur
        pv_fn(*units[-1], *pending)

    acc_ref[...] = jnp.zeros(acc_ref.shape, F32)

    def chunk_pair(n, carry):
        attend([2 * n, 2 * n + 1], probs_of, pv_of)
        return carry

    lax.fori_loop(0, nch // 2, chunk_pair, 0)

    @pl.when(nch % 2 == 1)
    def _():
        attend([nch - 1], probs_of, pv_of)

    sums = jnp.concatenate([acc_ref[h, SUM_ROW:SUM_ROW + 1, :] for h in range(N_ATT_HEADS)], axis=0)
    underflow = jnp.where(sums >= MIN_SOFTMAX_SUM, 0, 1)

    @pl.when(jnp.max(underflow) > 0)
    def _():
        m_ref[...] = jnp.full(m_ref.shape, -jnp.inf, F32)
        acc_ref[...] = jnp.zeros(acc_ref.shape, F32)

        def chunk_online(c, carry):
            attend([c], online_probs_of, online_pv_of)
            return carry

        lax.fori_loop(0, nch, chunk_online, 0)

    for h in range(N_ATT_HEADS):
        ot_ref[h * hd:(h + 1) * hd, :] = acc_ref[h, :hd, :] / acc_ref[h, SUM_ROW:SUM_ROW + 1, :]
    o_ref[0] = ot_ref[...].T.astype(BF16)


def _attn_call(ki, qit, tailt, k4, vt5, qt, topk):
    bsz, seq, _ = ki.shape
    tq = vt5.shape[-1]
    nch = seq // tq
    assert tq >= topk and vt5.shape[2] == nch
    return pl.pallas_call(
        functools.partial(_attn_kernel, topk=topk),
        grid=(bsz, nch),
        in_specs=[
            pl.BlockSpec((1, seq, IDX_DIM), lambda b, j: (b, 0, 0)),
            pl.BlockSpec((1, IDX_HEADS * IDX_DIM, tq), lambda b, j: (b, 0, j)),
            pl.BlockSpec((1, LANES, tq), lambda b, j: (b, 0, j)),
            pl.BlockSpec((1, N_KV_HEADS, seq, LANES), lambda b, j: (b, 0, 0, 0)),
            pl.BlockSpec((1, N_KV_HEADS, nch, V_ROWS, tq), lambda b, j: (b, 0, 0, 0, 0)),
            pl.BlockSpec((1, ATT_DIM, tq), lambda b, j: (b, 0, j)),
        ],
        out_specs=pl.BlockSpec((1, tq, ATT_DIM), lambda b, j: (b, j, 0)),
        out_shape=jax.ShapeDtypeStruct((bsz, seq, ATT_DIM), BF16),
        scratch_shapes=[pltpu.VMEM((nch, tq, tq), F32), pltpu.VMEM((nch, tq, tq), BF16),
                        pltpu.VMEM((nch, tq, tq), BF16),
                        pltpu.VMEM((1, tq), I32), pltpu.VMEM((ATT_DIM, tq), F32),
                        pltpu.VMEM((N_ATT_HEADS, tq), F32),
                        pltpu.VMEM((N_ATT_HEADS, V_ROWS, tq), F32),
                        pltpu.VMEM((N_ATT_HEADS, LANES, tq), BF16),
                        pltpu.VMEM((N_KV_HEADS, tq), F32)],
        compiler_params=pltpu.CompilerParams(dimension_semantics=("parallel", "arbitrary"),
                                             vmem_limit_bytes=VMEM_LIMIT_BYTES),
        name="attn",
    )(ki, qit, tailt, k4, vt5, qt)


def _split2(v):
    hi = v.astype(BF16)
    mid = (v - hi.astype(F32)).astype(BF16)
    return jnp.concatenate([hi, mid], axis=1)


def _ssd_kernel(xbc_ref, z_ref, tail_ref, cw_ref, cb_ref, alog_ref, e2_ref,
                dx_ref, nw_ref, out_ref, carry, state, ybuf):
    L = CHUNK
    hpg = SSD_HEADS // SSD_GROUPS
    gw = hpg * SSD_HEAD_DIM

    nshift = CONV_W - 1
    carry_rows = nshift * SUBLANES

    @pl.when(pl.program_id(1) == 0)
    def _():
        carry[...] = jnp.zeros_like(carry)
        state[...] = jnp.zeros_like(state)

    u = xbc_ref[0]
    sub8 = lax.broadcasted_iota(I32, (SUBLANES, XBC_DIM), 0)
    wrapped = []
    for k in range(nshift):
        rows = slice(L - carry_rows + k * SUBLANES, L - carry_rows + (k + 1) * SUBLANES)
        merged = jnp.where(sub8 == SUBLANES - 1, carry[k * SUBLANES:(k + 1) * SUBLANES, :], u[rows, :])
        wrapped.append(pltpu.roll(merged, 1, 0))
    carry[...] = u[L - carry_rows:, :]
    conv = cb_ref[...] + cw_ref[CONV_W - 1:CONV_W, :] * u
    for d in range(1, CONV_W):
        shifted = jnp.concatenate(wrapped[nshift - d:] + [u[:L - d * SUBLANES, :]], axis=0)
        conv = conv + cw_ref[CONV_W - 1 - d:CONV_W - d, :] * shifted
    act = _silu(conv)
    xs = act[:, :D_INNER]

    row = lax.broadcasted_iota(I32, (L, L), 0)
    col = lax.broadcasted_iota(I32, (L, L), 1)
    time_of = lambda r: PERM_STRIDE * (r & (SUBLANES - 1)) + (r >> (SUBLANES.bit_length() - 1))
    lower = time_of(row) >= time_of(col)
    tail = tail_ref[0]
    adt = tail * (-jnp.exp(alog_ref[...]) * LOG2_E)
    hi = adt.astype(BF16)
    r1 = adt - hi.astype(F32)
    mid = r1.astype(BF16)
    lo = (r1 - mid.astype(F32)).astype(BF16)
    ones_l = jnp.where(lower, 1.0, 0.0).astype(BF16)
    a_cum = jnp.dot(jnp.concatenate([ones_l, ones_l, ones_l], axis=1),
                    jnp.concatenate([hi, mid, lo], axis=0), preferred_element_type=F32)
    a_cum_t = a_cum.T
    e2 = e2_ref[...]
    a_cum_x = jnp.dot(_split2(a_cum), e2, preferred_element_type=F32)
    dt_x = jnp.dot(_split2(tail), e2, preferred_element_type=F32)
    a_last_x = a_cum_x[L - 1:L, :]
    xf = xs * dt_x
    xf_b = xf.astype(BF16)
    xd_b = (xf * jnp.exp2(a_last_x - a_cum_x)).astype(BF16)
    ea_x = jnp.exp2(a_cum_x)
    chunk_decay_x = jnp.exp2(a_last_x)

    lane = lax.broadcasted_iota(I32, (L, LANES), 1)
    for g in range(SSD_GROUPS):
        b_f = act[:, D_INNER + g * D_STATE:D_INNER + (g + 1) * D_STATE]
        bg = b_f.astype(BF16)
        bg_t = b_f.T.astype(BF16)
        cg = act[:, D_INNER + BC_DIM + g * D_STATE:D_INNER + BC_DIM + (g + 1) * D_STATE].astype(BF16)
        gs = slice(g * gw, (g + 1) * gw)
        cb = lax.dot_general(cg, bg, (((1,), (1,)), ((), ())), preferred_element_type=F32)
        st_prev = state[:, gs]
        y_off = jnp.dot(cg, st_prev.astype(BF16), preferred_element_type=F32) * ea_x[:, gs]
        st_new = jnp.dot(bg_t, xd_b[:, gs], preferred_element_type=F32)
        state[:, gs] = st_prev * chunk_decay_x[:, gs] + st_new
        for pr in range(hpg // 2):
            h0 = g * hpg + 2 * pr
            xs_pair = xf_b[:, h0 * SSD_HEAD_DIM:(h0 + 2) * SSD_HEAD_DIM]
            ys = []
            for h in (h0, h0 + 1):
                seg = a_cum[:, DT_OFF + h:DT_OFF + h + 1] - a_cum_t[DT_OFF + h:DT_OFF + h + 1, :]
                lmat = jnp.exp2(jnp.where(lower, seg, -jnp.inf))
                ys.append(jnp.dot((cb * lmat).astype(BF16), xs_pair, preferred_element_type=F32))
            ybuf[h0 // 2] = (jnp.where(lane < SSD_HEAD_DIM, ys[0], ys[1])
                             + y_off[:, 2 * pr * SSD_HEAD_DIM:(2 * pr + 2) * SSD_HEAD_DIM])

    ntile = D_INNER // LANES
    y = jnp.concatenate([ybuf[j] for j in range(ntile)], axis=1) + dx_ref[...] * xs
    zf = z_ref[0].astype(F32)
    yz = y * _silu(zf)
    for g in range(SSD_GROUPS):
        gs = slice(g * gw, (g + 1) * gw)
        yg = yz[:, gs]
        msq = jnp.mean(yg * yg, axis=-1, keepdims=True)
        og = yg * lax.rsqrt(msq + LN_EPS) * nw_ref[:, gs]
        for j in range(gw // LANES):
            ybuf[g * (gw // LANES) + j] = og[:, j * LANES:(j + 1) * LANES]
    half = L // 2
    out_ref[0] = jnp.concatenate(
        [jnp.concatenate([ybuf[j, pl.ds(half * (t % 2) + t // 2, SUBLANES, stride=SUBLANES), :]
                          for j in range(ntile)], axis=1)
         for t in range(PERM_STRIDE)], axis=0).astype(BF16)


def _ssd_call(xbc, z, tail, cw, cb, alog, e2, dx, nw):
    bsz, seq, _ = xbc.shape
    L = CHUNK
    blk = lambda n: pl.BlockSpec((1, L, n), lambda b, c: (b, c, 0))
    consts = (cw, cb, alog, e2, dx, nw)
    return pl.pallas_call(
        _ssd_kernel,
        grid=(bsz, seq // L),
        in_specs=[blk(XBC_DIM), blk(D_INNER), blk(LANES)] + [_const_spec(c.shape) for c in consts],
        out_specs=blk(D_INNER),
        out_shape=jax.ShapeDtypeStruct((bsz, seq, D_INNER), BF16),
        scratch_shapes=[pltpu.VMEM(((CONV_W - 1) * SUBLANES, XBC_DIM), F32),
                        pltpu.VMEM((D_STATE, D_INNER), F32),
                        pltpu.VMEM((D_INNER // LANES, L, LANES), F32)],
        compiler_params=pltpu.CompilerParams(dimension_semantics=("parallel", "arbitrary"),
                                             vmem_limit_bytes=VMEM_LIMIT_BYTES),
        name="ssd",
    )(xbc, z, tail, *consts)


def _layer_norm(v, g, b):
    mu = jnp.mean(v, axis=-1, keepdims=True)
    d = v - mu
    var = jnp.mean(d * d, axis=-1, keepdims=True)
    return d * lax.rsqrt(var + LN_EPS) * g + b


def _mix_kernel(att_ref, ssd_ref, g_ref, x_ref, wa_ref, ws_ref, wo_ref, l1g_ref, l1b_ref,
                wu_ref, wd_ref, l2g_ref, l2b_ref, o_ref, *, alpha, ff_chunk):
    tm = x_ref.shape[0]
    subs = [slice(s * MIX_SUB_ROWS, (s + 1) * MIX_SUB_ROWS) for s in range(tm // MIX_SUB_ROWS)]
    mixed = []
    for r in subs:
        ga = g_ref[r, :D_MODEL].astype(F32)
        gs = g_ref[r, D_MODEL:].astype(F32)
        mixed.append(
            _sigmoid(ga) * jnp.dot(att_ref[r, :], wa_ref[...], preferred_element_type=F32)
            + _sigmoid(gs) * jnp.dot(ssd_ref[r, :], ws_ref[...], preferred_element_type=F32))
    hs = []
    for r, mx in zip(subs, mixed):
        proj = jnp.dot(mx.astype(BF16), wo_ref[...], preferred_element_type=F32)
        hs.append(_layer_norm(alpha * x_ref[r, :] + proj, l1g_ref[...], l1b_ref[...]))
    hbs = [h.astype(BF16) for h in hs]
    ffs = [jnp.zeros_like(h) for h in hs]
    for c in range(D_FF // ff_chunk):
        cs = slice(c * ff_chunk, (c + 1) * ff_chunk)
        for s in range(len(subs)):
            u = jnp.maximum(jnp.dot(hbs[s], wu_ref[:, cs], preferred_element_type=F32), 0.0)
            ffs[s] = ffs[s] + jnp.dot((u * u).astype(BF16), wd_ref[cs, :],
                                      preferred_element_type=F32)
    for r, h, ff in zip(subs, hs, ffs):
        o_ref[r, :] = _layer_norm(alpha * h + ff, l2g_ref[...], l2b_ref[...])


def _mix_call(att, ssd, gates, x2, wa, ws, wo, l1g, l1b, wu, wd, l2g, l2b, alpha, tm):
    m = x2.shape[0]
    row = lambda n: pl.BlockSpec((tm, n), lambda i: (i, 0))
    consts = (wa, ws, wo, l1g, l1b, wu, wd, l2g, l2b)
    return pl.pallas_call(
        functools.partial(_mix_kernel, alpha=alpha, ff_chunk=1024),
        grid=(m // tm,),
        in_specs=[row(ATT_DIM), row(D_INNER), row(2 * D_MODEL), row(D_MODEL)]
                 + [_const_spec(c.shape) for c in consts],
        out_specs=row(D_MODEL),
        out_shape=jax.ShapeDtypeStruct((m, D_MODEL), F32),
        compiler_params=pltpu.CompilerParams(dimension_semantics=("parallel",),
                                             vmem_limit_bytes=VMEM_LIMIT_BYTES),
        name="mix",
    )(att, ssd, gates, x2, *consts)


def _rope_tables(seq):
    half = ROT_DIM // 2
    inv = ROPE_THETA ** (-jnp.arange(0, ROT_DIM, 2, dtype=F32) / ROT_DIM)
    ang = jnp.arange(seq, dtype=F32)[:, None] * inv[None, :]
    cos, sin = jnp.cos(ang), jnp.sin(ang)
    pad = jnp.zeros((seq, ATT_HEAD_DIM - ROT_DIM), F32)
    zero = jnp.zeros((seq, half), F32)
    rc = jnp.concatenate([cos, cos, pad + 1.0], -1)
    rs1 = jnp.concatenate([-sin, zero, pad], -1)
    rs2 = jnp.concatenate([zero, sin, pad], -1)
    rep = LANES // ATT_HEAD_DIM
    return tuple(jnp.tile(t, (1, rep)) for t in (rc, rs1, rs2))


def _lane_row(v, off):
    return jnp.zeros((1, LANES), F32).at[0, off:off + v.shape[0]].set(v)


def _layer(x, tables, topk, alpha, w_in, q_norm_w, w_uq, w_iq, k_idx_norm_g, k_idx_norm_b,
           conv_w, conv_b, dt_bias, a_log, d_skip, ssd_norm_w, w_attn_branch, w_ssd_branch,
           w_out, ln1_g, ln1_b, w_up, w_down, ln2_g, ln2_b):
    bsz, seq, _ = x.shape
    m = bsz * seq

    offs = [0]
    for s in IN_SIZES:
        offs.append(offs[-1] + s)
    seg = lambda i: w_in[:, offs[i]:offs[i + 1]]
    pad = jnp.zeros((D_MODEL, LANES - TAIL_END), F32)
    wa = jnp.concatenate([seg(0), seg(1), seg(2), seg(3), seg(4), seg(7), pad], 1).astype(BF16)
    wz, wx, wg = seg(5).astype(BF16), seg(6).astype(BF16), seg(8).astype(BF16)

    qt, qit, k4, vt5, ki, tail, tailt, z, xbc, gates = _proj_call(
        x, wa, wz, wx, wg, q_norm_w.reshape(1, Q_LORA), w_uq.astype(BF16), w_iq.astype(BF16),
        _lane_row(k_idx_norm_g, 0), _lane_row(k_idx_norm_b, 0), _lane_row(dt_bias, DT_OFF), *tables)

    att = _attn_call(ki, qit, tailt, k4, vt5, qt, topk)

    slot = jnp.arange(LANES)[:, None] - DT_OFF
    expand = (slot == jnp.arange(D_INNER)[None, :] // SSD_HEAD_DIM).astype(BF16)
    e2 = jnp.concatenate([expand, expand], axis=0)
    ssd = _ssd_call(xbc, z, tail, conv_w, conv_b.reshape(1, XBC_DIM), _lane_row(a_log, DT_OFF), e2,
                    jnp.repeat(d_skip, SSD_HEAD_DIM).reshape(1, D_INNER),
                    ssd_norm_w.reshape(1, D_INNER))

    r = lambda t: t.reshape(1, -1)
    out = _mix_call(att.reshape(m, ATT_DIM), ssd.reshape(m, D_INNER), gates.reshape(m, 2 * D_MODEL),
                    x.reshape(m, D_MODEL), w_attn_branch.astype(BF16), w_ssd_branch.astype(BF16),
                    w_out.astype(BF16), r(ln1_g), r(ln1_b), w_up.astype(BF16), w_down.astype(BF16),
                    r(ln2_g), r(ln2_b), alpha, tm=MIX_ROWS)
    return out.reshape(bsz, seq, D_MODEL)


def kernel(x, w_in, q_norm_w, w_uq, w_iq, k_idx_norm_g, k_idx_norm_b, conv_w, conv_b, dt_bias,
           a_log, d_skip, ssd_norm_w, w_attn_branch, w_ssd_branch, w_out, ln1_g, ln1_b, w_up,
           w_down, ln2_g, ln2_b):
    depth = w_in.shape[0]
    seq = x.shape[1]
    topk = min(TOPK_MAX, seq // 4)
    alpha = (2 * depth) ** 0.25
    tables = _rope_tables(seq)
    h = x
    for l in range(depth):
        h = _layer(h, tables, topk, alpha, w_in[l], q_norm_w[l], w_uq[l], w_iq[l], k_idx_norm_g[l],
                   k_idx_norm_b[l], conv_w[l], conv_b[l], dt_bias[l], a_log[l], d_skip[l],
                   ssd_norm_w[l], w_attn_branch[l], w_ssd_branch[l], w_out[l], ln1_g[l], ln1_b[l],
                   w_up[l], w_down[l], ln2_g[l], ln2_b[l])
    return h
```

```python
import functools

import jax
import jax.numpy as jnp
from jax import lax
from jax.experimental import pallas as pl
from jax.experimental.pallas import tpu as pltpu

F32 = jnp.float32
BF16 = jnp.bfloat16
I32 = jnp.int32

D_MODEL = 1024
N_ATT_HEADS = 16
ATT_HEAD_DIM = 64
N_KV_HEADS = 4
ATT_DIM = N_ATT_HEADS * ATT_HEAD_DIM
KV_DIM = N_KV_HEADS * ATT_HEAD_DIM
Q_LORA = 256
IDX_HEADS = 8
IDX_DIM = 64
TOPK_MAX = 256
ROT_DIM = ATT_HEAD_DIM // 4
ROPE_THETA = 500000.0
D_INNER = 2 * D_MODEL
SSD_HEAD_DIM = 64
SSD_HEADS = D_INNER // SSD_HEAD_DIM
SSD_GROUPS = 4
D_STATE = 128
CONV_W = 4
CHUNK = 128
BC_DIM = SSD_GROUPS * D_STATE
XBC_DIM = D_INNER + 2 * BC_DIM
D_FF = 4 * D_MODEL
IN_SIZES = (Q_LORA, KV_DIM, KV_DIM, IDX_DIM, IDX_HEADS, D_INNER, XBC_DIM, SSD_HEADS, 2 * D_MODEL)
LN_EPS = 1e-5

LANES = 128
SUBLANES = 8
VMEM_LIMIT_BYTES = 56 * 1024 * 1024

SEQ_TILE = 256
MIX_ROWS = 512
MIX_SUB_ROWS = 256

SMALL_COLS = Q_LORA + 2 * KV_DIM + LANES
TAIL_OFF = Q_LORA + 2 * KV_DIM
WI_OFF = IDX_DIM
DT_OFF = IDX_DIM + IDX_HEADS
TAIL_END = DT_OFF + SSD_HEADS

PERM_STRIDE = CHUNK // SUBLANES
BF16_SUBLANES = 16
V_ROWS = ATT_HEAD_DIM + BF16_SUBLANES
SUM_ROW = ATT_HEAD_DIM
SHIFT_MARGIN = 1.0 + 2.0 ** -6
MIN_SOFTMAX_SUM = 2.0 ** -40
LOG2_E = 1.4426950408889634
INT_MIN = -(2 ** 31)


def _sigmoid(v):
    return 0.5 + 0.5 * jnp.tanh(0.5 * v)


def _silu(v):
    h = 0.5 * v
    return h + h * jnp.tanh(h)


def _const_spec(shape):
    zeros = (0,) * len(shape)
    return pl.BlockSpec(shape, lambda *_: zeros, pipeline_mode=pl.Buffered(1))


def _proj_kernel(x_ref, wa_ref, wz_ref, wx_ref, wg_ref, qnw_ref, wuq_ref, wiq_ref,
                 kng_ref, knb_ref, dtb_ref, rc_ref, rs1_ref, rs2_ref,
                 qt_ref, qit_ref, k4_ref, vt_ref, ki_ref, tail_ref, tailt_ref,
                 z_ref, xbc_ref, g_ref, perm_ref):
    xv = x_ref[0]
    xb = xv.astype(BF16)
    rc, rs1, rs2 = rc_ref[...], rs1_ref[...], rs2_ref[...]
    hd = ATT_HEAD_DIM

    ntile = D_MODEL // LANES
    for j in range(ntile):
        perm_ref[j] = xv[:, j * LANES:(j + 1) * LANES]
    xp = jnp.concatenate(
        [jnp.concatenate([perm_ref[j, pl.ds(k * CHUNK + i, SUBLANES, stride=PERM_STRIDE), :]
                          for j in range(ntile)], axis=1)
         for k in range(xv.shape[0] // CHUNK) for i in range(PERM_STRIDE)], axis=0).astype(BF16)

    def rope(t):
        half = ROT_DIM // 2
        return (t * rc + pltpu.roll(t, LANES - half, 1) * rs1 + pltpu.roll(t, half, 1) * rs2)

    pa = jnp.dot(xb, wa_ref[...], preferred_element_type=F32)

    lane = lax.broadcasted_iota(I32, (xv.shape[0], LANES), 1)
    dtr_p = jnp.dot(xp, wa_ref[:, TAIL_OFF:TAIL_OFF + LANES], preferred_element_type=F32) + dtb_ref[...]
    dt_p = jnp.maximum(dtr_p, 0.0) + jnp.log1p(jnp.exp(-jnp.abs(dtr_p)))
    tail_ref[0] = jnp.where((lane >= DT_OFF) & (lane < TAIL_END), dt_p, 0.0)
    z_ref[0] = jnp.dot(xp, wz_ref[...], preferred_element_type=F32).astype(BF16)
    xbc_ref[0] = jnp.dot(xp, wx_ref[...], preferred_element_type=F32)

    c_q = pa[:, :Q_LORA]
    ms = jnp.mean(c_q * c_q, axis=-1, keepdims=True)
    cq = (c_q * lax.rsqrt(ms + LN_EPS) * qnw_ref[...]).astype(BF16)

    scale = ATT_HEAD_DIM ** -0.5 * LOG2_E
    qf = jnp.dot(cq, wuq_ref[...], preferred_element_type=F32)
    for j in range(ATT_DIM // LANES):
        sl = slice(j * LANES, (j + 1) * LANES)
        qt_ref[0, sl, :] = (rope(qf[:, sl]) * scale).T.astype(BF16)
    qif = jnp.dot(cq, wiq_ref[...], preferred_element_type=F32)
    iscale = IDX_DIM ** -0.5
    for j in range(IDX_HEADS * IDX_DIM // LANES):
        sl = slice(j * LANES, (j + 1) * LANES)
        qit_ref[0, sl, :] = (rope(qif[:, sl]) * iscale).T.astype(BF16)
    for j in range(KV_DIM // LANES):
        kr = rope(pa[:, Q_LORA + j * LANES:Q_LORA + (j + 1) * LANES])
        for i, kh in enumerate((kr, pltpu.roll(kr, hd, 1))):
            k4_ref[0, 2 * j + i] = jnp.where(lane < hd, kh, jnp.where(lane == hd, 1.0, 0.0)).astype(BF16)
        vtr = pa[:, Q_LORA + KV_DIM + j * LANES:Q_LORA + KV_DIM + (j + 1) * LANES].T
        sum_rows = jnp.where(lax.broadcasted_iota(I32, (V_ROWS - hd, vtr.shape[1]), 0) == 0,
                             1.0, 0.0).astype(BF16)
        vt_ref[0, 2 * j, 0] = jnp.concatenate([vtr[:hd, :].astype(BF16), sum_rows], axis=0)
        vt_ref[0, 2 * j + 1, 0] = jnp.concatenate([vtr[hd:, :].astype(BF16), sum_rows], axis=0)

    tail = pa[:, TAIL_OFF:TAIL_OFF + LANES]
    is_ki = lane < IDX_DIM
    mu = jnp.sum(jnp.where(is_ki, tail, 0.0), axis=-1, keepdims=True) * (1.0 / IDX_DIM)
    dv = jnp.where(is_ki, tail - mu, 0.0)
    var = jnp.sum(dv * dv, axis=-1, keepdims=True) * (1.0 / IDX_DIM)
    ki = rope(dv * lax.rsqrt(var + LN_EPS) * kng_ref[...] + knb_ref[...])
    wi = tail * (IDX_HEADS ** -0.5)
    ki_ref[0] = ki[:, :IDX_DIM].astype(BF16)
    tailt_ref[0] = jnp.where(is_ki, ki, jnp.where(lane < DT_OFF, wi, 0.0)).T

    g_ref[0] = jnp.dot(xb, wg_ref[...], preferred_element_type=F32).astype(BF16)


def _proj_call(x, wa, wz, wx, wg, qnw, wuq, wiq, kng, knb, dtb, rc, rs1, rs2):
    bsz, seq, _ = x.shape
    tm = SEQ_TILE
    nsb = seq // tm
    row = lambda n: pl.BlockSpec((1, tm, n), lambda b, i: (b, i, 0))
    col = lambda n: pl.BlockSpec((1, n, tm), lambda b, i: (b, 0, i))
    tab = pl.BlockSpec((tm, LANES), lambda b, i: (i, 0))
    sds = jax.ShapeDtypeStruct
    out_shapes = (
        sds((bsz, ATT_DIM, seq), BF16),
        sds((bsz, IDX_HEADS * IDX_DIM, seq), BF16),
        sds((bsz, N_KV_HEADS, seq, LANES), BF16),
        sds((bsz, N_KV_HEADS, nsb, V_ROWS, tm), BF16),
        sds((bsz, seq, IDX_DIM), BF16),
        sds((bsz, seq, LANES), F32),
        sds((bsz, LANES, seq), F32),
        sds((bsz, seq, D_INNER), BF16),
        sds((bsz, seq, XBC_DIM), F32),
        sds((bsz, seq, 2 * D_MODEL), BF16),
    )
    out_specs = [
        col(ATT_DIM), col(IDX_HEADS * IDX_DIM),
        pl.BlockSpec((1, N_KV_HEADS, tm, LANES), lambda b, i: (b, 0, i, 0)),
        pl.BlockSpec((1, N_KV_HEADS, 1, V_ROWS, tm), lambda b, i: (b, 0, i, 0, 0)),
        row(IDX_DIM), row(LANES), col(LANES), row(D_INNER), row(XBC_DIM), row(2 * D_MODEL),
    ]
    consts = (wa, wz, wx, wg, qnw, wuq, wiq, kng, knb, dtb)
    return pl.pallas_call(
        _proj_kernel,
        grid=(bsz, nsb),
        in_specs=[row(D_MODEL)] + [_const_spec(c.shape) for c in consts] + [tab, tab, tab],
        out_specs=out_specs,
        out_shape=out_shapes,
        scratch_shapes=[pltpu.VMEM((D_MODEL // LANES, tm, LANES), F32)],
        compiler_params=pltpu.CompilerParams(dimension_semantics=("parallel", "parallel"),
                                             vmem_limit_bytes=VMEM_LIMIT_BYTES),
        name="proj",
    )(x, *consts, rc, rs1, rs2)


COUNT_ROWS = 32


def _fold_rows(v, op, rows=32):
    while v.shape[0] > rows:
        half = v.shape[0] // 2
        v = op(v[:half], v[half:])
    return v


def _attn_kernel(ki_ref, qit_ref, wit_ref, k_ref, vt_ref, qt_ref, o_ref,
                 score_ref, coarse_ref, mask_ref, ot_ref, m_ref, acc_ref, qa_ref, kmax_ref,
                 *, topk):
    tq = qt_ref.shape[2]
    tk = tq
    qblk = pl.program_id(1)
    nch = qblk + 1
    sub = lax.broadcasted_iota(I32, (tk, tq), 0)
    tpos = qblk * tq + lax.broadcasted_iota(I32, (tk, tq), 1)

    def score_chunk(c, carry):
        off = pl.multiple_of(c * tk, tk)
        kic = ki_ref[0, pl.ds(off, tk), :]
        acc = jnp.zeros((tk, tq), F32)
        for hh in range(IDX_HEADS):
            sc = jnp.dot(kic, qit_ref[0, hh * IDX_DIM:(hh + 1) * IDX_DIM, :],
                         preferred_element_type=F32)
            acc = acc + wit_ref[0, WI_OFF + hh:WI_OFF + hh + 1, :] * jnp.maximum(sc, 0.0)
        masked = jnp.where(off + sub <= tpos, acc, -jnp.inf)
        score_ref[c] = masked
        coarse_ref[c] = masked.astype(BF16)
        return carry

    lax.fori_loop(0, nch, score_chunk, 0)

    def count(pred, ref=score_ref, dtype=I32):
        def body(c, part):
            hit = jnp.where(pred(ref[c], c), jnp.ones((), dtype), jnp.zeros((), dtype))
            return part + _fold_rows(hit, jnp.add, COUNT_ROWS)

        part = lax.fori_loop(0, nch, body, jnp.zeros((COUNT_ROWS, tq), dtype))
        return jnp.sum(part.astype(I32), axis=0, keepdims=True)

    def ordered_float(u):
        key = u ^ jnp.int32(INT_MIN)
        return pltpu.bitcast(jnp.where(key < 0, key ^ jnp.int32(0x7FFFFFFF), key), F32)

    half_bits = 16
    neg_fill = jnp.int32(2 ** half_bits - 1)

    def coarse_key(u16):
        return lax.shift_left(u16, half_bits) | jnp.where(u16 >= 2 ** (half_bits - 1), 0, neg_fill)

    def coarse_search(i, u16):
        cand = u16 | lax.shift_left(jnp.int32(1), half_bits - 1 - i)
        cand_b = ordered_float(coarse_key(cand)).astype(BF16)
        n = count(lambda sc, c: sc >= cand_b, coarse_ref, BF16)
        return jnp.where(n >= topk, cand, u16)

    u16 = lax.fori_loop(0, half_bits, coarse_search, jnp.zeros((1, tq), I32))
    base_u = coarse_key(u16) - jnp.int32(2 ** half_bits)

    def fine_search(i, d):
        cand = d | lax.shift_left(jnp.int32(1), half_bits - i)
        cand_f = ordered_float(base_u + cand)
        return jnp.where(count(lambda sc, c: sc >= cand_f) >= topk, cand, d)

    thr = ordered_float(base_u + lax.fori_loop(0, half_bits + 1, fine_search,
                                               jnp.zeros((1, tq), I32)))
    thr = jnp.where(tpos[0:1, :] < topk, -jnp.inf, thr)

    need = (topk - count(lambda sc, c: sc > thr)).astype(F32)
    tri = jnp.where(sub >= lax.broadcasted_iota(I32, (tk, tk), 1), 1.0, 0.0).astype(BF16)

    def mask_chunk(c, ties_before):
        sc = score_ref[c]
        tie = jnp.where(sc == thr, 1.0, 0.0)
        tie_rank = jnp.dot(tri, tie.astype(BF16), preferred_element_type=F32) + ties_before
        sel = jnp.where(sc > thr, 1.0, jnp.where(tie_rank <= need, tie, 0.0))
        mask_ref[c] = jnp.where(c * tk + sub <= tpos, sel, 0.0).astype(BF16)
        return tie_rank[tk - 1:tk, :]

    lax.fori_loop(0, nch, mask_chunk, jnp.zeros((1, tq), F32))

    grp = N_ATT_HEADS // N_KV_HEADS
    hd = ATT_HEAD_DIM
    klane = lax.broadcasted_iota(I32, (k_ref.shape[2], LANES), 1)

    @pl.when(qblk == 0)
    def _():
        for g in range(N_KV_HEADS):
            kf = k_ref[0, g].astype(F32)
            norm2 = jnp.sum(jnp.where(klane < hd, kf * kf, 0.0), axis=1, keepdims=True)
            kmax_ref[g:g + 1, :] = jnp.broadcast_to(jnp.max(norm2, axis=0, keepdims=True), (1, tq))

    shift_row = lax.broadcasted_iota(I32, (LANES - hd, tq), 0) == 0
    for h in range(N_ATT_HEADS):
        qs = qt_ref[0, h * hd:(h + 1) * hd, :]
        qf = qs.astype(F32)
        r = jnp.sqrt(jnp.sum(qf * qf, axis=0, keepdims=True) * kmax_ref[h // grp:h // grp + 1, :])
        r = r * SHIFT_MARGIN
        qa_ref[h] = jnp.concatenate([qs, jnp.where(shift_row, -r, 0.0).astype(BF16)], axis=0)

    def scores_of(c, g):
        kc = k_ref[0, g, pl.ds(pl.multiple_of(c * tk, tk), tk), :]
        return [jnp.dot(kc, qa_ref[g * grp + i], preferred_element_type=F32) for i in range(grp)]

    def probs_of(c, g, scores):
        mk = mask_ref[c]
        return ([jnp.exp2(s).astype(BF16) * mk for s in scores],)

    def pv_of(c, g, probs):
        vc = vt_ref[0, g, c]
        for i in range(grp):
            h = g * grp + i
            acc_ref[h] = acc_ref[h] + jnp.dot(vc, probs[i], preferred_element_type=F32)

    def online_probs_of(c, g, scores):
        bias = jnp.where(mask_ref[c].astype(F32) > 0.0, 0.0, -jnp.inf)
        probs, alphas = [], []
        for i in range(grp):
            h = g * grp + i
            m = m_ref[h:h + 1, :]
            s = scores[i] + bias
            m_new = jnp.maximum(m, jnp.max(_fold_rows(s, jnp.maximum), axis=0, keepdims=True))
            m_safe = jnp.where(m_new == -jnp.inf, 0.0, m_new)
            alphas.append(jnp.exp2(m - m_safe))
            probs.append(jnp.exp2(s - m_safe).astype(BF16))
            m_ref[h:h + 1, :] = m_new
        return probs, alphas

    def online_pv_of(c, g, probs, alphas):
        vc = vt_ref[0, g, c]
        for i in range(grp):
            h = g * grp + i
            acc_ref[h] = alphas[i] * acc_ref[h] + jnp.dot(vc, probs[i], preferred_element_type=F32)

    def attend(chunks, probs_fn, pv_fn):
        units = [(c, g) for c in chunks for g in range(N_KV_HEADS)]
        scores = scores_of(*units[0])
        pending = None
        for n, unit in enumerate(units):
            nxt = scores_of(*units[n + 1]) if n + 1 < len(units) else None
            cur = probs_fn(*unit, scores)
            if pending is not None:
                pv_fn(*units[n - 1], *pending)
            scores, pending = nxt, cur
        pv_fn(*units[-1], *pending)

    acc_ref[...] = jnp.zeros(acc_ref.shape, F32)

    def chunk_pair(n, carry):
        attend([2 * n, 2 * n + 1], probs_of, pv_of)
        return carry

    lax.fori_loop(0, nch // 2, chunk_pair, 0)

    @pl.when(nch % 2 == 1)
    def _():
        attend([nch - 1], probs_of, pv_of)

    sums = jnp.concatenate([acc_ref[h, SUM_ROW:SUM_ROW + 1, :] for h in range(N_ATT_HEADS)], axis=0)
    underflow = jnp.where(sums >= MIN_SOFTMAX_SUM, 0, 1)

    @pl.when(jnp.max(underflow) > 0)
    def _():
        m_ref[...] = jnp.full(m_ref.shape, -jnp.inf, F32)
        acc_ref[...] = jnp.zeros(acc_ref.shape, F32)

        def chunk_online(c, carry):
            attend([c], online_probs_of, online_pv_of)
            return carry

        lax.fori_loop(0, nch, chunk_online, 0)

    for h in range(N_ATT_HEADS):
        ot_ref[h * hd:(h + 1) * hd, :] = acc_ref[h, :hd, :] / acc_ref[h, SUM_ROW:SUM_ROW + 1, :]
    o_ref[0] = ot_ref[...].T.astype(BF16)


def _attn_call(ki, qit, tailt, k4, vt5, qt, topk):
    bsz, seq, _ = ki.shape
    tq = vt5.shape[-1]
    nch = seq // tq
    assert tq >= topk and vt5.shape[2] == nch
    return pl.pallas_call(
        functools.partial(_attn_kernel, topk=topk),
        grid=(bsz, nch),
        in_specs=[
            pl.BlockSpec((1, seq, IDX_DIM), lambda b, j: (b, 0, 0)),
            pl.BlockSpec((1, IDX_HEADS * IDX_DIM, tq), lambda b, j: (b, 0, j)),
            pl.BlockSpec((1, LANES, tq), lambda b, j: (b, 0, j)),
            pl.BlockSpec((1, N_KV_HEADS, seq, LANES), lambda b, j: (b, 0, 0, 0)),
            pl.BlockSpec((1, N_KV_HEADS, nch, V_ROWS, tq), lambda b, j: (b, 0, 0, 0, 0)),
            pl.BlockSpec((1, ATT_DIM, tq), lambda b, j: (b, 0, j)),
        ],
        out_specs=pl.BlockSpec((1, tq, ATT_DIM), lambda b, j: (b, j, 0)),
        out_shape=jax.ShapeDtypeStruct((bsz, seq, ATT_DIM), BF16),
        scratch_shapes=[pltpu.VMEM((nch, tq, tq), F32), pltpu.VMEM((nch, tq, tq), BF16),
                        pltpu.VMEM((nch, tq, tq), BF16),
                        pltpu.VMEM((ATT_DIM, tq), F32),
                        pltpu.VMEM((N_ATT_HEADS, tq), F32),
                        pltpu.VMEM((N_ATT_HEADS, V_ROWS, tq), F32),
                        pltpu.VMEM((N_ATT_HEADS, LANES, tq), BF16),
                        pltpu.VMEM((N_KV_HEADS, tq), F32)],
        compiler_params=pltpu.CompilerParams(dimension_semantics=("parallel", "arbitrary"),
                                             vmem_limit_bytes=VMEM_LIMIT_BYTES),
        name="attn",
    )(ki, qit, tailt, k4, vt5, qt)


def _split2(v):
    hi = v.astype(BF16)
    mid = (v - hi.astype(F32)).astype(BF16)
    return jnp.concatenate([hi, mid], axis=1)


def _ssd_kernel(xbc_ref, z_ref, tail_ref, cw_ref, cb_ref, alog_ref, e2_ref,
                dx_ref, nw_ref, out_ref, carry, state, ybuf):
    L = CHUNK
    hpg = SSD_HEADS // SSD_GROUPS
    gw = hpg * SSD_HEAD_DIM

    nshift = CONV_W - 1
    carry_rows = nshift * SUBLANES

    @pl.when(pl.program_id(1) == 0)
    def _():
        carry[...] = jnp.zeros_like(carry)
        state[...] = jnp.zeros_like(state)

    u = xbc_ref[0]
    sub8 = lax.broadcasted_iota(I32, (SUBLANES, XBC_DIM), 0)
    wrapped = []
    for k in range(nshift):
        rows = slice(L - carry_rows + k * SUBLANES, L - carry_rows + (k + 1) * SUBLANES)
        merged = jnp.where(sub8 == SUBLANES - 1, carry[k * SUBLANES:(k + 1) * SUBLANES, :], u[rows, :])
        wrapped.append(pltpu.roll(merged, 1, 0))
    carry[...] = u[L - carry_rows:, :]
    conv = cb_ref[...] + cw_ref[CONV_W - 1:CONV_W, :] * u
    for d in range(1, CONV_W):
        shifted = jnp.concatenate(wrapped[nshift - d:] + [u[:L - d * SUBLANES, :]], axis=0)
        conv = conv + cw_ref[CONV_W - 1 - d:CONV_W - d, :] * shifted
    act = _silu(conv)
    xs = act[:, :D_INNER]

    row = lax.broadcasted_iota(I32, (L, L), 0)
    col = lax.broadcasted_iota(I32, (L, L), 1)
    time_of = lambda r: PERM_STRIDE * (r & (SUBLANES - 1)) + (r >> (SUBLANES.bit_length() - 1))
    lower = time_of(row) >= time_of(col)
    tail = tail_ref[0]
    adt = tail * (-jnp.exp(alog_ref[...]) * LOG2_E)
    hi = adt.astype(BF16)
    r1 = adt - hi.astype(F32)
    mid = r1.astype(BF16)
    lo = (r1 - mid.astype(F32)).astype(BF16)
    ones_l = jnp.where(lower, 1.0, 0.0).astype(BF16)
    a_cum = jnp.dot(jnp.concatenate([ones_l, ones_l, ones_l], axis=1),
                    jnp.concatenate([hi, mid, lo], axis=0), preferred_element_type=F32)
    a_cum_t = a_cum.T
    e2 = e2_ref[...]
    a_cum_x = jnp.dot(_split2(a_cum), e2, preferred_element_type=F32)
    dt_x = jnp.dot(_split2(tail), e2, preferred_element_type=F32)
    a_last_x = a_cum_x[L - 1:L, :]
    xf = xs * dt_x
    xf_b = xf.astype(BF16)
    xd_b = (xf * jnp.exp2(a_last_x - a_cum_x)).astype(BF16)
    ea_x = jnp.exp2(a_cum_x)
    chunk_decay_x = jnp.exp2(a_last_x)

    lane = lax.broadcasted_iota(I32, (L, LANES), 1)
    for g in range(SSD_GROUPS):
        b_f = act[:, D_INNER + g * D_STATE:D_INNER + (g + 1) * D_STATE]
        bg = b_f.astype(BF16)
        bg_t = b_f.T.astype(BF16)
        cg = act[:, D_INNER + BC_DIM + g * D_STATE:D_INNER + BC_DIM + (g + 1) * D_STATE].astype(BF16)
        gs = slice(g * gw, (g + 1) * gw)
        cb = lax.dot_general(cg, bg, (((1,), (1,)), ((), ())), preferred_element_type=F32)
        st_prev = state[:, gs]
        y_off = jnp.dot(cg, st_prev.astype(BF16), preferred_element_type=F32) * ea_x[:, gs]
        st_new = jnp.dot(bg_t, xd_b[:, gs], preferred_element_type=F32)
        state[:, gs] = st_prev * chunk_decay_x[:, gs] + st_new
        for pr in range(hpg // 2):
            h0 = g * hpg + 2 * pr
            xs_pair = xf_b[:, h0 * SSD_HEAD_DIM:(h0 + 2) * SSD_HEAD_DIM]
            ys = []
            for h in (h0, h0 + 1):
                seg = a_cum[:, DT_OFF + h:DT_OFF + h + 1] - a_cum_t[DT_OFF + h:DT_OFF + h + 1, :]
                lmat = jnp.exp2(jnp.where(lower, seg, -jnp.inf))
                ys.append(jnp.dot((cb * lmat).astype(BF16), xs_pair, preferred_element_type=F32))
            ybuf[h0 // 2] = (jnp.where(lane < SSD_HEAD_DIM, ys[0], ys[1])
                             + y_off[:, 2 * pr * SSD_HEAD_DIM:(2 * pr + 2) * SSD_HEAD_DIM])

    ntile = D_INNER // LANES
    y = jnp.concatenate([ybuf[j] for j in range(ntile)], axis=1) + dx_ref[...] * xs
    zf = z_ref[0].astype(F32)
    yz = y * _silu(zf)
    for g in range(SSD_GROUPS):
        gs = slice(g * gw, (g + 1) * gw)
        yg = yz[:, gs]
        msq = jnp.mean(yg * yg, axis=-1, keepdims=True)
        og = yg * lax.rsqrt(msq + LN_EPS) * nw_ref[:, gs]
        for j in range(gw // LANES):
            ybuf[g * (gw // LANES) + j] = og[:, j * LANES:(j + 1) * LANES]
    half = L // 2
    out_ref[0] = jnp.concatenate(
        [jnp.concatenate([ybuf[j, pl.ds(half * (t % 2) + t // 2, SUBLANES, stride=SUBLANES), :]
                          for j in range(ntile)], axis=1)
         for t in range(PERM_STRIDE)], axis=0).astype(BF16)


def _ssd_call(xbc, z, tail, cw, cb, alog, e2, dx, nw):
    bsz, seq, _ = xbc.shape
    L = CHUNK
    blk = lambda n: pl.BlockSpec((1, L, n), lambda b, c: (b, c, 0))
    consts = (cw, cb, alog, e2, dx, nw)
    return pl.pallas_call(
        _ssd_kernel,
        grid=(bsz, seq // L),
        in_specs=[blk(XBC_DIM), blk(D_INNER), blk(LANES)] + [_const_spec(c.shape) for c in consts],
        out_specs=blk(D_INNER),
        out_shape=jax.ShapeDtypeStruct((bsz, seq, D_INNER), BF16),
        scratch_shapes=[pltpu.VMEM(((CONV_W - 1) * SUBLANES, XBC_DIM), F32),
                        pltpu.VMEM((D_STATE, D_INNER), F32),
                        pltpu.VMEM((D_INNER // LANES, L, LANES), F32)],
        compiler_params=pltpu.CompilerParams(dimension_semantics=("parallel", "arbitrary"),
                                             vmem_limit_bytes=VMEM_LIMIT_BYTES),
        name="ssd",
    )(xbc, z, tail, *consts)


def _layer_norm(v, g, b):
    mu = jnp.mean(v, axis=-1, keepdims=True)
    d = v - mu
    var = jnp.mean(d * d, axis=-1, keepdims=True)
    return d * lax.rsqrt(var + LN_EPS) * g + b


def _mix_kernel(att_ref, ssd_ref, g_ref, x_ref, wa_ref, ws_ref, wo_ref, l1g_ref, l1b_ref,
                wu_ref, wd_ref, l2g_ref, l2b_ref, o_ref, *, alpha, ff_chunk):
    tm = x_ref.shape[0]
    subs = [slice(s * MIX_SUB_ROWS, (s + 1) * MIX_SUB_ROWS) for s in range(tm // MIX_SUB_ROWS)]
    mixed = []
    for r in subs:
        ga = g_ref[r, :D_MODEL].astype(F32)
        gs = g_ref[r, D_MODEL:].astype(F32)
        mixed.append(
            _sigmoid(ga) * jnp.dot(att_ref[r, :], wa_ref[...], preferred_element_type=F32)
            + _sigmoid(gs) * jnp.dot(ssd_ref[r, :], ws_ref[...], preferred_element_type=F32))
    hs = []
    for r, mx in zip(subs, mixed):
        proj = jnp.dot(mx.astype(BF16), wo_ref[...], preferred_element_type=F32)
        hs.append(_layer_norm(alpha * x_ref[r, :] + proj, l1g_ref[...], l1b_ref[...]))
    hbs = [h.astype(BF16) for h in hs]
    ffs = [jnp.zeros_like(h) for h in hs]
    for c in range(D_FF // ff_chunk):
        cs = slice(c * ff_chunk, (c + 1) * ff_chunk)
        for s in range(len(subs)):
            u = jnp.maximum(jnp.dot(hbs[s], wu_ref[:, cs], preferred_element_type=F32), 0.0)
            ffs[s] = ffs[s] + jnp.dot((u * u).astype(BF16), wd_ref[cs, :],
                                      preferred_element_type=F32)
    for r, h, ff in zip(subs, hs, ffs):
        o_ref[r, :] = _layer_norm(alpha * h + ff, l2g_ref[...], l2b_ref[...])


def _mix_call(att, ssd, gates, x2, wa, ws, wo, l1g, l1b, wu, wd, l2g, l2b, alpha, tm):
    m = x2.shape[0]
    row = lambda n: pl.BlockSpec((tm, n), lambda i: (i, 0))
    consts = (wa, ws, wo, l1g, l1b, wu, wd, l2g, l2b)
    return pl.pallas_call(
        functools.partial(_mix_kernel, alpha=alpha, ff_chunk=1024),
        grid=(m // tm,),
        in_specs=[row(ATT_DIM), row(D_INNER), row(2 * D_MODEL), row(D_MODEL)]
                 + [_const_spec(c.shape) for c in consts],
        out_specs=row(D_MODEL),
        out_shape=jax.ShapeDtypeStruct((m, D_MODEL), F32),
        compiler_params=pltpu.CompilerParams(dimension_semantics=("parallel",),
                                             vmem_limit_bytes=VMEM_LIMIT_BYTES),
        name="mix",
    )(att, ssd, gates, x2, *consts)


def _rope_tables(seq):
    half = ROT_DIM // 2
    inv = ROPE_THETA ** (-jnp.arange(0, ROT_DIM, 2, dtype=F32) / ROT_DIM)
    ang = jnp.arange(seq, dtype=F32)[:, None] * inv[None, :]
    cos, sin = jnp.cos(ang), jnp.sin(ang)
    pad = jnp.zeros((seq, ATT_HEAD_DIM - ROT_DIM), F32)
    zero = jnp.zeros((seq, half), F32)
    rc = jnp.concatenate([cos, cos, pad + 1.0], -1)
    rs1 = jnp.concatenate([-sin, zero, pad], -1)
    rs2 = jnp.concatenate([zero, sin, pad], -1)
    rep = LANES // ATT_HEAD_DIM
    return tuple(jnp.tile(t, (1, rep)) for t in (rc, rs1, rs2))


def _lane_row(v, off):
    return jnp.zeros((1, LANES), F32).at[0, off:off + v.shape[0]].set(v)


def _layer(x, tables, topk, alpha, w_in, q_norm_w, w_uq, w_iq, k_idx_norm_g, k_idx_norm_b,
           conv_w, conv_b, dt_bias, a_log, d_skip, ssd_norm_w, w_attn_branch, w_ssd_branch,
           w_out, ln1_g, ln1_b, w_up, w_down, ln2_g, ln2_b):
    bsz, seq, _ = x.shape
    m = bsz * seq

    offs = [0]
    for s in IN_SIZES:
        offs.append(offs[-1] + s)
    seg = lambda i: w_in[:, offs[i]:offs[i + 1]]
    pad = jnp.zeros((D_MODEL, LANES - TAIL_END), F32)
    wa = jnp.concatenate([seg(0), seg(1), seg(2), seg(3), seg(4), seg(7), pad], 1).astype(BF16)
    wz, wx, wg = seg(5).astype(BF16), seg(6).astype(BF16), seg(8).astype(BF16)

    qt, qit, k4, vt5, ki, tail, tailt, z, xbc, gates = _proj_call(
        x, wa, wz, wx, wg, q_norm_w.reshape(1, Q_LORA), w_uq.astype(BF16), w_iq.astype(BF16),
        _lane_row(k_idx_norm_g, 0), _lane_row(k_idx_norm_b, 0), _lane_row(dt_bias, DT_OFF), *tables)

    att = _attn_call(ki, qit, tailt, k4, vt5, qt, topk)

    slot = jnp.arange(LANES)[:, None] - DT_OFF
    expand = (slot == jnp.arange(D_INNER)[None, :] // SSD_HEAD_DIM).astype(BF16)
    e2 = jnp.concatenate([expand, expand], axis=0)
    ssd = _ssd_call(xbc, z, tail, conv_w, conv_b.reshape(1, XBC_DIM), _lane_row(a_log, DT_OFF), e2,
                    jnp.repeat(d_skip, SSD_HEAD_DIM).reshape(1, D_INNER),
                    ssd_norm_w.reshape(1, D_INNER))

    r = lambda t: t.reshape(1, -1)
    out = _mix_call(att.reshape(m, ATT_DIM), ssd.reshape(m, D_INNER), gates.reshape(m, 2 * D_MODEL),
                    x.reshape(m, D_MODEL), w_attn_branch.astype(BF16), w_ssd_branch.astype(BF16),
                    w_out.astype(BF16), r(ln1_g), r(ln1_b), w_up.astype(BF16), w_down.astype(BF16),
                    r(ln2_g), r(ln2_b), alpha, tm=MIX_ROWS)
    return out.reshape(bsz, seq, D_MODEL)


def kernel(x, w_in, q_norm_w, w_uq, w_iq, k_idx_norm_g, k_idx_norm_b, conv_w, conv_b, dt_bias,
           a_log, d_skip, ssd_norm_w, w_attn_branch, w_ssd_branch, w_out, ln1_g, ln1_b, w_up,
           w_down, ln2_g, ln2_b):
    depth = w_in.shape[0]
    seq = x.shape[1]
    topk = min(TOPK_MAX, seq // 4)
    alpha = (2 * depth) ** 0.25
    tables = _rope_tables(seq)
    h = x
    for l in range(depth):
        h = _layer(h, tables, topk, alpha, w_in[l], q_norm_w[l], w_uq[l], w_iq[l], k_idx_norm_g[l],
                   k_idx_norm_b[l], conv_w[l], conv_b[l], dt_bias[l], a_log[l], d_skip[l],
                   ssd_norm_w[l], w_attn_branch[l], w_ssd_branch[l], w_out[l], ln1_g[l], ln1_b[l],
                   w_up[l], w_down[l], ln2_g[l], ln2_b[l])
    return h
```

```python
import functools

import jax
import jax.numpy as jnp
from jax import lax
from jax.experimental import pallas as pl
from jax.experimental.pallas import tpu as pltpu

F32 = jnp.float32
BF16 = jnp.bfloat16
I32 = jnp.int32

D_MODEL = 1024
N_ATT_HEADS = 16
ATT_HEAD_DIM = 64
N_KV_HEADS = 4
ATT_DIM = N_ATT_HEADS * ATT_HEAD_DIM
KV_DIM = N_KV_HEADS * ATT_HEAD_DIM
Q_LORA = 256
IDX_HEADS = 8
IDX_DIM = 64
TOPK_MAX = 256
ROT_DIM = ATT_HEAD_DIM // 4
ROPE_THETA = 500000.0
D_INNER = 2 * D_MODEL
SSD_HEAD_DIM = 64
SSD_HEADS = D_INNER // SSD_HEAD_DIM
SSD_GROUPS = 4
D_STATE = 128
CONV_W = 4
CHUNK = 128
BC_DIM = SSD_GROUPS * D_STATE
XBC_DIM = D_INNER + 2 * BC_DIM
D_FF = 4 * D_MODEL
IN_SIZES = (Q_LORA, KV_DIM, KV_DIM, IDX_DIM, IDX_HEADS, D_INNER, XBC_DIM, SSD_HEADS, 2 * D_MODEL)
LN_EPS = 1e-5

LANES = 128
SUBLANES = 8
VMEM_LIMIT_BYTES = 56 * 1024 * 1024

SEQ_TILE = 256
MIX_ROWS = 512
MIX_SUB_ROWS = 256

SMALL_COLS = Q_LORA + 2 * KV_DIM + LANES
TAIL_OFF = Q_LORA + 2 * KV_DIM
WI_OFF = IDX_DIM
DT_OFF = IDX_DIM + IDX_HEADS
TAIL_END = DT_OFF + SSD_HEADS
WZ_OFF = SMALL_COLS
WX_OFF = WZ_OFF + D_INNER
WG_OFF = WX_OFF + XBC_DIM

PERM_STRIDE = CHUNK // SUBLANES
BF16_SUBLANES = 16
V_ROWS = ATT_HEAD_DIM + BF16_SUBLANES
SUM_ROW = ATT_HEAD_DIM
SHIFT_MARGIN = 1.0 + 2.0 ** -6
MIN_SOFTMAX_SUM = 2.0 ** -40
LOG2_E = 1.4426950408889634
INT_MIN = -(2 ** 31)


def _sigmoid(v):
    return 0.5 + 0.5 * jnp.tanh(0.5 * v)


def _silu(v):
    h = 0.5 * v
    return h + h * jnp.tanh(h)


def _const_spec(shape):
    zeros = (0,) * len(shape)
    return pl.BlockSpec(shape, lambda *_: zeros, pipeline_mode=pl.Buffered(1))


def _proj_kernel(x_ref, w_ref, perm_ref, qnw_ref, wuq_ref, wiq_ref,
                 kng_ref, knb_ref, dtb_ref, rc_ref, rs1_ref, rs2_ref,
                 qt_ref, qit_ref, k4_ref, vt_ref, ki_ref, tail_ref, tailt_ref,
                 z_ref, xbc_ref, g_ref):
    xb = x_ref[0].astype(BF16)
    rc, rs1, rs2 = rc_ref[...], rs1_ref[...], rs2_ref[...]
    hd = ATT_HEAD_DIM
    w_cols = lambda off, n: w_ref[:, off:off + n]

    xp = jnp.dot(perm_ref[...], xb, preferred_element_type=F32).astype(BF16)

    def rope(t):
        half = ROT_DIM // 2
        return (t * rc + pltpu.roll(t, LANES - half, 1) * rs1 + pltpu.roll(t, half, 1) * rs2)

    pa = jnp.dot(xb, w_cols(0, SMALL_COLS), preferred_element_type=F32)

    lane = lax.broadcasted_iota(I32, (xb.shape[0], LANES), 1)
    dtr_p = jnp.dot(xp, w_cols(TAIL_OFF, LANES), preferred_element_type=F32) + dtb_ref[...]
    dt_p = jnp.maximum(dtr_p, 0.0) + jnp.log1p(jnp.exp(-jnp.abs(dtr_p)))
    tail_ref[0] = jnp.where((lane >= DT_OFF) & (lane < TAIL_END), dt_p, 0.0)
    z_ref[0] = jnp.dot(xp, w_cols(WZ_OFF, D_INNER), preferred_element_type=F32).astype(BF16)
    xbc_ref[0] = jnp.dot(xp, w_cols(WX_OFF, XBC_DIM), preferred_element_type=F32)

    c_q = pa[:, :Q_LORA]
    ms = jnp.mean(c_q * c_q, axis=-1, keepdims=True)
    cq = (c_q * lax.rsqrt(ms + LN_EPS) * qnw_ref[...]).astype(BF16)

    scale = ATT_HEAD_DIM ** -0.5 * LOG2_E
    qf = jnp.dot(cq, wuq_ref[...], preferred_element_type=F32)
    for j in range(ATT_DIM // LANES):
        sl = slice(j * LANES, (j + 1) * LANES)
        qt_ref[0, sl, :] = (rope(qf[:, sl]) * scale).T.astype(BF16)
    qif = jnp.dot(cq, wiq_ref[...], preferred_element_type=F32)
    iscale = IDX_DIM ** -0.5
    for j in range(IDX_HEADS * IDX_DIM // LANES):
        sl = slice(j * LANES, (j + 1) * LANES)
        qit_ref[0, sl, :] = (rope(qif[:, sl]) * iscale).T.astype(BF16)
    for j in range(KV_DIM // LANES):
        kr = rope(pa[:, Q_LORA + j * LANES:Q_LORA + (j + 1) * LANES])
        for i, kh in enumerate((kr, pltpu.roll(kr, hd, 1))):
            k4_ref[0, 2 * j + i] = jnp.where(lane < hd, kh, jnp.where(lane == hd, 1.0, 0.0)).astype(BF16)
        vtr = pa[:, Q_LORA + KV_DIM + j * LANES:Q_LORA + KV_DIM + (j + 1) * LANES].T
        sum_rows = jnp.where(lax.broadcasted_iota(I32, (V_ROWS - hd, vtr.shape[1]), 0) == 0,
                             1.0, 0.0).astype(BF16)
        vt_ref[0, 2 * j, 0] = jnp.concatenate([vtr[:hd, :].astype(BF16), sum_rows], axis=0)
        vt_ref[0, 2 * j + 1, 0] = jnp.concatenate([vtr[hd:, :].astype(BF16), sum_rows], axis=0)

    tail = pa[:, TAIL_OFF:TAIL_OFF + LANES]
    is_ki = lane < IDX_DIM
    mu = jnp.sum(jnp.where(is_ki, tail, 0.0), axis=-1, keepdims=True) * (1.0 / IDX_DIM)
    dv = jnp.where(is_ki, tail - mu, 0.0)
    var = jnp.sum(dv * dv, axis=-1, keepdims=True) * (1.0 / IDX_DIM)
    ki = rope(dv * lax.rsqrt(var + LN_EPS) * kng_ref[...] + knb_ref[...])
    wi = tail * (IDX_HEADS ** -0.5)
    ki_ref[0] = ki[:, :IDX_DIM].astype(BF16)
    tailt_ref[0] = jnp.where(is_ki, ki, jnp.where(lane < DT_OFF, wi, 0.0)).T

    g_ref[0] = jnp.dot(xb, w_cols(WG_OFF, 2 * D_MODEL), preferred_element_type=F32).astype(BF16)


def _proj_call(x, w_packed, qnw, wuq, wiq, kng, knb, dtb, rc, rs1, rs2):
    bsz, seq, _ = x.shape
    tm = SEQ_TILE
    nsb = seq // tm
    r = jnp.arange(tm)
    src = (r // CHUNK) * CHUNK + PERM_STRIDE * (r % SUBLANES) + (r % CHUNK) // SUBLANES
    perm = (jnp.arange(tm)[None, :] == src[:, None]).astype(BF16)
    row = lambda n: pl.BlockSpec((1, tm, n), lambda b, i: (b, i, 0))
    col = lambda n: pl.BlockSpec((1, n, tm), lambda b, i: (b, 0, i))
    tab = pl.BlockSpec((tm, LANES), lambda b, i: (i, 0))
    sds = jax.ShapeDtypeStruct
    out_shapes = (
        sds((bsz, ATT_DIM, seq), BF16),
        sds((bsz, IDX_HEADS * IDX_DIM, seq), BF16),
        sds((bsz, N_KV_HEADS, seq, LANES), BF16),
        sds((bsz, N_KV_HEADS, nsb, V_ROWS, tm), BF16),
        sds((bsz, seq, IDX_DIM), BF16),
        sds((bsz, seq, LANES), F32),
        sds((bsz, LANES, seq), F32),
        sds((bsz, seq, D_INNER), BF16),
        sds((bsz, seq, XBC_DIM), F32),
        sds((bsz, seq, 2 * D_MODEL), BF16),
    )
    out_specs = [
        col(ATT_DIM), col(IDX_HEADS * IDX_DIM),
        pl.BlockSpec((1, N_KV_HEADS, tm, LANES), lambda b, i: (b, 0, i, 0)),
        pl.BlockSpec((1, N_KV_HEADS, 1, V_ROWS, tm), lambda b, i: (b, 0, i, 0, 0)),
        row(IDX_DIM), row(LANES), col(LANES), row(D_INNER), row(XBC_DIM), row(2 * D_MODEL),
    ]
    consts = (w_packed, perm, qnw, wuq, wiq, kng, knb, dtb)
    return pl.pallas_call(
        _proj_kernel,
        grid=(bsz, nsb),
        in_specs=[row(D_MODEL)] + [_const_spec(c.shape) for c in consts] + [tab, tab, tab],
        out_specs=out_specs,
        out_shape=out_shapes,
        compiler_params=pltpu.CompilerParams(dimension_semantics=("parallel", "parallel"),
                                             vmem_limit_bytes=VMEM_LIMIT_BYTES),
        name="proj",
    )(x, *consts, rc, rs1, rs2)


COUNT_ROWS = 32


def _fold_rows(v, op, rows=32):
    while v.shape[0] > rows:
        half = v.shape[0] // 2
        v = op(v[:half], v[half:])
    return v


def _attn_kernel(ki_ref, qit_ref, wit_ref, k_ref, vt_ref, qt_ref, o_ref,
                 score_ref, coarse_ref, mask_ref, ot_ref, m_ref, acc_ref, qa_ref, kmax_ref,
                 *, topk):
    tq = qt_ref.shape[2]
    tk = tq
    qblk = pl.program_id(1)
    nch = qblk + 1
    sub = lax.broadcasted_iota(I32, (tk, tq), 0)
    tpos = qblk * tq + lax.broadcasted_iota(I32, (tk, tq), 1)

    def score_chunk(c, carry):
        off = pl.multiple_of(c * tk, tk)
        kic = ki_ref[0, pl.ds(off, tk), :]
        acc = jnp.zeros((tk, tq), F32)
        for hh in range(IDX_HEADS):
            sc = jnp.dot(kic, qit_ref[0, hh * IDX_DIM:(hh + 1) * IDX_DIM, :],
                         preferred_element_type=F32)
            acc = acc + wit_ref[0, WI_OFF + hh:WI_OFF + hh + 1, :] * jnp.maximum(sc, 0.0)
        masked = jnp.where(off + sub <= tpos, acc, -jnp.inf)
        score_ref[c] = masked
        coarse_ref[c] = masked.astype(BF16)
        return carry

    lax.fori_loop(0, nch, score_chunk, 0)

    def count(pred, ref=score_ref, dtype=I32):
        def body(c, part):
            hit = jnp.where(pred(ref[c], c), jnp.ones((), dtype), jnp.zeros((), dtype))
            return part + _fold_rows(hit, jnp.add, COUNT_ROWS)

        part = lax.fori_loop(0, nch, body, jnp.zeros((COUNT_ROWS, tq), dtype))
        return jnp.sum(part.astype(I32), axis=0, keepdims=True)

    def ordered_float(u):
        key = u ^ jnp.int32(INT_MIN)
        return pltpu.bitcast(jnp.where(key < 0, key ^ jnp.int32(0x7FFFFFFF), key), F32)

    half_bits = 16
    neg_fill = jnp.int32(2 ** half_bits - 1)

    def coarse_key(u16):
        return lax.shift_left(u16, half_bits) | jnp.where(u16 >= 2 ** (half_bits - 1), 0, neg_fill)

    def coarse_search(i, u16):
        cand = u16 | lax.shift_left(jnp.int32(1), half_bits - 1 - i)
        cand_b = ordered_float(coarse_key(cand)).astype(BF16)
        n = count(lambda sc, c: sc >= cand_b, coarse_ref, BF16)
        return jnp.where(n >= topk, cand, u16)

    u16 = lax.fori_loop(0, half_bits, coarse_search, jnp.zeros((1, tq), I32))
    base_u = coarse_key(u16) - jnp.int32(2 ** half_bits)

    def fine_search(i, d):
        cand = d | lax.shift_left(jnp.int32(1), half_bits - i)
        cand_f = ordered_float(base_u + cand)
        return jnp.where(count(lambda sc, c: sc >= cand_f) >= topk, cand, d)

    thr = ordered_float(base_u + lax.fori_loop(0, half_bits + 1, fine_search,
                                               jnp.zeros((1, tq), I32)))
    thr = jnp.where(tpos[0:1, :] < topk, -jnp.inf, thr)

    need = (topk - count(lambda sc, c: sc > thr)).astype(F32)
    tri = jnp.where(sub >= lax.broadcasted_iota(I32, (tk, tk), 1), 1.0, 0.0).astype(BF16)

    def mask_chunk(c, ties_before):
        sc = score_ref[c]
        tie = jnp.where(sc == thr, 1.0, 0.0)
        tie_rank = jnp.dot(tri, tie.astype(BF16), preferred_element_type=F32) + ties_before
        sel = jnp.where(sc > thr, 1.0, jnp.where(tie_rank <= need, tie, 0.0))
        mask_ref[c] = jnp.where(c * tk + sub <= tpos, sel, 0.0).astype(BF16)
        return tie_rank[tk - 1:tk, :]

    grp = N_ATT_HEADS // N_KV_HEADS
    hd = ATT_HEAD_DIM
    klane = lax.broadcasted_iota(I32, (k_ref.shape[2], LANES), 1)

    @pl.when(qblk == 0)
    def _():
        for g in range(N_KV_HEADS):
            kf = k_ref[0, g].astype(F32)
            norm2 = jnp.sum(jnp.where(klane < hd, kf * kf, 0.0), axis=1, keepdims=True)
            kmax_ref[g:g + 1, :] = jnp.broadcast_to(jnp.max(norm2, axis=0, keepdims=True), (1, tq))

    shift_row = lax.broadcasted_iota(I32, (LANES - hd, tq), 0) == 0
    for h in range(N_ATT_HEADS):
        qs = qt_ref[0, h * hd:(h + 1) * hd, :]
        qf = qs.astype(F32)
        r = jnp.sqrt(jnp.sum(qf * qf, axis=0, keepdims=True) * kmax_ref[h // grp:h // grp + 1, :])
        r = r * SHIFT_MARGIN
        qa_ref[h] = jnp.concatenate([qs, jnp.where(shift_row, -r, 0.0).astype(BF16)], axis=0)

    def scores_of(c, g):
        kc = k_ref[0, g, pl.ds(pl.multiple_of(c * tk, tk), tk), :]
        return [jnp.dot(kc, qa_ref[g * grp + i], preferred_element_type=F32) for i in range(grp)]

    def probs_of(c, g, scores):
        mk = mask_ref[c]
        return ([jnp.exp2(s).astype(BF16) * mk for s in scores],)

    def pv_of(c, g, probs):
        vc = vt_ref[0, g, c]
        for i in range(grp):
            h = g * grp + i
            acc_ref[h] = acc_ref[h] + jnp.dot(vc, probs[i], preferred_element_type=F32)

    def online_probs_of(c, g, scores):
        bias = jnp.where(mask_ref[c].astype(F32) > 0.0, 0.0, -jnp.inf)
        probs, alphas = [], []
        for i in range(grp):
            h = g * grp + i
            m = m_ref[h:h + 1, :]
            s = scores[i] + bias
            m_new = jnp.maximum(m, jnp.max(_fold_rows(s, jnp.maximum), axis=0, keepdims=True))
            m_safe = jnp.where(m_new == -jnp.inf, 0.0, m_new)
            alphas.append(jnp.exp2(m - m_safe))
            probs.append(jnp.exp2(s - m_safe).astype(BF16))
            m_ref[h:h + 1, :] = m_new
        return probs, alphas

    def online_pv_of(c, g, probs, alphas):
        vc = vt_ref[0, g, c]
        for i in range(grp):
            h = g * grp + i
            acc_ref[h] = alphas[i] * acc_ref[h] + jnp.dot(vc, probs[i], preferred_element_type=F32)

    def attend(chunks, probs_fn, pv_fn):
        units = [(c, g) for c in chunks for g in range(N_KV_HEADS)]
        scores = scores_of(*units[0])
        pending = None
        for n, unit in enumerate(units):
            nxt = scores_of(*units[n + 1]) if n + 1 < len(units) else None
            cur = probs_fn(*unit, scores)
            if pending is not None:
                pv_fn(*units[n - 1], *pending)
            scores, pending = nxt, cur
        pv_fn(*units[-1], *pending)

    acc_ref[...] = jnp.zeros(acc_ref.shape, F32)

    def chunk_pair(n, ties):
        ties = mask_chunk(2 * n + 1, mask_chunk(2 * n, ties))
        attend([2 * n, 2 * n + 1], probs_of, pv_of)
        return ties

    ties = lax.fori_loop(0, nch // 2, chunk_pair, jnp.zeros((1, tq), F32))

    @pl.when(nch % 2 == 1)
    def _():
        mask_chunk(nch - 1, ties)
        attend([nch - 1], probs_of, pv_of)

    sums = jnp.concatenate([acc_ref[h, SUM_ROW:SUM_ROW + 1, :] for h in range(N_ATT_HEADS)], axis=0)
    underflow = jnp.where(sums >= MIN_SOFTMAX_SUM, 0, 1)

    @pl.when(jnp.max(underflow) > 0)
    def _():
        m_ref[...] = jnp.full(m_ref.shape, -jnp.inf, F32)
        acc_ref[...] = jnp.zeros(acc_ref.shape, F32)

        def chunk_online(c, carry):
            attend([c], online_probs_of, online_pv_of)
            return carry

        lax.fori_loop(0, nch, chunk_online, 0)

    for h in range(N_ATT_HEADS):
        ot_ref[h * hd:(h + 1) * hd, :] = acc_ref[h, :hd, :] / acc_ref[h, SUM_ROW:SUM_ROW + 1, :]
    o_ref[0] = ot_ref[...].T.astype(BF16)


def _attn_call(ki, qit, tailt, k4, vt5, qt, topk):
    bsz, seq, _ = ki.shape
    tq = vt5.shape[-1]
    nch = seq // tq
    assert tq >= topk and vt5.shape[2] == nch
    return pl.pallas_call(
        functools.partial(_attn_kernel, topk=topk),
        grid=(bsz, nch),
        in_specs=[
            pl.BlockSpec((1, seq, IDX_DIM), lambda b, j: (b, 0, 0)),
            pl.BlockSpec((1, IDX_HEADS * IDX_DIM, tq), lambda b, j: (b, 0, j)),
            pl.BlockSpec((1, LANES, tq), lambda b, j: (b, 0, j)),
            pl.BlockSpec((1, N_KV_HEADS, seq, LANES), lambda b, j: (b, 0, 0, 0)),
            pl.BlockSpec((1, N_KV_HEADS, nch, V_ROWS, tq), lambda b, j: (b, 0, 0, 0, 0)),
            pl.BlockSpec((1, ATT_DIM, tq), lambda b, j: (b, 0, j)),
        ],
        out_specs=pl.BlockSpec((1, tq, ATT_DIM), lambda b, j: (b, j, 0)),
        out_shape=jax.ShapeDtypeStruct((bsz, seq, ATT_DIM), BF16),
        scratch_shapes=[pltpu.VMEM((nch, tq, tq), F32), pltpu.VMEM((nch, tq, tq), BF16),
                        pltpu.VMEM((nch, tq, tq), BF16),
                        pltpu.VMEM((ATT_DIM, tq), F32),
                        pltpu.VMEM((N_ATT_HEADS, tq), F32),
                        pltpu.VMEM((N_ATT_HEADS, V_ROWS, tq), F32),
                        pltpu.VMEM((N_ATT_HEADS, LANES, tq), BF16),
                        pltpu.VMEM((N_KV_HEADS, tq), F32)],
        compiler_params=pltpu.CompilerParams(dimension_semantics=("parallel", "arbitrary"),
                                             vmem_limit_bytes=VMEM_LIMIT_BYTES),
        name="attn",
    )(ki, qit, tailt, k4, vt5, qt)


def _split2(v):
    hi = v.astype(BF16)
    mid = (v - hi.astype(F32)).astype(BF16)
    return jnp.concatenate([hi, mid], axis=1)


def _ssd_kernel(xbc_ref, z_ref, tail_ref, cw_ref, cb_ref, alog_ref, e2_ref,
                dx_ref, nw_ref, out_ref, carry, state, ybuf):
    L = CHUNK
    hpg = SSD_HEADS // SSD_GROUPS
    gw = hpg * SSD_HEAD_DIM

    nshift = CONV_W - 1
    carry_rows = nshift * SUBLANES

    @pl.when(pl.program_id(1) == 0)
    def _():
        carry[...] = jnp.zeros_like(carry)
        state[...] = jnp.zeros_like(state)

    u = xbc_ref[0]
    sub8 = lax.broadcasted_iota(I32, (SUBLANES, XBC_DIM), 0)
    wrapped = []
    for k in range(nshift):
        rows = slice(L - carry_rows + k * SUBLANES, L - carry_rows + (k + 1) * SUBLANES)
        merged = jnp.where(sub8 == SUBLANES - 1, carry[k * SUBLANES:(k + 1) * SUBLANES, :], u[rows, :])
        wrapped.append(pltpu.roll(merged, 1, 0))
    carry[...] = u[L - carry_rows:, :]
    conv = cb_ref[...] + cw_ref[CONV_W - 1:CONV_W, :] * u
    for d in range(1, CONV_W):
        shifted = jnp.concatenate(wrapped[nshift - d:] + [u[:L - d * SUBLANES, :]], axis=0)
        conv = conv + cw_ref[CONV_W - 1 - d:CONV_W - d, :] * shifted
    act = _silu(conv)
    xs = act[:, :D_INNER]

    row = lax.broadcasted_iota(I32, (L, L), 0)
    col = lax.broadcasted_iota(I32, (L, L), 1)
    time_of = lambda r: PERM_STRIDE * (r & (SUBLANES - 1)) + (r >> (SUBLANES.bit_length() - 1))
    lower = time_of(row) >= time_of(col)
    tail = tail_ref[0]
    adt = tail * (-jnp.exp(alog_ref[...]) * LOG2_E)
    hi = adt.astype(BF16)
    r1 = adt - hi.astype(F32)
    mid = r1.astype(BF16)
    lo = (r1 - mid.astype(F32)).astype(BF16)
    ones_l = jnp.where(lower, 1.0, 0.0).astype(BF16)
    a_cum = jnp.dot(jnp.concatenate([ones_l, ones_l, ones_l], axis=1),
                    jnp.concatenate([hi, mid, lo], axis=0), preferred_element_type=F32)
    a_cum_t = a_cum.T
    e2 = e2_ref[...]
    a_cum_x = jnp.dot(_split2(a_cum), e2, preferred_element_type=F32)
    dt_x = jnp.dot(_split2(tail), e2, preferred_element_type=F32)
    a_last_x = a_cum_x[L - 1:L, :]
    xf = xs * dt_x
    xf_b = xf.astype(BF16)
    xd_b = (xf * jnp.exp2(a_last_x - a_cum_x)).astype(BF16)
    ea_x = jnp.exp2(a_cum_x)
    chunk_decay_x = jnp.exp2(a_last_x)

    lane = lax.broadcasted_iota(I32, (L, LANES), 1)
    for g in range(SSD_GROUPS):
        b_f = act[:, D_INNER + g * D_STATE:D_INNER + (g + 1) * D_STATE]
        bg = b_f.astype(BF16)
        bg_t = b_f.T.astype(BF16)
        cg = act[:, D_INNER + BC_DIM + g * D_STATE:D_INNER + BC_DIM + (g + 1) * D_STATE].astype(BF16)
        gs = slice(g * gw, (g + 1) * gw)
        cb = lax.dot_general(cg, bg, (((1,), (1,)), ((), ())), preferred_element_type=F32)
        st_prev = state[:, gs]
        y_off = jnp.dot(cg, st_prev.astype(BF16), preferred_element_type=F32) * ea_x[:, gs]
        st_new = jnp.dot(bg_t, xd_b[:, gs], preferred_element_type=F32)
        state[:, gs] = st_prev * chunk_decay_x[:, gs] + st_new
        for pr in range(hpg // 2):
            h0 = g * hpg + 2 * pr
            xs_pair = xf_b[:, h0 * SSD_HEAD_DIM:(h0 + 2) * SSD_HEAD_DIM]
            ys = []
            for h in (h0, h0 + 1):
                seg = a_cum[:, DT_OFF + h:DT_OFF + h + 1] - a_cum_t[DT_OFF + h:DT_OFF + h + 1, :]
                lmat = jnp.exp2(jnp.where(lower, seg, -jnp.inf))
                ys.append(jnp.dot((cb * lmat).astype(BF16), xs_pair, preferred_element_type=F32))
            ybuf[h0 // 2] = (jnp.where(lane < SSD_HEAD_DIM, ys[0], ys[1])
                             + y_off[:, 2 * pr * SSD_HEAD_DIM:(2 * pr + 2) * SSD_HEAD_DIM])

    ntile = D_INNER // LANES
    y = jnp.concatenate([ybuf[j] for j in range(ntile)], axis=1) + dx_ref[...] * xs
    zf = z_ref[0].astype(F32)
    yz = y * _silu(zf)
    for g in range(SSD_GROUPS):
        gs = slice(g * gw, (g + 1) * gw)
        yg = yz[:, gs]
        msq = jnp.mean(yg * yg, axis=-1, keepdims=True)
        og = yg * lax.rsqrt(msq + LN_EPS) * nw_ref[:, gs]
        for j in range(gw // LANES):
            ybuf[g * (gw // LANES) + j] = og[:, j * LANES:(j + 1) * LANES]
    half = L // 2
    out_ref[0] = jnp.concatenate(
        [jnp.concatenate([ybuf[j, pl.ds(half * (t % 2) + t // 2, SUBLANES, stride=SUBLANES), :]
                          for j in range(ntile)], axis=1)
         for t in range(PERM_STRIDE)], axis=0).astype(BF16)


def _ssd_call(xbc, z, tail, cw, cb, alog, e2, dx, nw):
    bsz, seq, _ = xbc.shape
    L = CHUNK
    blk = lambda n: pl.BlockSpec((1, L, n), lambda b, c: (b, c, 0))
    consts = (cw, cb, alog, e2, dx, nw)
    return pl.pallas_call(
        _ssd_kernel,
        grid=(bsz, seq // L),
        in_specs=[blk(XBC_DIM), blk(D_INNER), blk(LANES)] + [_const_spec(c.shape) for c in consts],
        out_specs=blk(D_INNER),
        out_shape=jax.ShapeDtypeStruct((bsz, seq, D_INNER), BF16),
        scratch_shapes=[pltpu.VMEM(((CONV_W - 1) * SUBLANES, XBC_DIM), F32),
                        pltpu.VMEM((D_STATE, D_INNER), F32),
                        pltpu.VMEM((D_INNER // LANES, L, LANES), F32)],
        compiler_params=pltpu.CompilerParams(dimension_semantics=("parallel", "arbitrary"),
                                             vmem_limit_bytes=VMEM_LIMIT_BYTES),
        name="ssd",
    )(xbc, z, tail, *consts)


def _layer_norm(v, g, b):
    mu = jnp.mean(v, axis=-1, keepdims=True)
    d = v - mu
    var = jnp.mean(d * d, axis=-1, keepdims=True)
    return d * lax.rsqrt(var + LN_EPS) * g + b


def _mix_kernel(att_ref, ssd_ref, g_ref, x_ref, wa_ref, ws_ref, wo_ref, l1g_ref, l1b_ref,
                wu_ref, wd_ref, l2g_ref, l2b_ref, o_ref, *, alpha, ff_chunk):
    tm = x_ref.shape[0]
    subs = [slice(s * MIX_SUB_ROWS, (s + 1) * MIX_SUB_ROWS) for s in range(tm // MIX_SUB_ROWS)]
    mixed = []
    for r in subs:
        ga = g_ref[r, :D_MODEL].astype(F32)
        gs = g_ref[r, D_MODEL:].astype(F32)
        mixed.append(
            _sigmoid(ga) * jnp.dot(att_ref[r, :], wa_ref[...], preferred_element_type=F32)
            + _sigmoid(gs) * jnp.dot(ssd_ref[r, :], ws_ref[...], preferred_element_type=F32))
    hs = []
    for r, mx in zip(subs, mixed):
        proj = jnp.dot(mx.astype(BF16), wo_ref[...], preferred_element_type=F32)
        hs.append(_layer_norm(alpha * x_ref[r, :] + proj, l1g_ref[...], l1b_ref[...]))
    hbs = [h.astype(BF16) for h in hs]
    ffs = [jnp.zeros_like(h) for h in hs]
    for c in range(D_FF // ff_chunk):
        cs = slice(c * ff_chunk, (c + 1) * ff_chunk)
        for s in range(len(subs)):
            u = jnp.maximum(jnp.dot(hbs[s], wu_ref[:, cs], preferred_element_type=F32), 0.0)
            ffs[s] = ffs[s] + jnp.dot((u * u).astype(BF16), wd_ref[cs, :],
                                      preferred_element_type=F32)
    for r, h, ff in zip(subs, hs, ffs):
        o_ref[r, :] = _layer_norm(alpha * h + ff, l2g_ref[...], l2b_ref[...])


def _mix_call(att, ssd, gates, x2, wa, ws, wo, l1g, l1b, wu, wd, l2g, l2b, alpha, tm):
    m = x2.shape[0]
    row = lambda n: pl.BlockSpec((tm, n), lambda i: (i, 0))
    consts = (wa, ws, wo, l1g, l1b, wu, wd, l2g, l2b)
    return pl.pallas_call(
        functools.partial(_mix_kernel, alpha=alpha, ff_chunk=1024),
        grid=(m // tm,),
        in_specs=[row(ATT_DIM), row(D_INNER), row(2 * D_MODEL), row(D_MODEL)]
                 + [_const_spec(c.shape) for c in consts],
        out_specs=row(D_MODEL),
        out_shape=jax.ShapeDtypeStruct((m, D_MODEL), F32),
        compiler_params=pltpu.CompilerParams(dimension_semantics=("parallel",),
                                             vmem_limit_bytes=VMEM_LIMIT_BYTES),
        name="mix",
    )(att, ssd, gates, x2, *consts)


def _rope_tables(seq):
    half = ROT_DIM // 2
    inv = ROPE_THETA ** (-jnp.arange(0, ROT_DIM, 2, dtype=F32) / ROT_DIM)
    ang = jnp.arange(seq, dtype=F32)[:, None] * inv[None, :]
    cos, sin = jnp.cos(ang), jnp.sin(ang)
    pad = jnp.zeros((seq, ATT_HEAD_DIM - ROT_DIM), F32)
    zero = jnp.zeros((seq, half), F32)
    rc = jnp.concatenate([cos, cos, pad + 1.0], -1)
    rs1 = jnp.concatenate([-sin, zero, pad], -1)
    rs2 = jnp.concatenate([zero, sin, pad], -1)
    rep = LANES // ATT_HEAD_DIM
    return tuple(jnp.tile(t, (1, rep)) for t in (rc, rs1, rs2))


def _lane_row(v, off):
    return jnp.zeros((1, LANES), F32).at[0, off:off + v.shape[0]].set(v)


def _layer(x, tables, topk, alpha, w_in, q_norm_w, w_uq, w_iq, k_idx_norm_g, k_idx_norm_b,
           conv_w, conv_b, dt_bias, a_log, d_skip, ssd_norm_w, w_attn_branch, w_ssd_branch,
           w_out, ln1_g, ln1_b, w_up, w_down, ln2_g, ln2_b):
    bsz, seq, _ = x.shape
    m = bsz * seq

    offs = [0]
    for s in IN_SIZES:
        offs.append(offs[-1] + s)
    seg = lambda i: w_in[:, offs[i]:offs[i + 1]]
    pad = jnp.zeros((D_MODEL, LANES - TAIL_END), F32)
    w_packed = jnp.concatenate([seg(0), seg(1), seg(2), seg(3), seg(4), seg(7), pad,
                                seg(5), seg(6), seg(8)], 1).astype(BF16)

    qt, qit, k4, vt5, ki, tail, tailt, z, xbc, gates = _proj_call(
        x, w_packed, q_norm_w.reshape(1, Q_LORA), w_uq.astype(BF16), w_iq.astype(BF16),
        _lane_row(k_idx_norm_g, 0), _lane_row(k_idx_norm_b, 0), _lane_row(dt_bias, DT_OFF), *tables)

    att = _attn_call(ki, qit, tailt, k4, vt5, qt, topk)

    slot = jnp.arange(LANES)[:, None] - DT_OFF
    expand = (slot == jnp.arange(D_INNER)[None, :] // SSD_HEAD_DIM).astype(BF16)
    e2 = jnp.concatenate([expand, expand], axis=0)
    ssd = _ssd_call(xbc, z, tail, conv_w, conv_b.reshape(1, XBC_DIM), _lane_row(a_log, DT_OFF), e2,
                    jnp.repeat(d_skip, SSD_HEAD_DIM).reshape(1, D_INNER),
                    ssd_norm_w.reshape(1, D_INNER))

    r = lambda t: t.reshape(1, -1)
    out = _mix_call(att.reshape(m, ATT_DIM), ssd.reshape(m, D_INNER), gates.reshape(m, 2 * D_MODEL),
                    x.reshape(m, D_MODEL), w_attn_branch.astype(BF16), w_ssd_branch.astype(BF16),
                    w_out.astype(BF16), r(ln1_g), r(ln1_b), w_up.astype(BF16), w_down.astype(BF16),
                    r(ln2_g), r(ln2_b), alpha, tm=MIX_ROWS)
    return out.reshape(bsz, seq, D_MODEL)


def kernel(x, w_in, q_norm_w, w_uq, w_iq, k_idx_norm_g, k_idx_norm_b, conv_w, conv_b, dt_bias,
           a_log, d_skip, ssd_norm_w, w_attn_branch, w_ssd_branch, w_out, ln1_g, ln1_b, w_up,
           w_down, ln2_g, ln2_b):
    depth = w_in.shape[0]
    seq = x.shape[1]
    topk = min(TOPK_MAX, seq // 4)
    alpha = (2 * depth) ** 0.25
    tables = _rope_tables(seq)
    h = x
    for l in range(depth):
        h = _layer(h, tables, topk, alpha, w_in[l], q_norm_w[l], w_uq[l], w_iq[l], k_idx_norm_g[l],
                   k_idx_norm_b[l], conv_w[l], conv_b[l], dt_bias[l], a_log[l], d_skip[l],
                   ssd_norm_w[l], w_attn_branch[l], w_ssd_branch[l], w_out[l], ln1_g[l], ln1_b[l],
                   w_up[l], w_down[l], ln2_g[l], ln2_b[l])
    return h
```

```python
import functools

import jax
import jax.numpy as jnp
from jax import lax
from jax.experimental import pallas as pl
from jax.experimental.pallas import tpu as pltpu

F32 = jnp.float32
BF16 = jnp.bfloat16
I32 = jnp.int32

D_MODEL = 1024
N_ATT_HEADS = 16
ATT_HEAD_DIM = 64
N_KV_HEADS = 4
ATT_DIM = N_ATT_HEADS * ATT_HEAD_DIM
KV_DIM = N_KV_HEADS * ATT_HEAD_DIM
Q_LORA = 256
IDX_HEADS = 8
IDX_DIM = 64
TOPK_MAX = 256
ROT_DIM = ATT_HEAD_DIM // 4
ROPE_THETA = 500000.0
D_INNER = 2 * D_MODEL
SSD_HEAD_DIM = 64
SSD_HEADS = D_INNER // SSD_HEAD_DIM
SSD_GROUPS = 4
D_STATE = 128
CONV_W = 4
CHUNK = 128
BC_DIM = SSD_GROUPS * D_STATE
XBC_DIM = D_INNER + 2 * BC_DIM
D_FF = 4 * D_MODEL
IN_SIZES = (Q_LORA, KV_DIM, KV_DIM, IDX_DIM, IDX_HEADS, D_INNER, XBC_DIM, SSD_HEADS, 2 * D_MODEL)
LN_EPS = 1e-5

LANES = 128
SUBLANES = 8
VMEM_LIMIT_BYTES = 56 * 1024 * 1024

SEQ_TILE = 256
MIX_ROWS = 512
MIX_SUB_ROWS = 256

SMALL_COLS = Q_LORA + 2 * KV_DIM + LANES
TAIL_OFF = Q_LORA + 2 * KV_DIM
WI_OFF = IDX_DIM
DT_OFF = IDX_DIM + IDX_HEADS
TAIL_END = DT_OFF + SSD_HEADS
WZ_OFF = SMALL_COLS
WX_OFF = WZ_OFF + D_INNER
WG_OFF = WX_OFF + XBC_DIM

PERM_STRIDE = CHUNK // SUBLANES
BF16_SUBLANES = 16
V_ROWS = ATT_HEAD_DIM + BF16_SUBLANES
SUM_ROW = ATT_HEAD_DIM
SHIFT_MARGIN = 1.0 + 2.0 ** -6
MIN_SOFTMAX_SUM = 2.0 ** -40
LOG2_E = 1.4426950408889634
INT_MIN = -(2 ** 31)


def _sigmoid(v):
    return 0.5 + 0.5 * jnp.tanh(0.5 * v)


def _silu(v):
    h = 0.5 * v
    return h + h * jnp.tanh(h)


def _const_spec(shape):
    zeros = (0,) * len(shape)
    return pl.BlockSpec(shape, lambda *_: zeros, pipeline_mode=pl.Buffered(1))


PACK_ROWS = 128
PACKED_COLS = SMALL_COLS + D_INNER + XBC_DIM + 2 * D_MODEL


def _pack_kernel(w_ref, o_ref):
    offs = [0]
    for s in IN_SIZES:
        offs.append(offs[-1] + s)
    seg = lambda i: w_ref[:, offs[i]:offs[i + 1]]
    pad = jnp.zeros((w_ref.shape[0], LANES - TAIL_END), F32)
    o_ref[...] = jnp.concatenate([seg(0), seg(1), seg(2), seg(3), seg(4), seg(7), pad,
                                  seg(5), seg(6), seg(8)], axis=1).astype(BF16)


def _pack_call(w_in):
    rows, cols = w_in.shape
    return pl.pallas_call(
        _pack_kernel,
        grid=(rows // PACK_ROWS,),
        in_specs=[pl.BlockSpec((PACK_ROWS, cols), lambda i: (i, 0))],
        out_specs=pl.BlockSpec((PACK_ROWS, PACKED_COLS), lambda i: (i, 0)),
        out_shape=jax.ShapeDtypeStruct((rows, PACKED_COLS), BF16),
        compiler_params=pltpu.CompilerParams(dimension_semantics=("parallel",),
                                             vmem_limit_bytes=VMEM_LIMIT_BYTES),
        name="pack",
    )(w_in)


def _proj_kernel(x_ref, w_ref, perm_ref, qnw_ref, wuq_ref, wiq_ref,
                 kng_ref, knb_ref, dtb_ref, rc_ref, rs1_ref, rs2_ref,
                 qt_ref, qit_ref, k4_ref, vt_ref, ki_ref, tail_ref, tailt_ref,
                 z_ref, xbc_ref, g_ref):
    xb = x_ref[0].astype(BF16)
    rc, rs1, rs2 = rc_ref[...], rs1_ref[...], rs2_ref[...]
    hd = ATT_HEAD_DIM
    w_cols = lambda off, n: w_ref[:, off:off + n]

    xp = jnp.dot(perm_ref[...], xb, preferred_element_type=F32).astype(BF16)

    def rope(t):
        half = ROT_DIM // 2
        return (t * rc + pltpu.roll(t, LANES - half, 1) * rs1 + pltpu.roll(t, half, 1) * rs2)

    pa = jnp.dot(xb, w_cols(0, SMALL_COLS), preferred_element_type=F32)

    lane = lax.broadcasted_iota(I32, (xb.shape[0], LANES), 1)
    dtr_p = jnp.dot(xp, w_cols(TAIL_OFF, LANES), preferred_element_type=F32) + dtb_ref[...]
    dt_p = jnp.maximum(dtr_p, 0.0) + jnp.log1p(jnp.exp(-jnp.abs(dtr_p)))
    tail_ref[0] = jnp.where((lane >= DT_OFF) & (lane < TAIL_END), dt_p, 0.0)
    z_ref[0] = jnp.dot(xp, w_cols(WZ_OFF, D_INNER), preferred_element_type=F32).astype(BF16)
    xbc_ref[0] = jnp.dot(xp, w_cols(WX_OFF, XBC_DIM), preferred_element_type=F32)

    c_q = pa[:, :Q_LORA]
    ms = jnp.mean(c_q * c_q, axis=-1, keepdims=True)
    cq = (c_q * lax.rsqrt(ms + LN_EPS) * qnw_ref[...]).astype(BF16)

    scale = ATT_HEAD_DIM ** -0.5 * LOG2_E
    qf = jnp.dot(cq, wuq_ref[...], preferred_element_type=F32)
    for j in range(ATT_DIM // LANES):
        sl = slice(j * LANES, (j + 1) * LANES)
        qt_ref[0, sl, :] = (rope(qf[:, sl]) * scale).T.astype(BF16)
    qif = jnp.dot(cq, wiq_ref[...], preferred_element_type=F32)
    iscale = IDX_DIM ** -0.5
    for j in range(IDX_HEADS * IDX_DIM // LANES):
        sl = slice(j * LANES, (j + 1) * LANES)
        qit_ref[0, sl, :] = (rope(qif[:, sl]) * iscale).T.astype(BF16)
    for j in range(KV_DIM // LANES):
        kr = rope(pa[:, Q_LORA + j * LANES:Q_LORA + (j + 1) * LANES])
        for i, kh in enumerate((kr, pltpu.roll(kr, hd, 1))):
            k4_ref[0, 2 * j + i] = jnp.where(lane < hd, kh, jnp.where(lane == hd, 1.0, 0.0)).astype(BF16)
        vtr = pa[:, Q_LORA + KV_DIM + j * LANES:Q_LORA + KV_DIM + (j + 1) * LANES].T
        sum_rows = jnp.where(lax.broadcasted_iota(I32, (V_ROWS - hd, vtr.shape[1]), 0) == 0,
                             1.0, 0.0).astype(BF16)
        vt_ref[0, 2 * j, 0] = jnp.concatenate([vtr[:hd, :].astype(BF16), sum_rows], axis=0)
        vt_ref[0, 2 * j + 1, 0] = jnp.concatenate([vtr[hd:, :].astype(BF16), sum_rows], axis=0)

    tail = pa[:, TAIL_OFF:TAIL_OFF + LANES]
    is_ki = lane < IDX_DIM
    mu = jnp.sum(jnp.where(is_ki, tail, 0.0), axis=-1, keepdims=True) * (1.0 / IDX_DIM)
    dv = jnp.where(is_ki, tail - mu, 0.0)
    var = jnp.sum(dv * dv, axis=-1, keepdims=True) * (1.0 / IDX_DIM)
    ki = rope(dv * lax.rsqrt(var + LN_EPS) * kng_ref[...] + knb_ref[...])
    wi = tail * (IDX_HEADS ** -0.5)
    ki_ref[0] = ki[:, :IDX_DIM].astype(BF16)
    tailt_ref[0] = jnp.where(is_ki, ki, jnp.where(lane < DT_OFF, wi, 0.0)).T

    g_ref[0] = jnp.dot(xb, w_cols(WG_OFF, 2 * D_MODEL), preferred_element_type=F32).astype(BF16)


def _proj_call(x, w_packed, qnw, wuq, wiq, kng, knb, dtb, rc, rs1, rs2):
    bsz, seq, _ = x.shape
    tm = SEQ_TILE
    nsb = seq // tm
    r = jnp.arange(tm)
    src = (r // CHUNK) * CHUNK + PERM_STRIDE * (r % SUBLANES) + (r % CHUNK) // SUBLANES
    perm = (jnp.arange(tm)[None, :] == src[:, None]).astype(BF16)
    row = lambda n: pl.BlockSpec((1, tm, n), lambda b, i: (b, i, 0))
    col = lambda n: pl.BlockSpec((1, n, tm), lambda b, i: (b, 0, i))
    tab = pl.BlockSpec((tm, LANES), lambda b, i: (i, 0))
    sds = jax.ShapeDtypeStruct
    out_shapes = (
        sds((bsz, ATT_DIM, seq), BF16),
        sds((bsz, IDX_HEADS * IDX_DIM, seq), BF16),
        sds((bsz, N_KV_HEADS, seq, LANES), BF16),
        sds((bsz, N_KV_HEADS, nsb, V_ROWS, tm), BF16),
        sds((bsz, seq, IDX_DIM), BF16),
        sds((bsz, seq, LANES), F32),
        sds((bsz, LANES, seq), F32),
        sds((bsz, seq, D_INNER), BF16),
        sds((bsz, seq, XBC_DIM), F32),
        sds((bsz, seq, 2 * D_MODEL), BF16),
    )
    out_specs = [
        col(ATT_DIM), col(IDX_HEADS * IDX_DIM),
        pl.BlockSpec((1, N_KV_HEADS, tm, LANES), lambda b, i: (b, 0, i, 0)),
        pl.BlockSpec((1, N_KV_HEADS, 1, V_ROWS, tm), lambda b, i: (b, 0, i, 0, 0)),
        row(IDX_DIM), row(LANES), col(LANES), row(D_INNER), row(XBC_DIM), row(2 * D_MODEL),
    ]
    consts = (w_packed, perm, qnw, wuq, wiq, kng, knb, dtb)
    return pl.pallas_call(
        _proj_kernel,
        grid=(bsz, nsb),
        in_specs=[row(D_MODEL)] + [_const_spec(c.shape) for c in consts] + [tab, tab, tab],
        out_specs=out_specs,
        out_shape=out_shapes,
        compiler_params=pltpu.CompilerParams(dimension_semantics=("parallel", "parallel"),
                                             vmem_limit_bytes=VMEM_LIMIT_BYTES),
        name="proj",
    )(x, *consts, rc, rs1, rs2)


COUNT_ROWS = 32


def _fold_rows(v, op, rows=32):
    while v.shape[0] > rows:
        half = v.shape[0] // 2
        v = op(v[:half], v[half:])
    return v


def _attn_kernel(ki_ref, qit_ref, wit_ref, k_ref, vt_ref, qt_ref, o_ref,
                 score_ref, coarse_ref, mask_ref, ot_ref, m_ref, acc_ref, qa_ref, kmax_ref,
                 *, topk):
    tq = qt_ref.shape[2]
    tk = tq
    qblk = pl.program_id(1)
    nch = qblk + 1
    sub = lax.broadcasted_iota(I32, (tk, tq), 0)
    tpos = qblk * tq + lax.broadcasted_iota(I32, (tk, tq), 1)

    def score_chunk(c, carry):
        off = pl.multiple_of(c * tk, tk)
        kic = ki_ref[0, pl.ds(off, tk), :]
        acc = jnp.zeros((tk, tq), F32)
        for hh in range(IDX_HEADS):
            sc = jnp.dot(kic, qit_ref[0, hh * IDX_DIM:(hh + 1) * IDX_DIM, :],
                         preferred_element_type=F32)
            acc = acc + wit_ref[0, WI_OFF + hh:WI_OFF + hh + 1, :] * jnp.maximum(sc, 0.0)
        masked = jnp.where(off + sub <= tpos, acc, -jnp.inf)
        score_ref[c] = masked
        coarse_ref[c] = masked.astype(BF16)
        return carry

    lax.fori_loop(0, nch, score_chunk, 0)

    def count(pred, ref=score_ref, dtype=I32):
        def body(c, part):
            hit = jnp.where(pred(ref[c], c), jnp.ones((), dtype), jnp.zeros((), dtype))
            return part + _fold_rows(hit, jnp.add, COUNT_ROWS)

        part = lax.fori_loop(0, nch, body, jnp.zeros((COUNT_ROWS, tq), dtype))
        return jnp.sum(part.astype(I32), axis=0, keepdims=True)

    def ordered_float(u):
        key = u ^ jnp.int32(INT_MIN)
        return pltpu.bitcast(jnp.where(key < 0, key ^ jnp.int32(0x7FFFFFFF), key), F32)

    half_bits = 16
    neg_fill = jnp.int32(2 ** half_bits - 1)

    def coarse_key(u16):
        return lax.shift_left(u16, half_bits) | jnp.where(u16 >= 2 ** (half_bits - 1), 0, neg_fill)

    def coarse_search(i, u16):
        cand = u16 | lax.shift_left(jnp.int32(1), half_bits - 1 - i)
        cand_b = ordered_float(coarse_key(cand)).astype(BF16)
        n = count(lambda sc, c: sc >= cand_b, coarse_ref, BF16)
        return jnp.where(n >= topk, cand, u16)

    u16 = lax.fori_loop(0, half_bits, coarse_search, jnp.zeros((1, tq), I32))
    base_u = coarse_key(u16) - jnp.int32(2 ** half_bits)

    def fine_search(i, d):
        cand = d | lax.shift_left(jnp.int32(1), half_bits - i)
        cand_f = ordered_float(base_u + cand)
        return jnp.where(count(lambda sc, c: sc >= cand_f) >= topk, cand, d)

    thr = ordered_float(base_u + lax.fori_loop(0, half_bits + 1, fine_search,
                                               jnp.zeros((1, tq), I32)))
    thr = jnp.where(tpos[0:1, :] < topk, -jnp.inf, thr)

    need = (topk - count(lambda sc, c: sc > thr)).astype(F32)
    tri = jnp.where(sub >= lax.broadcasted_iota(I32, (tk, tk), 1), 1.0, 0.0).astype(BF16)

    def mask_chunk(c, ties_before):
        sc = score_ref[c]
        tie = jnp.where(sc == thr, 1.0, 0.0)
        tie_rank = jnp.dot(tri, tie.astype(BF16), preferred_element_type=F32) + ties_before
        sel = jnp.where(sc > thr, 1.0, jnp.where(tie_rank <= need, tie, 0.0))
        mask_ref[c] = jnp.where(c * tk + sub <= tpos, sel, 0.0).astype(BF16)
        return tie_rank[tk - 1:tk, :]

    grp = N_ATT_HEADS // N_KV_HEADS
    hd = ATT_HEAD_DIM
    klane = lax.broadcasted_iota(I32, (k_ref.shape[2], LANES), 1)

    @pl.when(qblk == 0)
    def _():
        for g in range(N_KV_HEADS):
            kf = k_ref[0, g].astype(F32)
            norm2 = jnp.sum(jnp.where(klane < hd, kf * kf, 0.0), axis=1, keepdims=True)
            kmax_ref[g:g + 1, :] = jnp.broadcast_to(jnp.max(norm2, axis=0, keepdims=True), (1, tq))

    shift_row = lax.broadcasted_iota(I32, (LANES - hd, tq), 0) == 0
    for h in range(N_ATT_HEADS):
        qs = qt_ref[0, h * hd:(h + 1) * hd, :]
        qf = qs.astype(F32)
        r = jnp.sqrt(jnp.sum(qf * qf, axis=0, keepdims=True) * kmax_ref[h // grp:h // grp + 1, :])
        r = r * SHIFT_MARGIN
        qa_ref[h] = jnp.concatenate([qs, jnp.where(shift_row, -r, 0.0).astype(BF16)], axis=0)

    def scores_of(c, g):
        kc = k_ref[0, g, pl.ds(pl.multiple_of(c * tk, tk), tk), :]
        return [jnp.dot(kc, qa_ref[g * grp + i], preferred_element_type=F32) for i in range(grp)]

    def probs_of(c, g, scores):
        mk = mask_ref[c]
        return ([jnp.exp2(s).astype(BF16) * mk for s in scores],)

    def pv_of(c, g, probs):
        vc = vt_ref[0, g, c]
        for i in range(grp):
            h = g * grp + i
            acc_ref[h] = acc_ref[h] + jnp.dot(vc, probs[i], preferred_element_type=F32)

    def online_probs_of(c, g, scores):
        bias = jnp.where(mask_ref[c].astype(F32) > 0.0, 0.0, -jnp.inf)
        probs, alphas = [], []
        for i in range(grp):
            h = g * grp + i
            m = m_ref[h:h + 1, :]
            s = scores[i] + bias
            m_new = jnp.maximum(m, jnp.max(_fold_rows(s, jnp.maximum), axis=0, keepdims=True))
            m_safe = jnp.where(m_new == -jnp.inf, 0.0, m_new)
            alphas.append(jnp.exp2(m - m_safe))
            probs.append(jnp.exp2(s - m_safe).astype(BF16))
            m_ref[h:h + 1, :] = m_new
        return probs, alphas

    def online_pv_of(c, g, probs, alphas):
        vc = vt_ref[0, g, c]
        for i in range(grp):
            h = g * grp + i
            acc_ref[h] = alphas[i] * acc_ref[h] + jnp.dot(vc, probs[i], preferred_element_type=F32)

    def attend(chunks, probs_fn, pv_fn):
        units = [(c, g) for c in chunks for g in range(N_KV_HEADS)]
        scores = scores_of(*units[0])
        pending = None
        for n, unit in enumerate(units):
            nxt = scores_of(*units[n + 1]) if n + 1 < len(units) else None
            cur = probs_fn(*unit, scores)
            if pending is not None:
                pv_fn(*units[n - 1], *pending)
            scores, pending = nxt, cur
        pv_fn(*units[-1], *pending)

    acc_ref[...] = jnp.zeros(acc_ref.shape, F32)

    def chunk_pair(n, ties):
        ties = mask_chunk(2 * n + 1, mask_chunk(2 * n, ties))
        attend([2 * n, 2 * n + 1], probs_of, pv_of)
        return ties

    ties = lax.fori_loop(0, nch // 2, chunk_pair, jnp.zeros((1, tq), F32))

    @pl.when(nch % 2 == 1)
    def _():
        mask_chunk(nch - 1, ties)
        attend([nch - 1], probs_of, pv_of)

    sums = jnp.concatenate([acc_ref[h, SUM_ROW:SUM_ROW + 1, :] for h in range(N_ATT_HEADS)], axis=0)
    underflow = jnp.where(sums >= MIN_SOFTMAX_SUM, 0, 1)

    @pl.when(jnp.max(underflow) > 0)
    def _():
        m_ref[...] = jnp.full(m_ref.shape, -jnp.inf, F32)
        acc_ref[...] = jnp.zeros(acc_ref.shape, F32)

        def chunk_online(c, carry):
            attend([c], online_probs_of, online_pv_of)
            return carry

        lax.fori_loop(0, nch, chunk_online, 0)

    for h in range(N_ATT_HEADS):
        ot_ref[h * hd:(h + 1) * hd, :] = acc_ref[h, :hd, :] / acc_ref[h, SUM_ROW:SUM_ROW + 1, :]
    o_ref[0] = ot_ref[...].T.astype(BF16)


def _attn_call(ki, qit, tailt, k4, vt5, qt, topk):
    bsz, seq, _ = ki.shape
    tq = vt5.shape[-1]
    nch = seq // tq
    assert tq >= topk and vt5.shape[2] == nch
    return pl.pallas_call(
        functools.partial(_attn_kernel, topk=topk),
        grid=(bsz, nch),
        in_specs=[
            pl.BlockSpec((1, seq, IDX_DIM), lambda b, j: (b, 0, 0)),
            pl.BlockSpec((1, IDX_HEADS * IDX_DIM, tq), lambda b, j: (b, 0, j)),
            pl.BlockSpec((1, LANES, tq), lambda b, j: (b, 0, j)),
            pl.BlockSpec((1, N_KV_HEADS, seq, LANES), lambda b, j: (b, 0, 0, 0)),
            pl.BlockSpec((1, N_KV_HEADS, nch, V_ROWS, tq), lambda b, j: (b, 0, 0, 0, 0)),
            pl.BlockSpec((1, ATT_DIM, tq), lambda b, j: (b, 0, j)),
        ],
        out_specs=pl.BlockSpec((1, tq, ATT_DIM), lambda b, j: (b, j, 0)),
        out_shape=jax.ShapeDtypeStruct((bsz, seq, ATT_DIM), BF16),
        scratch_shapes=[pltpu.VMEM((nch, tq, tq), F32), pltpu.VMEM((nch, tq, tq), BF16),
                        pltpu.VMEM((nch, tq, tq), BF16),
                        pltpu.VMEM((ATT_DIM, tq), F32),
                        pltpu.VMEM((N_ATT_HEADS, tq), F32),
                        pltpu.VMEM((N_ATT_HEADS, V_ROWS, tq), F32),
                        pltpu.VMEM((N_ATT_HEADS, LANES, tq), BF16),
                        pltpu.VMEM((N_KV_HEADS, tq), F32)],
        compiler_params=pltpu.CompilerParams(dimension_semantics=("parallel", "arbitrary"),
                                             vmem_limit_bytes=VMEM_LIMIT_BYTES),
        name="attn",
    )(ki, qit, tailt, k4, vt5, qt)


def _split2(v):
    hi = v.astype(BF16)
    mid = (v - hi.astype(F32)).astype(BF16)
    return jnp.concatenate([hi, mid], axis=1)


def _ssd_kernel(xbc_ref, z_ref, tail_ref, cw_ref, cb_ref, alog_ref, e2_ref,
                dx_ref, nw_ref, out_ref, carry, state, ybuf):
    L = CHUNK
    hpg = SSD_HEADS // SSD_GROUPS
    gw = hpg * SSD_HEAD_DIM

    nshift = CONV_W - 1
    carry_rows = nshift * SUBLANES

    @pl.when(pl.program_id(1) == 0)
    def _():
        carry[...] = jnp.zeros_like(carry)
        state[...] = jnp.zeros_like(state)

    u = xbc_ref[0]
    sub8 = lax.broadcasted_iota(I32, (SUBLANES, XBC_DIM), 0)
    wrapped = []
    for k in range(nshift):
        rows = slice(L - carry_rows + k * SUBLANES, L - carry_rows + (k + 1) * SUBLANES)
        merged = jnp.where(sub8 == SUBLANES - 1, carry[k * SUBLANES:(k + 1) * SUBLANES, :], u[rows, :])
        wrapped.append(pltpu.roll(merged, 1, 0))
    carry[...] = u[L - carry_rows:, :]
    conv = cb_ref[...] + cw_ref[CONV_W - 1:CONV_W, :] * u
    for d in range(1, CONV_W):
        shifted = jnp.concatenate(wrapped[nshift - d:] + [u[:L - d * SUBLANES, :]], axis=0)
        conv = conv + cw_ref[CONV_W - 1 - d:CONV_W - d, :] * shifted
    act = _silu(conv)
    xs = act[:, :D_INNER]

    row = lax.broadcasted_iota(I32, (L, L), 0)
    col = lax.broadcasted_iota(I32, (L, L), 1)
    time_of = lambda r: PERM_STRIDE * (r & (SUBLANES - 1)) + (r >> (SUBLANES.bit_length() - 1))
    lower = time_of(row) >= time_of(col)
    tail = tail_ref[0]
    adt = tail * (-jnp.exp(alog_ref[...]) * LOG2_E)
    hi = adt.astype(BF16)
    r1 = adt - hi.astype(F32)
    mid = r1.astype(BF16)
    lo = (r1 - mid.astype(F32)).astype(BF16)
    ones_l = jnp.where(lower, 1.0, 0.0).astype(BF16)
    a_cum = jnp.dot(jnp.concatenate([ones_l, ones_l, ones_l], axis=1),
                    jnp.concatenate([hi, mid, lo], axis=0), preferred_element_type=F32)
    a_cum_t = a_cum.T
    e2 = e2_ref[...]
    a_cum_x = jnp.dot(_split2(a_cum), e2, preferred_element_type=F32)
    dt_x = jnp.dot(_split2(tail), e2, preferred_element_type=F32)
    a_last_x = a_cum_x[L - 1:L, :]
    xf = xs * dt_x
    xf_b = xf.astype(BF16)
    xd_b = (xf * jnp.exp2(a_last_x - a_cum_x)).astype(BF16)
    ea_x = jnp.exp2(a_cum_x)
    chunk_decay_x = jnp.exp2(a_last_x)

    lane = lax.broadcasted_iota(I32, (L, LANES), 1)
    for g in range(SSD_GROUPS):
        b_f = act[:, D_INNER + g * D_STATE:D_INNER + (g + 1) * D_STATE]
        bg = b_f.astype(BF16)
        bg_t = b_f.T.astype(BF16)
        cg = act[:, D_INNER + BC_DIM + g * D_STATE:D_INNER + BC_DIM + (g + 1) * D_STATE].astype(BF16)
        gs = slice(g * gw, (g + 1) * gw)
        cb = lax.dot_general(cg, bg, (((1,), (1,)), ((), ())), preferred_element_type=F32)
        st_prev = state[:, gs]
        y_off = jnp.dot(cg, st_prev.astype(BF16), preferred_element_type=F32) * ea_x[:, gs]
        st_new = jnp.dot(bg_t, xd_b[:, gs], preferred_element_type=F32)
        state[:, gs] = st_prev * chunk_decay_x[:, gs] + st_new
        for pr in range(hpg // 2):
            h0 = g * hpg + 2 * pr
            xs_pair = xf_b[:, h0 * SSD_HEAD_DIM:(h0 + 2) * SSD_HEAD_DIM]
            ys = []
            for h in (h0, h0 + 1):
                seg = a_cum[:, DT_OFF + h:DT_OFF + h + 1] - a_cum_t[DT_OFF + h:DT_OFF + h + 1, :]
                lmat = jnp.exp2(jnp.where(lower, seg, -jnp.inf))
                ys.append(jnp.dot((cb * lmat).astype(BF16), xs_pair, preferred_element_type=F32))
            ybuf[h0 // 2] = (jnp.where(lane < SSD_HEAD_DIM, ys[0], ys[1])
                             + y_off[:, 2 * pr * SSD_HEAD_DIM:(2 * pr + 2) * SSD_HEAD_DIM])

    ntile = D_INNER // LANES
    y = jnp.concatenate([ybuf[j] for j in range(ntile)], axis=1) + dx_ref[...] * xs
    zf = z_ref[0].astype(F32)
    yz = y * _silu(zf)
    for g in range(SSD_GROUPS):
        gs = slice(g * gw, (g + 1) * gw)
        yg = yz[:, gs]
        msq = jnp.mean(yg * yg, axis=-1, keepdims=True)
        og = yg * lax.rsqrt(msq + LN_EPS) * nw_ref[:, gs]
        for j in range(gw // LANES):
            ybuf[g * (gw // LANES) + j] = og[:, j * LANES:(j + 1) * LANES]
    half = L // 2
    out_ref[0] = jnp.concatenate(
        [jnp.concatenate([ybuf[j, pl.ds(half * (t % 2) + t // 2, SUBLANES, stride=SUBLANES), :]
                          for j in range(ntile)], axis=1)
         for t in range(PERM_STRIDE)], axis=0).astype(BF16)


def _ssd_call(xbc, z, tail, cw, cb, alog, e2, dx, nw):
    bsz, seq, _ = xbc.shape
    L = CHUNK
    blk = lambda n: pl.BlockSpec((1, L, n), lambda b, c: (b, c, 0))
    consts = (cw, cb, alog, e2, dx, nw)
    return pl.pallas_call(
        _ssd_kernel,
        grid=(bsz, seq // L),
        in_specs=[blk(XBC_DIM), blk(D_INNER), blk(LANES)] + [_const_spec(c.shape) for c in consts],
        out_specs=blk(D_INNER),
        out_shape=jax.ShapeDtypeStruct((bsz, seq, D_INNER), BF16),
        scratch_shapes=[pltpu.VMEM(((CONV_W - 1) * SUBLANES, XBC_DIM), F32),
                        pltpu.VMEM((D_STATE, D_INNER), F32),
                        pltpu.VMEM((D_INNER // LANES, L, LANES), F32)],
        compiler_params=pltpu.CompilerParams(dimension_semantics=("parallel", "arbitrary"),
                                             vmem_limit_bytes=VMEM_LIMIT_BYTES),
        name="ssd",
    )(xbc, z, tail, *consts)


def _layer_norm(v, g, b):
    mu = jnp.mean(v, axis=-1, keepdims=True)
    d = v - mu
    var = jnp.mean(d * d, axis=-1, keepdims=True)
    return d * lax.rsqrt(var + LN_EPS) * g + b


def _mix_kernel(att_ref, ssd_ref, g_ref, x_ref, wa_ref, ws_ref, wo_ref, l1g_ref, l1b_ref,
                wu_ref, wd_ref, l2g_ref, l2b_ref, o_ref, *, alpha, ff_chunk):
    tm = x_ref.shape[0]
    subs = [slice(s * MIX_SUB_ROWS, (s + 1) * MIX_SUB_ROWS) for s in range(tm // MIX_SUB_ROWS)]
    mixed = []
    for r in subs:
        ga = g_ref[r, :D_MODEL].astype(F32)
        gs = g_ref[r, D_MODEL:].astype(F32)
        mixed.append(
            _sigmoid(ga) * jnp.dot(att_ref[r, :], wa_ref[...], preferred_element_type=F32)
            + _sigmoid(gs) * jnp.dot(ssd_ref[r, :], ws_ref[...], preferred_element_type=F32))
    hs = []
    for r, mx in zip(subs, mixed):
        proj = jnp.dot(mx.astype(BF16), wo_ref[...], preferred_element_type=F32)
        hs.append(_layer_norm(alpha * x_ref[r, :] + proj, l1g_ref[...], l1b_ref[...]))
    hbs = [h.astype(BF16) for h in hs]
    ffs = [jnp.zeros_like(h) for h in hs]
    for c in range(D_FF // ff_chunk):
        cs = slice(c * ff_chunk, (c + 1) * ff_chunk)
        for s in range(len(subs)):
            u = jnp.maximum(jnp.dot(hbs[s], wu_ref[:, cs], preferred_element_type=F32), 0.0)
            ffs[s] = ffs[s] + jnp.dot((u * u).astype(BF16), wd_ref[cs, :],
                                      preferred_element_type=F32)
    for r, h, ff in zip(subs, hs, ffs):
        o_ref[r, :] = _layer_norm(alpha * h + ff, l2g_ref[...], l2b_ref[...])


def _mix_call(att, ssd, gates, x2, wa, ws, wo, l1g, l1b, wu, wd, l2g, l2b, alpha, tm):
    m = x2.shape[0]
    row = lambda n: pl.BlockSpec((tm, n), lambda i: (i, 0))
    consts = (wa, ws, wo, l1g, l1b, wu, wd, l2g, l2b)
    return pl.pallas_call(
        functools.partial(_mix_kernel, alpha=alpha, ff_chunk=1024),
        grid=(m // tm,),
        in_specs=[row(ATT_DIM), row(D_INNER), row(2 * D_MODEL), row(D_MODEL)]
                 + [_const_spec(c.shape) for c in consts],
        out_specs=row(D_MODEL),
        out_shape=jax.ShapeDtypeStruct((m, D_MODEL), F32),
        compiler_params=pltpu.CompilerParams(dimension_semantics=("parallel",),
                                             vmem_limit_bytes=VMEM_LIMIT_BYTES),
        name="mix",
    )(att, ssd, gates, x2, *consts)


def _rope_tables(seq):
    half = ROT_DIM // 2
    inv = ROPE_THETA ** (-jnp.arange(0, ROT_DIM, 2, dtype=F32) / ROT_DIM)
    ang = jnp.arange(seq, dtype=F32)[:, None] * inv[None, :]
    cos, sin = jnp.cos(ang), jnp.sin(ang)
    pad = jnp.zeros((seq, ATT_HEAD_DIM - ROT_DIM), F32)
    zero = jnp.zeros((seq, half), F32)
    rc = jnp.concatenate([cos, cos, pad + 1.0], -1)
    rs1 = jnp.concatenate([-sin, zero, pad], -1)
    rs2 = jnp.concatenate([zero, sin, pad], -1)
    rep = LANES // ATT_HEAD_DIM
    return tuple(jnp.tile(t, (1, rep)) for t in (rc, rs1, rs2))


def _lane_row(v, off):
    return jnp.zeros((1, LANES), F32).at[0, off:off + v.shape[0]].set(v)


def _layer(x, tables, topk, alpha, w_in, q_norm_w, w_uq, w_iq, k_idx_norm_g, k_idx_norm_b,
           conv_w, conv_b, dt_bias, a_log, d_skip, ssd_norm_w, w_attn_branch, w_ssd_branch,
           w_out, ln1_g, ln1_b, w_up, w_down, ln2_g, ln2_b):
    bsz, seq, _ = x.shape
    m = bsz * seq

    w_packed = _pack_call(w_in)

    qt, qit, k4, vt5, ki, tail, tailt, z, xbc, gates = _proj_call(
        x, w_packed, q_norm_w.reshape(1, Q_LORA), w_uq.astype(BF16), w_iq.astype(BF16),
        _lane_row(k_idx_norm_g, 0), _lane_row(k_idx_norm_b, 0), _lane_row(dt_bias, DT_OFF), *tables)

    att = _attn_call(ki, qit, tailt, k4, vt5, qt, topk)

    slot = jnp.arange(LANES)[:, None] - DT_OFF
    expand = (slot == jnp.arange(D_INNER)[None, :] // SSD_HEAD_DIM).astype(BF16)
    e2 = jnp.concatenate([expand, expand], axis=0)
    ssd = _ssd_call(xbc, z, tail, conv_w, conv_b.reshape(1, XBC_DIM), _lane_row(a_log, DT_OFF), e2,
                    jnp.repeat(d_skip, SSD_HEAD_DIM).reshape(1, D_INNER),
                    ssd_norm_w.reshape(1, D_INNER))

    r = lambda t: t.reshape(1, -1)
    out = _mix_call(att.reshape(m, ATT_DIM), ssd.reshape(m, D_INNER), gates.reshape(m, 2 * D_MODEL),
                    x.reshape(m, D_MODEL), w_attn_branch.astype(BF16), w_ssd_branch.astype(BF16),
                    w_out.astype(BF16), r(ln1_g), r(ln1_b), w_up.astype(BF16), w_down.astype(BF16),
                    r(ln2_g), r(ln2_b), alpha, tm=MIX_ROWS)
    return out.reshape(bsz, seq, D_MODEL)


def kernel(x, w_in, q_norm_w, w_uq, w_iq, k_idx_norm_g, k_idx_norm_b, conv_w, conv_b, dt_bias,
           a_log, d_skip, ssd_norm_w, w_attn_branch, w_ssd_branch, w_out, ln1_g, ln1_b, w_up,
           w_down, ln2_g, ln2_b):
    depth = w_in.shape[0]
    seq = x.shape[1]
    topk = min(TOPK_MAX, seq // 4)
    alpha = (2 * depth) ** 0.25
    tables = _rope_tables(seq)
    h = x
    for l in range(depth):
        h = _layer(h, tables, topk, alpha, w_in[l], q_norm_w[l], w_uq[l], w_iq[l], k_idx_norm_g[l],
                   k_idx_norm_b[l], conv_w[l], conv_b[l], dt_bias[l], a_log[l], d_skip[l],
                   ssd_norm_w[l], w_attn_branch[l], w_ssd_branch[l], w_out[l], ln1_g[l], ln1_b[l],
                   w_up[l], w_down[l], ln2_g[l], ln2_b[l])
    return h
```

```python
import functools

import jax
import jax.numpy as jnp
from jax import lax
from jax.experimental import pallas as pl
from jax.experimental.pallas import tpu as pltpu

F32 = jnp.float32
BF16 = jnp.bfloat16
I32 = jnp.int32

D_MODEL = 1024
N_ATT_HEADS = 16
ATT_HEAD_DIM = 64
N_KV_HEADS = 4
ATT_DIM = N_ATT_HEADS * ATT_HEAD_DIM
KV_DIM = N_KV_HEADS * ATT_HEAD_DIM
Q_LORA = 256
IDX_HEADS = 8
IDX_DIM = 64
TOPK_MAX = 256
ROT_DIM = ATT_HEAD_DIM // 4
ROPE_THETA = 500000.0
D_INNER = 2 * D_MODEL
SSD_HEAD_DIM = 64
SSD_HEADS = D_INNER // SSD_HEAD_DIM
SSD_GROUPS = 4
D_STATE = 128
CONV_W = 4
CHUNK = 128
BC_DIM = SSD_GROUPS * D_STATE
XBC_DIM = D_INNER + 2 * BC_DIM
D_FF = 4 * D_MODEL
IN_SIZES = (Q_LORA, KV_DIM, KV_DIM, IDX_DIM, IDX_HEADS, D_INNER, XBC_DIM, SSD_HEADS, 2 * D_MODEL)
LN_EPS = 1e-5

LANES = 128
SUBLANES = 8
VMEM_LIMIT_BYTES = 56 * 1024 * 1024

SEQ_TILE = 256
MIX_ROWS = 512
MIX_SUB_ROWS = 256

SMALL_COLS = Q_LORA + 2 * KV_DIM + LANES
TAIL_OFF = Q_LORA + 2 * KV_DIM
WI_OFF = IDX_DIM
WI_END = WI_OFF + IDX_HEADS
DT_OFF = 0
TAIL_END = DT_OFF + SSD_HEADS
WDT_OFF = SMALL_COLS
WZ_OFF = WDT_OFF + LANES
WX_OFF = WZ_OFF + D_INNER
WG_OFF = WX_OFF + XBC_DIM
PACKED_COLS = WG_OFF + 2 * D_MODEL

PERM_STRIDE = CHUNK // SUBLANES
BF16_SUBLANES = 16
V_ROWS = ATT_HEAD_DIM + BF16_SUBLANES
SUM_ROW = ATT_HEAD_DIM
SHIFT_MARGIN = 1.0 + 2.0 ** -6
MIN_SOFTMAX_SUM = 2.0 ** -40
LOG2_E = 1.4426950408889634
INT_MIN = -(2 ** 31)


def _sigmoid(v):
    return 0.5 + 0.5 * jnp.tanh(0.5 * v)


def _silu(v):
    h = 0.5 * v
    return h + h * jnp.tanh(h)


def _const_spec(shape):
    zeros = (0,) * len(shape)
    return pl.BlockSpec(shape, lambda *_: zeros, pipeline_mode=pl.Buffered(1))


PACK_TILES = 4
ROW_GROUPS = LANES // SUBLANES


def _pack_tile_source(t):
    offs = [0]
    for s in IN_SIZES:
        offs.append(offs[-1] + s)
    assert all(o % SUBLANES == 0 for o in offs)
    first = lambda seg_index, tile0: offs[seg_index] // SUBLANES + ROW_GROUPS * (t - tile0)
    return jnp.where(t < WDT_OFF // LANES, ROW_GROUPS * t,
                     jnp.where(t < WZ_OFF // LANES, first(7, WDT_OFF // LANES),
                               jnp.where(t < WX_OFF // LANES, first(5, WZ_OFF // LANES),
                                         jnp.where(t < WG_OFF // LANES, first(6, WX_OFF // LANES),
                                                   first(8, WG_OFF // LANES)))))


def _pack_kernel(*refs):
    o_ref = refs[-1]
    for i, w_ref in enumerate(refs[:-1]):
        rows = w_ref[...].reshape(LANES, w_ref.shape[-1])
        o_ref[:, i * LANES:(i + 1) * LANES] = rows.T.astype(BF16)


def _pack_call(w_t):
    cols, rows = w_t.shape
    w3 = w_t.reshape(cols // SUBLANES, SUBLANES, rows)
    tile_spec = lambda i: pl.BlockSpec(
        (pl.Element(ROW_GROUPS), pl.Element(SUBLANES), pl.Element(rows)),
        lambda j: (_pack_tile_source(PACK_TILES * j + i), 0, 0))
    return pl.pallas_call(
        _pack_kernel,
        grid=(PACKED_COLS // (PACK_TILES * LANES),),
        in_specs=[tile_spec(i) for i in range(PACK_TILES)],
        out_specs=pl.BlockSpec((rows, PACK_TILES * LANES), lambda j: (0, j)),
        out_shape=jax.ShapeDtypeStruct((rows, PACKED_COLS), BF16),
        compiler_params=pltpu.CompilerParams(dimension_semantics=("parallel",),
                                             vmem_limit_bytes=VMEM_LIMIT_BYTES),
        name="pack",
    )(*([w3] * PACK_TILES))


def _proj_kernel(x_ref, w_ref, perm_ref, qnw_ref, wuq_ref, wiq_ref,
                 kng_ref, knb_ref, dtb_ref, rc_ref, rs1_ref, rs2_ref,
                 qt_ref, qit_ref, k4_ref, vt_ref, ki_ref, tail_ref, tailt_ref,
                 z_ref, xbc_ref, g_ref):
    xb = x_ref[0].astype(BF16)
    rc, rs1, rs2 = rc_ref[...], rs1_ref[...], rs2_ref[...]
    hd = ATT_HEAD_DIM
    w_cols = lambda off, n: w_ref[:, off:off + n]

    xp = jnp.dot(perm_ref[...], xb, preferred_element_type=F32).astype(BF16)

    def rope(t):
        half = ROT_DIM // 2
        return (t * rc + pltpu.roll(t, LANES - half, 1) * rs1 + pltpu.roll(t, half, 1) * rs2)

    pa = jnp.dot(xb, w_cols(0, SMALL_COLS), preferred_element_type=F32)

    lane = lax.broadcasted_iota(I32, (xb.shape[0], LANES), 1)
    dtr_p = jnp.dot(xp, w_cols(WDT_OFF, LANES), preferred_element_type=F32) + dtb_ref[...]
    dt_p = jnp.maximum(dtr_p, 0.0) + jnp.log1p(jnp.exp(-jnp.abs(dtr_p)))
    tail_ref[0] = jnp.where((lane >= DT_OFF) & (lane < TAIL_END), dt_p, 0.0)
    z_ref[0] = jnp.dot(xp, w_cols(WZ_OFF, D_INNER), preferred_element_type=F32).astype(BF16)
    xbc_ref[0] = jnp.dot(xp, w_cols(WX_OFF, XBC_DIM), preferred_element_type=F32)

    c_q = pa[:, :Q_LORA]
    ms = jnp.mean(c_q * c_q, axis=-1, keepdims=True)
    cq = (c_q * lax.rsqrt(ms + LN_EPS) * qnw_ref[...]).astype(BF16)

    scale = ATT_HEAD_DIM ** -0.5 * LOG2_E
    qf = jnp.dot(cq, wuq_ref[...], preferred_element_type=F32)
    for j in range(ATT_DIM // LANES):
        sl = slice(j * LANES, (j + 1) * LANES)
        qt_ref[0, sl, :] = (rope(qf[:, sl]) * scale).T.astype(BF16)
    qif = jnp.dot(cq, wiq_ref[...], preferred_element_type=F32)
    iscale = IDX_DIM ** -0.5
    for j in range(IDX_HEADS * IDX_DIM // LANES):
        sl = slice(j * LANES, (j + 1) * LANES)
        qit_ref[0, sl, :] = (rope(qif[:, sl]) * iscale).T.astype(BF16)
    for j in range(KV_DIM // LANES):
        kr = rope(pa[:, Q_LORA + j * LANES:Q_LORA + (j + 1) * LANES])
        for i, kh in enumerate((kr, pltpu.roll(kr, hd, 1))):
            k4_ref[0, 2 * j + i] = jnp.where(lane < hd, kh, jnp.where(lane == hd, 1.0, 0.0)).astype(BF16)
        vtr = pa[:, Q_LORA + KV_DIM + j * LANES:Q_LORA + KV_DIM + (j + 1) * LANES].T
        sum_rows = jnp.where(lax.broadcasted_iota(I32, (V_ROWS - hd, vtr.shape[1]), 0) == 0,
                             1.0, 0.0).astype(BF16)
        vt_ref[0, 2 * j, 0] = jnp.concatenate([vtr[:hd, :].astype(BF16), sum_rows], axis=0)
        vt_ref[0, 2 * j + 1, 0] = jnp.concatenate([vtr[hd:, :].astype(BF16), sum_rows], axis=0)

    tail = pa[:, TAIL_OFF:TAIL_OFF + LANES]
    is_ki = lane < IDX_DIM
    mu = jnp.sum(jnp.where(is_ki, tail, 0.0), axis=-1, keepdims=True) * (1.0 / IDX_DIM)
    dv = jnp.where(is_ki, tail - mu, 0.0)
    var = jnp.sum(dv * dv, axis=-1, keepdims=True) * (1.0 / IDX_DIM)
    ki = rope(dv * lax.rsqrt(var + LN_EPS) * kng_ref[...] + knb_ref[...])
    wi = tail * (IDX_HEADS ** -0.5)
    ki_ref[0] = ki[:, :IDX_DIM].astype(BF16)
    tailt_ref[0] = jnp.where(is_ki, ki, jnp.where(lane < WI_END, wi, 0.0)).T

    g_ref[0] = jnp.dot(xb, w_cols(WG_OFF, 2 * D_MODEL), preferred_element_type=F32).astype(BF16)


def _proj_call(x, w_packed, qnw, wuq, wiq, kng, knb, dtb, rc, rs1, rs2):
    bsz, seq, _ = x.shape
    tm = SEQ_TILE
    nsb = seq // tm
    r = jnp.arange(tm)
    src = (r // CHUNK) * CHUNK + PERM_STRIDE * (r % SUBLANES) + (r % CHUNK) // SUBLANES
    perm = (jnp.arange(tm)[None, :] == src[:, None]).astype(BF16)
    row = lambda n: pl.BlockSpec((1, tm, n), lambda b, i: (b, i, 0))
    col = lambda n: pl.BlockSpec((1, n, tm), lambda b, i: (b, 0, i))
    tab = pl.BlockSpec((tm, LANES), lambda b, i: (i, 0))
    sds = jax.ShapeDtypeStruct
    out_shapes = (
        sds((bsz, ATT_DIM, seq), BF16),
        sds((bsz, IDX_HEADS * IDX_DIM, seq), BF16),
        sds((bsz, N_KV_HEADS, seq, LANES), BF16),
        sds((bsz, N_KV_HEADS, nsb, V_ROWS, tm), BF16),
        sds((bsz, seq, IDX_DIM), BF16),
        sds((bsz, seq, LANES), F32),
        sds((bsz, LANES, seq), F32),
        sds((bsz, seq, D_INNER), BF16),
        sds((bsz, seq, XBC_DIM), F32),
        sds((bsz, seq, 2 * D_MODEL), BF16),
    )
    out_specs = [
        col(ATT_DIM), col(IDX_HEADS * IDX_DIM),
        pl.BlockSpec((1, N_KV_HEADS, tm, LANES), lambda b, i: (b, 0, i, 0)),
        pl.BlockSpec((1, N_KV_HEADS, 1, V_ROWS, tm), lambda b, i: (b, 0, i, 0, 0)),
        row(IDX_DIM), row(LANES), col(LANES), row(D_INNER), row(XBC_DIM), row(2 * D_MODEL),
    ]
    consts = (w_packed, perm, qnw, wuq, wiq, kng, knb, dtb)
    return pl.pallas_call(
        _proj_kernel,
        grid=(bsz, nsb),
        in_specs=[row(D_MODEL)] + [_const_spec(c.shape) for c in consts] + [tab, tab, tab],
        out_specs=out_specs,
        out_shape=out_shapes,
        compiler_params=pltpu.CompilerParams(dimension_semantics=("parallel", "parallel"),
                                             vmem_limit_bytes=VMEM_LIMIT_BYTES),
        name="proj",
    )(x, *consts, rc, rs1, rs2)


COUNT_ROWS = 32


def _fold_rows(v, op, rows=32):
    while v.shape[0] > rows:
        half = v.shape[0] // 2
        v = op(v[:half], v[half:])
    return v


def _attn_kernel(ki_ref, qit_ref, wit_ref, k_ref, vt_ref, qt_ref, o_ref,
                 score_ref, coarse_ref, mask_ref, ot_ref, m_ref, acc_ref, qa_ref, kmax_ref,
                 *, topk):
    tq = qt_ref.shape[2]
    tk = tq
    qblk = pl.program_id(1)
    nch = qblk + 1
    sub = lax.broadcasted_iota(I32, (tk, tq), 0)
    tpos = qblk * tq + lax.broadcasted_iota(I32, (tk, tq), 1)

    def score_chunk(c, carry):
        off = pl.multiple_of(c * tk, tk)
        kic = ki_ref[0, pl.ds(off, tk), :]
        acc = jnp.zeros((tk, tq), F32)
        for hh in range(IDX_HEADS):
            sc = jnp.dot(kic, qit_ref[0, hh * IDX_DIM:(hh + 1) * IDX_DIM, :],
                         preferred_element_type=F32)
            acc = acc + wit_ref[0, WI_OFF + hh:WI_OFF + hh + 1, :] * jnp.maximum(sc, 0.0)
        masked = jnp.where(off + sub <= tpos, acc, -jnp.inf)
        score_ref[c] = masked
        coarse_ref[c] = masked.astype(BF16)
        return carry

    lax.fori_loop(0, nch, score_chunk, 0)

    def count(pred, ref=score_ref, dtype=I32):
        def body(c, part):
            hit = jnp.where(pred(ref[c], c), jnp.ones((), dtype), jnp.zeros((), dtype))
            return part + _fold_rows(hit, jnp.add, COUNT_ROWS)

        part = lax.fori_loop(0, nch, body, jnp.zeros((COUNT_ROWS, tq), dtype))
        return jnp.sum(part.astype(I32), axis=0, keepdims=True)

    def ordered_float(u):
        key = u ^ jnp.int32(INT_MIN)
        return pltpu.bitcast(jnp.where(key < 0, key ^ jnp.int32(0x7FFFFFFF), key), F32)

    half_bits = 16
    neg_fill = jnp.int32(2 ** half_bits - 1)

    def coarse_key(u16):
        return lax.shift_left(u16, half_bits) | jnp.where(u16 >= 2 ** (half_bits - 1), 0, neg_fill)

    def coarse_search(i, u16):
        cand = u16 | lax.shift_left(jnp.int32(1), half_bits - 1 - i)
        cand_b = ordered_float(coarse_key(cand)).astype(BF16)
        n = count(lambda sc, c: sc >= cand_b, coarse_ref, BF16)
        return jnp.where(n >= topk, cand, u16)

    u16 = lax.fori_loop(0, half_bits, coarse_search, jnp.zeros((1, tq), I32))
    base_u = coarse_key(u16) - jnp.int32(2 ** half_bits)

    def fine_search(i, d):
        cand = d | lax.shift_left(jnp.int32(1), half_bits - i)
        cand_f = ordered_float(base_u + cand)
        return jnp.where(count(lambda sc, c: sc >= cand_f) >= topk, cand, d)

    thr = ordered_float(base_u + lax.fori_loop(0, half_bits + 1, fine_search,
                                               jnp.zeros((1, tq), I32)))
    thr = jnp.where(tpos[0:1, :] < topk, -jnp.inf, thr)

    need = (topk - count(lambda sc, c: sc > thr)).astype(F32)
    tri = jnp.where(sub >= lax.broadcasted_iota(I32, (tk, tk), 1), 1.0, 0.0).astype(BF16)

    def mask_chunk(c, ties_before):
        sc = score_ref[c]
        tie = jnp.where(sc == thr, 1.0, 0.0)
        tie_rank = jnp.dot(tri, tie.astype(BF16), preferred_element_type=F32) + ties_before
        sel = jnp.where(sc > thr, 1.0, jnp.where(tie_rank <= need, tie, 0.0))
        mask_ref[c] = jnp.where(c * tk + sub <= tpos, sel, 0.0).astype(BF16)
        return tie_rank[tk - 1:tk, :]

    grp = N_ATT_HEADS // N_KV_HEADS
    hd = ATT_HEAD_DIM
    klane = lax.broadcasted_iota(I32, (k_ref.shape[2], LANES), 1)

    @pl.when(qblk == 0)
    def _():
        for g in range(N_KV_HEADS):
            kf = k_ref[0, g].astype(F32)
            norm2 = jnp.sum(jnp.where(klane < hd, kf * kf, 0.0), axis=1, keepdims=True)
            kmax_ref[g:g + 1, :] = jnp.broadcast_to(jnp.max(norm2, axis=0, keepdims=True), (1, tq))

    shift_row = lax.broadcasted_iota(I32, (LANES - hd, tq), 0) == 0
    for h in range(N_ATT_HEADS):
        qs = qt_ref[0, h * hd:(h + 1) * hd, :]
        qf = qs.astype(F32)
        r = jnp.sqrt(jnp.sum(qf * qf, axis=0, keepdims=True) * kmax_ref[h // grp:h // grp + 1, :])
        r = r * SHIFT_MARGIN
        qa_ref[h] = jnp.concatenate([qs, jnp.where(shift_row, -r, 0.0).astype(BF16)], axis=0)

    def scores_of(c, g):
        kc = k_ref[0, g, pl.ds(pl.multiple_of(c * tk, tk), tk), :]
        return [jnp.dot(kc, qa_ref[g * grp + i], preferred_element_type=F32) for i in range(grp)]

    def probs_of(c, g, scores):
        mk = mask_ref[c]
        return ([jnp.exp2(s).astype(BF16) * mk for s in scores],)

    def pv_of(c, g, probs):
        vc = vt_ref[0, g, c]
        for i in range(grp):
            h = g * grp + i
            acc_ref[h] = acc_ref[h] + jnp.dot(vc, probs[i], preferred_element_type=F32)

    def online_probs_of(c, g, scores):
        bias = jnp.where(mask_ref[c].astype(F32) > 0.0, 0.0, -jnp.inf)
        probs, alphas = [], []
        for i in range(grp):
            h = g * grp + i
            m = m_ref[h:h + 1, :]
            s = scores[i] + bias
            m_new = jnp.maximum(m, jnp.max(_fold_rows(s, jnp.maximum), axis=0, keepdims=True))
            m_safe = jnp.where(m_new == -jnp.inf, 0.0, m_new)
            alphas.append(jnp.exp2(m - m_safe))
            probs.append(jnp.exp2(s - m_safe).astype(BF16))
            m_ref[h:h + 1, :] = m_new
        return probs, alphas

    def online_pv_of(c, g, probs, alphas):
        vc = vt_ref[0, g, c]
        for i in range(grp):
            h = g * grp + i
            acc_ref[h] = alphas[i] * acc_ref[h] + jnp.dot(vc, probs[i], preferred_element_type=F32)

    def attend(chunks, probs_fn, pv_fn):
        units = [(c, g) for c in chunks for g in range(N_KV_HEADS)]
        scores = scores_of(*units[0])
        pending = None
        for n, unit in enumerate(units):
            nxt = scores_of(*units[n + 1]) if n + 1 < len(units) else None
            cur = probs_fn(*unit, scores)
            if pending is not None:
                pv_fn(*units[n - 1], *pending)
            scores, pending = nxt, cur
        pv_fn(*units[-1], *pending)

    acc_ref[...] = jnp.zeros(acc_ref.shape, F32)

    def chunk_pair(n, ties):
        ties = mask_chunk(2 * n + 1, mask_chunk(2 * n, ties))
        attend([2 * n, 2 * n + 1], probs_of, pv_of)
        return ties

    ties = lax.fori_loop(0, nch // 2, chunk_pair, jnp.zeros((1, tq), F32))

    @pl.when(nch % 2 == 1)
    def _():
        mask_chunk(nch - 1, ties)
        attend([nch - 1], probs_of, pv_of)

    sums = jnp.concatenate([acc_ref[h, SUM_ROW:SUM_ROW + 1, :] for h in range(N_ATT_HEADS)], axis=0)
    underflow = jnp.where(sums >= MIN_SOFTMAX_SUM, 0, 1)

    @pl.when(jnp.max(underflow) > 0)
    def _():
        m_ref[...] = jnp.full(m_ref.shape, -jnp.inf, F32)
        acc_ref[...] = jnp.zeros(acc_ref.shape, F32)

        def chunk_online(c, carry):
            attend([c], online_probs_of, online_pv_of)
            return carry

        lax.fori_loop(0, nch, chunk_online, 0)

    for h in range(N_ATT_HEADS):
        ot_ref[h * hd:(h + 1) * hd, :] = acc_ref[h, :hd, :] / acc_ref[h, SUM_ROW:SUM_ROW + 1, :]
    o_ref[0] = ot_ref[...].T.astype(BF16)


def _attn_call(ki, qit, tailt, k4, vt5, qt, topk):
    bsz, seq, _ = ki.shape
    tq = vt5.shape[-1]
    nch = seq // tq
    assert tq >= topk and vt5.shape[2] == nch
    return pl.pallas_call(
        functools.partial(_attn_kernel, topk=topk),
        grid=(bsz, nch),
        in_specs=[
            pl.BlockSpec((1, seq, IDX_DIM), lambda b, j: (b, 0, 0)),
            pl.BlockSpec((1, IDX_HEADS * IDX_DIM, tq), lambda b, j: (b, 0, j)),
            pl.BlockSpec((1, LANES, tq), lambda b, j: (b, 0, j)),
            pl.BlockSpec((1, N_KV_HEADS, seq, LANES), lambda b, j: (b, 0, 0, 0)),
            pl.BlockSpec((1, N_KV_HEADS, nch, V_ROWS, tq), lambda b, j: (b, 0, 0, 0, 0)),
            pl.BlockSpec((1, ATT_DIM, tq), lambda b, j: (b, 0, j)),
        ],
        out_specs=pl.BlockSpec((1, tq, ATT_DIM), lambda b, j: (b, j, 0)),
        out_shape=jax.ShapeDtypeStruct((bsz, seq, ATT_DIM), BF16),
        scratch_shapes=[pltpu.VMEM((nch, tq, tq), F32), pltpu.VMEM((nch, tq, tq), BF16),
                        pltpu.VMEM((nch, tq, tq), BF16),
                        pltpu.VMEM((ATT_DIM, tq), F32),
                        pltpu.VMEM((N_ATT_HEADS, tq), F32),
                        pltpu.VMEM((N_ATT_HEADS, V_ROWS, tq), F32),
                        pltpu.VMEM((N_ATT_HEADS, LANES, tq), BF16),
                        pltpu.VMEM((N_KV_HEADS, tq), F32)],
        compiler_params=pltpu.CompilerParams(dimension_semantics=("parallel", "arbitrary"),
                                             vmem_limit_bytes=VMEM_LIMIT_BYTES),
        name="attn",
    )(ki, qit, tailt, k4, vt5, qt)


def _split2(v):
    hi = v.astype(BF16)
    mid = (v - hi.astype(F32)).astype(BF16)
    return jnp.concatenate([hi, mid], axis=1)


def _ssd_kernel(xbc_ref, z_ref, tail_ref, cw_ref, cb_ref, alog_ref, e2_ref,
                dx_ref, nw_ref, out_ref, carry, state, ybuf):
    L = CHUNK
    hpg = SSD_HEADS // SSD_GROUPS
    gw = hpg * SSD_HEAD_DIM

    nshift = CONV_W - 1
    carry_rows = nshift * SUBLANES

    @pl.when(pl.program_id(1) == 0)
    def _():
        carry[...] = jnp.zeros_like(carry)
        state[...] = jnp.zeros_like(state)

    u = xbc_ref[0]
    sub8 = lax.broadcasted_iota(I32, (SUBLANES, XBC_DIM), 0)
    wrapped = []
    for k in range(nshift):
        rows = slice(L - carry_rows + k * SUBLANES, L - carry_rows + (k + 1) * SUBLANES)
        merged = jnp.where(sub8 == SUBLANES - 1, carry[k * SUBLANES:(k + 1) * SUBLANES, :], u[rows, :])
        wrapped.append(pltpu.roll(merged, 1, 0))
    carry[...] = u[L - carry_rows:, :]
    conv = cb_ref[...] + cw_ref[CONV_W - 1:CONV_W, :] * u
    for d in range(1, CONV_W):
        shifted = jnp.concatenate(wrapped[nshift - d:] + [u[:L - d * SUBLANES, :]], axis=0)
        conv = conv + cw_ref[CONV_W - 1 - d:CONV_W - d, :] * shifted
    act = _silu(conv)
    xs = act[:, :D_INNER]

    row = lax.broadcasted_iota(I32, (L, L), 0)
    col = lax.broadcasted_iota(I32, (L, L), 1)
    time_of = lambda r: PERM_STRIDE * (r & (SUBLANES - 1)) + (r >> (SUBLANES.bit_length() - 1))
    lower = time_of(row) >= time_of(col)
    tail = tail_ref[0]
    adt = tail * (-jnp.exp(alog_ref[...]) * LOG2_E)
    hi = adt.astype(BF16)
    r1 = adt - hi.astype(F32)
    mid = r1.astype(BF16)
    lo = (r1 - mid.astype(F32)).astype(BF16)
    ones_l = jnp.where(lower, 1.0, 0.0).astype(BF16)
    a_cum = jnp.dot(jnp.concatenate([ones_l, ones_l, ones_l], axis=1),
                    jnp.concatenate([hi, mid, lo], axis=0), preferred_element_type=F32)
    a_cum_t = a_cum.T
    e2 = e2_ref[...]
    a_cum_x = jnp.dot(_split2(a_cum), e2, preferred_element_type=F32)
    dt_x = jnp.dot(_split2(tail), e2, preferred_element_type=F32)
    a_last_x = a_cum_x[L - 1:L, :]
    xf = xs * dt_x
    xf_b = xf.astype(BF16)
    xd_b = (xf * jnp.exp2(a_last_x - a_cum_x)).astype(BF16)
    ea_x = jnp.exp2(a_cum_x)
    chunk_decay_x = jnp.exp2(a_last_x)

    lane = lax.broadcasted_iota(I32, (L, LANES), 1)
    for g in range(SSD_GROUPS):
        b_f = act[:, D_INNER + g * D_STATE:D_INNER + (g + 1) * D_STATE]
        bg = b_f.astype(BF16)
        bg_t = b_f.T.astype(BF16)
        cg = act[:, D_INNER + BC_DIM + g * D_STATE:D_INNER + BC_DIM + (g + 1) * D_STATE].astype(BF16)
        gs = slice(g * gw, (g + 1) * gw)
        cb = lax.dot_general(cg, bg, (((1,), (1,)), ((), ())), preferred_element_type=F32)
        st_prev = state[:, gs]
        y_off = jnp.dot(cg, st_prev.astype(BF16), preferred_element_type=F32) * ea_x[:, gs]
        st_new = jnp.dot(bg_t, xd_b[:, gs], preferred_element_type=F32)
        state[:, gs] = st_prev * chunk_decay_x[:, gs] + st_new
        for pr in range(hpg // 2):
            h0 = g * hpg + 2 * pr
            xs_pair = xf_b[:, h0 * SSD_HEAD_DIM:(h0 + 2) * SSD_HEAD_DIM]
            ys = []
            for h in (h0, h0 + 1):
                seg = a_cum[:, DT_OFF + h:DT_OFF + h + 1] - a_cum_t[DT_OFF + h:DT_OFF + h + 1, :]
                lmat = jnp.exp2(jnp.where(lower, seg, -jnp.inf))
                ys.append(jnp.dot((cb * lmat).astype(BF16), xs_pair, preferred_element_type=F32))
            ybuf[h0 // 2] = (jnp.where(lane < SSD_HEAD_DIM, ys[0], ys[1])
                             + y_off[:, 2 * pr * SSD_HEAD_DIM:(2 * pr + 2) * SSD_HEAD_DIM])

    ntile = D_INNER // LANES
    y = jnp.concatenate([ybuf[j] for j in range(ntile)], axis=1) + dx_ref[...] * xs
    zf = z_ref[0].astype(F32)
    yz = y * _silu(zf)
    for g in range(SSD_GROUPS):
        gs = slice(g * gw, (g + 1) * gw)
        yg = yz[:, gs]
        msq = jnp.mean(yg * yg, axis=-1, keepdims=True)
        og = yg * lax.rsqrt(msq + LN_EPS) * nw_ref[:, gs]
        for j in range(gw // LANES):
            ybuf[g * (gw // LANES) + j] = og[:, j * LANES:(j + 1) * LANES]
    half = L // 2
    out_ref[0] = jnp.concatenate(
        [jnp.concatenate([ybuf[j, pl.ds(half * (t % 2) + t // 2, SUBLANES, stride=SUBLANES), :]
                          for j in range(ntile)], axis=1)
         for t in range(PERM_STRIDE)], axis=0).astype(BF16)


def _ssd_call(xbc, z, tail, cw, cb, alog, e2, dx, nw):
    bsz, seq, _ = xbc.shape
    L = CHUNK
    blk = lambda n: pl.BlockSpec((1, L, n), lambda b, c: (b, c, 0))
    consts = (cw, cb, alog, e2, dx, nw)
    return pl.pallas_call(
        _ssd_kernel,
        grid=(bsz, seq // L),
        in_specs=[blk(XBC_DIM), blk(D_INNER), blk(LANES)] + [_const_spec(c.shape) for c in consts],
        out_specs=blk(D_INNER),
        out_shape=jax.ShapeDtypeStruct((bsz, seq, D_INNER), BF16),
        scratch_shapes=[pltpu.VMEM(((CONV_W - 1) * SUBLANES, XBC_DIM), F32),
                        pltpu.VMEM((D_STATE, D_INNER), F32),
                        pltpu.VMEM((D_INNER // LANES, L, LANES), F32)],
        compiler_params=pltpu.CompilerParams(dimension_semantics=("parallel", "arbitrary"),
                                             vmem_limit_bytes=VMEM_LIMIT_BYTES),
        name="ssd",
    )(xbc, z, tail, *consts)


def _layer_norm(v, g, b):
    mu = jnp.mean(v, axis=-1, keepdims=True)
    d = v - mu
    var = jnp.mean(d * d, axis=-1, keepdims=True)
    return d * lax.rsqrt(var + LN_EPS) * g + b


def _mix_kernel(att_ref, ssd_ref, g_ref, x_ref, wa_ref, ws_ref, wo_ref, l1g_ref, l1b_ref,
                wu_ref, wd_ref, l2g_ref, l2b_ref, o_ref, *, alpha, ff_chunk):
    tm = x_ref.shape[0]
    subs = [slice(s * MIX_SUB_ROWS, (s + 1) * MIX_SUB_ROWS) for s in range(tm // MIX_SUB_ROWS)]
    mixed = []
    for r in subs:
        ga = g_ref[r, :D_MODEL].astype(F32)
        gs = g_ref[r, D_MODEL:].astype(F32)
        mixed.append(
            _sigmoid(ga) * jnp.dot(att_ref[r, :], wa_ref[...], preferred_element_type=F32)
            + _sigmoid(gs) * jnp.dot(ssd_ref[r, :], ws_ref[...], preferred_element_type=F32))
    hs = []
    for r, mx in zip(subs, mixed):
        proj = jnp.dot(mx.astype(BF16), wo_ref[...], preferred_element_type=F32)
        hs.append(_layer_norm(alpha * x_ref[r, :] + proj, l1g_ref[...], l1b_ref[...]))
    hbs = [h.astype(BF16) for h in hs]
    ffs = [jnp.zeros_like(h) for h in hs]
    for c in range(D_FF // ff_chunk):
        cs = slice(c * ff_chunk, (c + 1) * ff_chunk)
        for s in range(len(subs)):
            u = jnp.maximum(jnp.dot(hbs[s], wu_ref[:, cs], preferred_element_type=F32), 0.0)
            ffs[s] = ffs[s] + jnp.dot((u * u).astype(BF16), wd_ref[cs, :],
                                      preferred_element_type=F32)
    for r, h, ff in zip(subs, hs, ffs):
        o_ref[r, :] = _layer_norm(alpha * h + ff, l2g_ref[...], l2b_ref[...])


def _mix_call(att, ssd, gates, x2, wa, ws, wo, l1g, l1b, wu, wd, l2g, l2b, alpha, tm):
    m = x2.shape[0]
    row = lambda n: pl.BlockSpec((tm, n), lambda i: (i, 0))
    consts = (wa, ws, wo, l1g, l1b, wu, wd, l2g, l2b)
    return pl.pallas_call(
        functools.partial(_mix_kernel, alpha=alpha, ff_chunk=1024),
        grid=(m // tm,),
        in_specs=[row(ATT_DIM), row(D_INNER), row(2 * D_MODEL), row(D_MODEL)]
                 + [_const_spec(c.shape) for c in consts],
        out_specs=row(D_MODEL),
        out_shape=jax.ShapeDtypeStruct((m, D_MODEL), F32),
        compiler_params=pltpu.CompilerParams(dimension_semantics=("parallel",),
                                             vmem_limit_bytes=VMEM_LIMIT_BYTES),
        name="mix",
    )(att, ssd, gates, x2, *consts)


def _rope_tables(seq):
    half = ROT_DIM // 2
    inv = ROPE_THETA ** (-jnp.arange(0, ROT_DIM, 2, dtype=F32) / ROT_DIM)
    ang = jnp.arange(seq, dtype=F32)[:, None] * inv[None, :]
    cos, sin = jnp.cos(ang), jnp.sin(ang)
    pad = jnp.zeros((seq, ATT_HEAD_DIM - ROT_DIM), F32)
    zero = jnp.zeros((seq, half), F32)
    rc = jnp.concatenate([cos, cos, pad + 1.0], -1)
    rs1 = jnp.concatenate([-sin, zero, pad], -1)
    rs2 = jnp.concatenate([zero, sin, pad], -1)
    rep = LANES // ATT_HEAD_DIM
    return tuple(jnp.tile(t, (1, rep)) for t in (rc, rs1, rs2))


def _lane_row(v, off):
    return jnp.zeros((1, LANES), F32).at[0, off:off + v.shape[0]].set(v)


def _layer(x, tables, topk, alpha, w_in, q_norm_w, w_uq, w_iq, k_idx_norm_g, k_idx_norm_b,
           conv_w, conv_b, dt_bias, a_log, d_skip, ssd_norm_w, w_attn_branch, w_ssd_branch,
           w_out, ln1_g, ln1_b, w_up, w_down, ln2_g, ln2_b):
    bsz, seq, _ = x.shape
    m = bsz * seq

    w_packed = _pack_call(w_in.T)

    qt, qit, k4, vt5, ki, tail, tailt, z, xbc, gates = _proj_call(
        x, w_packed, q_norm_w.reshape(1, Q_LORA), w_uq.astype(BF16), w_iq.astype(BF16),
        _lane_row(k_idx_norm_g, 0), _lane_row(k_idx_norm_b, 0), _lane_row(dt_bias, DT_OFF), *tables)

    att = _attn_call(ki, qit, tailt, k4, vt5, qt, topk)

    slot = jnp.arange(LANES)[:, None] - DT_OFF
    expand = (slot == jnp.arange(D_INNER)[None, :] // SSD_HEAD_DIM).astype(BF16)
    e2 = jnp.concatenate([expand, expand], axis=0)
    ssd = _ssd_call(xbc, z, tail, conv_w, conv_b.reshape(1, XBC_DIM), _lane_row(a_log, DT_OFF), e2,
                    jnp.repeat(d_skip, SSD_HEAD_DIM).reshape(1, D_INNER),
                    ssd_norm_w.reshape(1, D_INNER))

    r = lambda t: t.reshape(1, -1)
    out = _mix_call(att.reshape(m, ATT_DIM), ssd.reshape(m, D_INNER), gates.reshape(m, 2 * D_MODEL),
                    x.reshape(m, D_MODEL), w_attn_branch.astype(BF16), w_ssd_branch.astype(BF16),
                    w_out.astype(BF16), r(ln1_g), r(ln1_b), w_up.astype(BF16), w_down.astype(BF16),
                    r(ln2_g), r(ln2_b), alpha, tm=MIX_ROWS)
    return out.reshape(bsz, seq, D_MODEL)


def kernel(x, w_in, q_norm_w, w_uq, w_iq, k_idx_norm_g, k_idx_norm_b, conv_w, conv_b, dt_bias,
           a_log, d_skip, ssd_norm_w, w_attn_branch, w_ssd_branch, w_out, ln1_g, ln1_b, w_up,
           w_down, ln2_g, ln2_b):
    depth = w_in.shape[0]
    seq = x.shape[1]
    topk = min(TOPK_MAX, seq // 4)
    alpha = (2 * depth) ** 0.25
    tables = _rope_tables(seq)
    h = x
    for l in range(depth):
        h = _layer(h, tables, topk, alpha, w_in[l], q_norm_w[l], w_uq[l], w_iq[l], k_idx_norm_g[l],
                   k_idx_norm_b[l], conv_w[l], conv_b[l], dt_bias[l], a_log[l], d_skip[l],
                   ssd_norm_w[l], w_attn_branch[l], w_ssd_branch[l], w_out[l], ln1_g[l], ln1_b[l],
                   w_up[l], w_down[l], ln2_g[l], ln2_b[l])
    return h
```

```python
import functools

import jax
import jax.numpy as jnp
from jax import lax
from jax.experimental import pallas as pl
from jax.experimental.pallas import tpu as pltpu

F32 = jnp.float32
BF16 = jnp.bfloat16
I32 = jnp.int32

D_MODEL = 1024
N_ATT_HEADS = 16
ATT_HEAD_DIM = 64
N_KV_HEADS = 4
ATT_DIM = N_ATT_HEADS * ATT_HEAD_DIM
KV_DIM = N_KV_HEADS * ATT_HEAD_DIM
Q_LORA = 256
IDX_HEADS = 8
IDX_DIM = 64
TOPK_MAX = 256
ROT_DIM = ATT_HEAD_DIM // 4
ROPE_THETA = 500000.0
D_INNER = 2 * D_MODEL
SSD_HEAD_DIM = 64
SSD_HEADS = D_INNER // SSD_HEAD_DIM
SSD_GROUPS = 4
D_STATE = 128
CONV_W = 4
CHUNK = 128
BC_DIM = SSD_GROUPS * D_STATE
XBC_DIM = D_INNER + 2 * BC_DIM
D_FF = 4 * D_MODEL
IN_SIZES = (Q_LORA, KV_DIM, KV_DIM, IDX_DIM, IDX_HEADS, D_INNER, XBC_DIM, SSD_HEADS, 2 * D_MODEL)
LN_EPS = 1e-5

LANES = 128
SUBLANES = 8
VMEM_LIMIT_BYTES = 56 * 1024 * 1024

SEQ_TILE = 256
MIX_ROWS = 512
MIX_SUB_ROWS = 256

SMALL_COLS = Q_LORA + 2 * KV_DIM + LANES
TAIL_OFF = Q_LORA + 2 * KV_DIM
WI_OFF = IDX_DIM
WI_END = WI_OFF + IDX_HEADS
DT_OFF = 0
TAIL_END = DT_OFF + SSD_HEADS
WDT_OFF = SMALL_COLS
WZ_OFF = WDT_OFF + LANES
WX_OFF = WZ_OFF + D_INNER
WG_OFF = WX_OFF + XBC_DIM
PACKED_COLS = WG_OFF + 2 * D_MODEL

PERM_STRIDE = CHUNK // SUBLANES
BF16_SUBLANES = 16
V_ROWS = ATT_HEAD_DIM + BF16_SUBLANES
SUM_ROW = ATT_HEAD_DIM
SHIFT_MARGIN = 1.0 + 2.0 ** -6
MIN_SOFTMAX_SUM = 2.0 ** -40
LOG2_E = 1.4426950408889634
INT_MIN = -(2 ** 31)


def _sigmoid(v):
    return 0.5 + 0.5 * jnp.tanh(0.5 * v)


def _silu(v):
    h = 0.5 * v
    return h + h * jnp.tanh(h)


def _const_spec(shape):
    zeros = (0,) * len(shape)
    return pl.BlockSpec(shape, lambda *_: zeros, pipeline_mode=pl.Buffered(1))


PACK_TILES = 4
ROW_GROUPS = LANES // SUBLANES


def _pack_tile_source(t):
    offs = [0]
    for s in IN_SIZES:
        offs.append(offs[-1] + s)
    assert all(o % SUBLANES == 0 for o in offs)
    first = lambda seg_index, tile0: offs[seg_index] // SUBLANES + ROW_GROUPS * (t - tile0)
    return jnp.where(t < WDT_OFF // LANES, ROW_GROUPS * t,
                     jnp.where(t < WZ_OFF // LANES, first(7, WDT_OFF // LANES),
                               jnp.where(t < WX_OFF // LANES, first(5, WZ_OFF // LANES),
                                         jnp.where(t < WG_OFF // LANES, first(6, WX_OFF // LANES),
                                                   first(8, WG_OFF // LANES)))))


def _pack_kernel(*refs):
    o_ref = refs[-1]
    for i, w_ref in enumerate(refs[:-1]):
        rows = w_ref[...].reshape(LANES, w_ref.shape[-1])
        o_ref[:, i * LANES:(i + 1) * LANES] = rows.T.astype(BF16)


def _pack_call(w_t):
    cols, rows = w_t.shape
    w3 = w_t.reshape(cols // SUBLANES, SUBLANES, rows)
    tile_spec = lambda i: pl.BlockSpec(
        (pl.Element(ROW_GROUPS), pl.Element(SUBLANES), pl.Element(rows)),
        lambda j: (_pack_tile_source(PACK_TILES * j + i), 0, 0))
    return pl.pallas_call(
        _pack_kernel,
        grid=(PACKED_COLS // (PACK_TILES * LANES),),
        in_specs=[tile_spec(i) for i in range(PACK_TILES)],
        out_specs=pl.BlockSpec((rows, PACK_TILES * LANES), lambda j: (0, j)),
        out_shape=jax.ShapeDtypeStruct((rows, PACKED_COLS), BF16),
        compiler_params=pltpu.CompilerParams(dimension_semantics=("parallel",),
                                             vmem_limit_bytes=VMEM_LIMIT_BYTES),
        name="pack",
    )(*([w3] * PACK_TILES))


def _proj_kernel(x_ref, w_ref, perm_ref, qnw_ref, wuq_ref, wiq_ref,
                 kng_ref, knb_ref, dtb_ref, rc_ref, rs1_ref, rs2_ref,
                 qt_ref, qit_ref, k4_ref, vt_ref, ki_ref, tail_ref, tailt_ref,
                 z_ref, xbc_ref, g_ref):
    xb = x_ref[0].astype(BF16)
    rc, rs1, rs2 = rc_ref[...], rs1_ref[...], rs2_ref[...]
    hd = ATT_HEAD_DIM
    w_cols = lambda off, n: w_ref[:, off:off + n]

    xp = jnp.dot(perm_ref[...], xb, preferred_element_type=F32).astype(BF16)

    def rope(t):
        half = ROT_DIM // 2
        return (t * rc + pltpu.roll(t, LANES - half, 1) * rs1 + pltpu.roll(t, half, 1) * rs2)

    pa = jnp.dot(xb, w_cols(0, SMALL_COLS), preferred_element_type=F32)

    lane = lax.broadcasted_iota(I32, (xb.shape[0], LANES), 1)
    dtr_p = jnp.dot(xp, w_cols(WDT_OFF, LANES), preferred_element_type=F32) + dtb_ref[...]
    dt_p = jnp.maximum(dtr_p, 0.0) + jnp.log1p(jnp.exp(-jnp.abs(dtr_p)))
    tail_ref[0] = jnp.where((lane >= DT_OFF) & (lane < TAIL_END), dt_p, 0.0)
    z_ref[0] = jnp.dot(xp, w_cols(WZ_OFF, D_INNER), preferred_element_type=F32).astype(BF16)
    xbc_ref[0] = jnp.dot(xp, w_cols(WX_OFF, XBC_DIM), preferred_element_type=F32)

    c_q = pa[:, :Q_LORA]
    ms = jnp.mean(c_q * c_q, axis=-1, keepdims=True)
    cq = (c_q * lax.rsqrt(ms + LN_EPS) * qnw_ref[...]).astype(BF16)

    scale = ATT_HEAD_DIM ** -0.5 * LOG2_E
    qf = jnp.dot(cq, wuq_ref[...], preferred_element_type=F32)
    for j in range(ATT_DIM // LANES):
        sl = slice(j * LANES, (j + 1) * LANES)
        qt_ref[0, sl, :] = (rope(qf[:, sl]) * scale).T.astype(BF16)
    qif = jnp.dot(cq, wiq_ref[...], preferred_element_type=F32)
    iscale = IDX_DIM ** -0.5
    for j in range(IDX_HEADS * IDX_DIM // LANES):
        sl = slice(j * LANES, (j + 1) * LANES)
        qit_ref[0, sl, :] = (rope(qif[:, sl]) * iscale).T.astype(BF16)
    for j in range(KV_DIM // LANES):
        kr = rope(pa[:, Q_LORA + j * LANES:Q_LORA + (j + 1) * LANES])
        for i, kh in enumerate((kr, pltpu.roll(kr, hd, 1))):
            k4_ref[0, 2 * j + i] = jnp.where(lane < hd, kh, jnp.where(lane == hd, 1.0, 0.0)).astype(BF16)
        vtr = pa[:, Q_LORA + KV_DIM + j * LANES:Q_LORA + KV_DIM + (j + 1) * LANES].T
        sum_rows = jnp.where(lax.broadcasted_iota(I32, (V_ROWS - hd, vtr.shape[1]), 0) == 0,
                             1.0, 0.0).astype(BF16)
        vt_ref[0, 2 * j, 0] = jnp.concatenate([vtr[:hd, :].astype(BF16), sum_rows], axis=0)
        vt_ref[0, 2 * j + 1, 0] = jnp.concatenate([vtr[hd:, :].astype(BF16), sum_rows], axis=0)

    tail = pa[:, TAIL_OFF:TAIL_OFF + LANES]
    is_ki = lane < IDX_DIM
    mu = jnp.sum(jnp.where(is_ki, tail, 0.0), axis=-1, keepdims=True) * (1.0 / IDX_DIM)
    dv = jnp.where(is_ki, tail - mu, 0.0)
    var = jnp.sum(dv * dv, axis=-1, keepdims=True) * (1.0 / IDX_DIM)
    ki = rope(dv * lax.rsqrt(var + LN_EPS) * kng_ref[...] + knb_ref[...])
    wi = tail * (IDX_HEADS ** -0.5)
    ki_ref[0] = ki[:, :IDX_DIM].astype(BF16)
    tailt_ref[0] = jnp.where(is_ki, ki, jnp.where(lane < WI_END, wi, 0.0)).T

    g_ref[0] = jnp.dot(xb, w_cols(WG_OFF, 2 * D_MODEL), preferred_element_type=F32).astype(BF16)


def _proj_call(x, w_packed, qnw, wuq, wiq, kng, knb, dtb, rc, rs1, rs2):
    bsz, seq, _ = x.shape
    tm = SEQ_TILE
    nsb = seq // tm
    r = jnp.arange(tm)
    src = (r // CHUNK) * CHUNK + PERM_STRIDE * (r % SUBLANES) + (r % CHUNK) // SUBLANES
    perm = (jnp.arange(tm)[None, :] == src[:, None]).astype(BF16)
    row = lambda n: pl.BlockSpec((1, tm, n), lambda b, i: (b, i, 0))
    col = lambda n: pl.BlockSpec((1, n, tm), lambda b, i: (b, 0, i))
    tab = pl.BlockSpec((tm, LANES), lambda b, i: (i, 0))
    sds = jax.ShapeDtypeStruct
    out_shapes = (
        sds((bsz, ATT_DIM, seq), BF16),
        sds((bsz, IDX_HEADS * IDX_DIM, seq), BF16),
        sds((bsz, N_KV_HEADS, seq, LANES), BF16),
        sds((bsz, N_KV_HEADS, nsb, V_ROWS, tm), BF16),
        sds((bsz, seq, IDX_DIM), BF16),
        sds((bsz, seq, LANES), F32),
        sds((bsz, LANES, seq), F32),
        sds((bsz, seq, D_INNER), BF16),
        sds((bsz, seq, XBC_DIM), F32),
        sds((bsz, seq, 2 * D_MODEL), BF16),
    )
    out_specs = [
        col(ATT_DIM), col(IDX_HEADS * IDX_DIM),
        pl.BlockSpec((1, N_KV_HEADS, tm, LANES), lambda b, i: (b, 0, i, 0)),
        pl.BlockSpec((1, N_KV_HEADS, 1, V_ROWS, tm), lambda b, i: (b, 0, i, 0, 0)),
        row(IDX_DIM), row(LANES), col(LANES), row(D_INNER), row(XBC_DIM), row(2 * D_MODEL),
    ]
    consts = (w_packed, perm, qnw, wuq, wiq, kng, knb, dtb)
    return pl.pallas_call(
        _proj_kernel,
        grid=(bsz, nsb),
        in_specs=[row(D_MODEL)] + [_const_spec(c.shape) for c in consts] + [tab, tab, tab],
        out_specs=out_specs,
        out_shape=out_shapes,
        compiler_params=pltpu.CompilerParams(dimension_semantics=("parallel", "parallel"),
                                             vmem_limit_bytes=VMEM_LIMIT_BYTES),
        name="proj",
    )(x, *consts, rc, rs1, rs2)


COUNT_ROWS = 32


def _fold_rows(v, op, rows=32):
    while v.shape[0] > rows:
        half = v.shape[0] // 2
        v = op(v[:half], v[half:])
    return v


def _attn_kernel(ki_ref, qit_ref, wit_ref, k_ref, vt_ref, qt_ref, o_ref,
                 score_ref, coarse_ref, mask_ref, ot_ref, m_ref, acc_ref, qa_ref, kmax_ref,
                 *, topk):
    tq = qt_ref.shape[2]
    tk = tq
    qblk = pl.program_id(1)
    nch = qblk + 1
    sub = lax.broadcasted_iota(I32, (tk, tq), 0)
    tpos = qblk * tq + lax.broadcasted_iota(I32, (tk, tq), 1)

    def score_chunk(c, carry):
        off = pl.multiple_of(c * tk, tk)
        kic = ki_ref[0, pl.ds(off, tk), :]
        acc = jnp.zeros((tk, tq), F32)
        for hh in range(IDX_HEADS):
            sc = jnp.dot(kic, qit_ref[0, hh * IDX_DIM:(hh + 1) * IDX_DIM, :],
                         preferred_element_type=F32)
            acc = acc + wit_ref[0, WI_OFF + hh:WI_OFF + hh + 1, :] * jnp.maximum(sc, 0.0)
        masked = jnp.where(off + sub <= tpos, acc, -jnp.inf)
        score_ref[c] = masked
        coarse_ref[c] = masked.astype(BF16)
        return carry

    lax.fori_loop(0, nch, score_chunk, 0)

    def count(pred, ref=score_ref, dtype=I32):
        def body(c, part):
            hit = jnp.where(pred(ref[c], c), jnp.ones((), dtype), jnp.zeros((), dtype))
            return part + _fold_rows(hit, jnp.add, COUNT_ROWS)

        part = lax.fori_loop(0, nch, body, jnp.zeros((COUNT_ROWS, tq), dtype))
        return jnp.sum(part.astype(I32), axis=0, keepdims=True)

    def ordered_float(u):
        key = u ^ jnp.int32(INT_MIN)
        return pltpu.bitcast(jnp.where(key < 0, key ^ jnp.int32(0x7FFFFFFF), key), F32)

    half_bits = 16
    neg_fill = jnp.int32(2 ** half_bits - 1)

    def coarse_key(u16):
        return lax.shift_left(u16, half_bits) | jnp.where(u16 >= 2 ** (half_bits - 1), 0, neg_fill)

    def coarse_search(i, u16):
        cand = u16 | lax.shift_left(jnp.int32(1), half_bits - 1 - i)
        cand_b = ordered_float(coarse_key(cand)).astype(BF16)
        n = count(lambda sc, c: sc >= cand_b, coarse_ref, BF16)
        return jnp.where(n >= topk, cand, u16)

    u16 = lax.fori_loop(0, half_bits, coarse_search, jnp.zeros((1, tq), I32))
    base_u = coarse_key(u16) - jnp.int32(2 ** half_bits)

    def fine_search(i, d):
        cand = d | lax.shift_left(jnp.int32(1), half_bits - i)
        cand_f = ordered_float(base_u + cand)
        return jnp.where(count(lambda sc, c: sc >= cand_f) >= topk, cand, d)

    thr = ordered_float(base_u + lax.fori_loop(0, half_bits + 1, fine_search,
                                               jnp.zeros((1, tq), I32)))
    thr = jnp.where(tpos[0:1, :] < topk, -jnp.inf, thr)

    need = (topk - count(lambda sc, c: sc > thr)).astype(F32)
    tri = jnp.where(sub >= lax.broadcasted_iota(I32, (tk, tk), 1), 1.0, 0.0).astype(BF16)

    def mask_chunk(c, ties_before):
        sc = score_ref[c]
        tie = jnp.where(sc == thr, 1.0, 0.0)
        tie_rank = jnp.dot(tri, tie.astype(BF16), preferred_element_type=F32) + ties_before
        sel = jnp.where(sc > thr, 1.0, jnp.where(tie_rank <= need, tie, 0.0))
        mask_ref[c] = jnp.where(c * tk + sub <= tpos, sel, 0.0).astype(BF16)
        return tie_rank[tk - 1:tk, :]

    grp = N_ATT_HEADS // N_KV_HEADS
    hd = ATT_HEAD_DIM
    klane = lax.broadcasted_iota(I32, (k_ref.shape[2], LANES), 1)

    @pl.when(qblk == 0)
    def _():
        for g in range(N_KV_HEADS):
            kf = k_ref[0, g].astype(F32)
            norm2 = jnp.sum(jnp.where(klane < hd, kf * kf, 0.0), axis=1, keepdims=True)
            kmax_ref[g:g + 1, :] = jnp.broadcast_to(jnp.max(norm2, axis=0, keepdims=True), (1, tq))

    shift_row = lax.broadcasted_iota(I32, (LANES - hd, tq), 0) == 0
    for h in range(N_ATT_HEADS):
        qs = qt_ref[0, h * hd:(h + 1) * hd, :]
        qf = qs.astype(F32)
        r = jnp.sqrt(jnp.sum(qf * qf, axis=0, keepdims=True) * kmax_ref[h // grp:h // grp + 1, :])
        r = r * SHIFT_MARGIN
        qa_ref[h] = jnp.concatenate([qs, jnp.where(shift_row, -r, 0.0).astype(BF16)], axis=0)

    def scores_of(c, g):
        kc = k_ref[0, g, pl.ds(pl.multiple_of(c * tk, tk), tk), :]
        return [jnp.dot(kc, qa_ref[g * grp + i], preferred_element_type=F32) for i in range(grp)]

    def probs_of(c, g, scores):
        mk = mask_ref[c]
        return ([jnp.exp2(s).astype(BF16) * mk for s in scores],)

    def pv_of(c, g, probs):
        vc = vt_ref[0, g, c]
        for i in range(grp):
            h = g * grp + i
            acc_ref[h] = acc_ref[h] + jnp.dot(vc, probs[i], preferred_element_type=F32)

    def online_probs_of(c, g, scores):
        bias = jnp.where(mask_ref[c].astype(F32) > 0.0, 0.0, -jnp.inf)
        probs, alphas = [], []
        for i in range(grp):
            h = g * grp + i
            m = m_ref[h:h + 1, :]
            s = scores[i] + bias
            m_new = jnp.maximum(m, jnp.max(_fold_rows(s, jnp.maximum), axis=0, keepdims=True))
            m_safe = jnp.where(m_new == -jnp.inf, 0.0, m_new)
            alphas.append(jnp.exp2(m - m_safe))
            probs.append(jnp.exp2(s - m_safe).astype(BF16))
            m_ref[h:h + 1, :] = m_new
        return probs, alphas

    def online_pv_of(c, g, probs, alphas):
        vc = vt_ref[0, g, c]
        for i in range(grp):
            h = g * grp + i
            acc_ref[h] = alphas[i] * acc_ref[h] + jnp.dot(vc, probs[i], preferred_element_type=F32)

    def attend(chunks, probs_fn, pv_fn):
        units = [(c, g) for c in chunks for g in range(N_KV_HEADS)]
        scores = scores_of(*units[0])
        pending = None
        for n, unit in enumerate(units):
            nxt = scores_of(*units[n + 1]) if n + 1 < len(units) else None
            cur = probs_fn(*unit, scores)
            if pending is not None:
                pv_fn(*units[n - 1], *pending)
            scores, pending = nxt, cur
        pv_fn(*units[-1], *pending)

    acc_ref[...] = jnp.zeros(acc_ref.shape, F32)

    def chunk_pair(n, ties):
        ties = mask_chunk(2 * n + 1, mask_chunk(2 * n, ties))
        attend([2 * n, 2 * n + 1], probs_of, pv_of)
        return ties

    ties = lax.fori_loop(0, nch // 2, chunk_pair, jnp.zeros((1, tq), F32))

    @pl.when(nch % 2 == 1)
    def _():
        mask_chunk(nch - 1, ties)
        attend([nch - 1], probs_of, pv_of)

    sums = jnp.concatenate([acc_ref[h, SUM_ROW:SUM_ROW + 1, :] for h in range(N_ATT_HEADS)], axis=0)
    underflow = jnp.where(sums >= MIN_SOFTMAX_SUM, 0, 1)

    @pl.when(jnp.max(underflow) > 0)
    def _():
        m_ref[...] = jnp.full(m_ref.shape, -jnp.inf, F32)
        acc_ref[...] = jnp.zeros(acc_ref.shape, F32)

        def chunk_online(c, carry):
            attend([c], online_probs_of, online_pv_of)
            return carry

        lax.fori_loop(0, nch, chunk_online, 0)

    for h in range(N_ATT_HEADS):
        ot_ref[h * hd:(h + 1) * hd, :] = acc_ref[h, :hd, :] / acc_ref[h, SUM_ROW:SUM_ROW + 1, :]
    o_ref[0] = ot_ref[...].T.astype(BF16)


def _attn_call(ki, qit, tailt, k4, vt5, qt, topk):
    bsz, seq, _ = ki.shape
    tq = vt5.shape[-1]
    nch = seq // tq
    assert tq >= topk and vt5.shape[2] == nch
    assert seq // COUNT_ROWS <= 256
    return pl.pallas_call(
        functools.partial(_attn_kernel, topk=topk),
        grid=(bsz, nch),
        in_specs=[
            pl.BlockSpec((1, seq, IDX_DIM), lambda b, j: (b, 0, 0)),
            pl.BlockSpec((1, IDX_HEADS * IDX_DIM, tq), lambda b, j: (b, 0, j)),
            pl.BlockSpec((1, LANES, tq), lambda b, j: (b, 0, j)),
            pl.BlockSpec((1, N_KV_HEADS, seq, LANES), lambda b, j: (b, 0, 0, 0)),
            pl.BlockSpec((1, N_KV_HEADS, nch, V_ROWS, tq), lambda b, j: (b, 0, 0, 0, 0)),
            pl.BlockSpec((1, ATT_DIM, tq), lambda b, j: (b, 0, j)),
        ],
        out_specs=pl.BlockSpec((1, tq, ATT_DIM), lambda b, j: (b, j, 0)),
        out_shape=jax.ShapeDtypeStruct((bsz, seq, ATT_DIM), BF16),
        scratch_shapes=[pltpu.VMEM((nch, tq, tq), F32), pltpu.VMEM((nch, tq, tq), BF16),
                        pltpu.VMEM((nch, tq, tq), BF16),
                        pltpu.VMEM((ATT_DIM, tq), F32),
                        pltpu.VMEM((N_ATT_HEADS, tq), F32),
                        pltpu.VMEM((N_ATT_HEADS, V_ROWS, tq), F32),
                        pltpu.VMEM((N_ATT_HEADS, LANES, tq), BF16),
                        pltpu.VMEM((N_KV_HEADS, tq), F32)],
        compiler_params=pltpu.CompilerParams(dimension_semantics=("parallel", "arbitrary"),
                                             vmem_limit_bytes=VMEM_LIMIT_BYTES),
        name="attn",
    )(ki, qit, tailt, k4, vt5, qt)


def _split2(v):
    hi = v.astype(BF16)
    mid = (v - hi.astype(F32)).astype(BF16)
    return jnp.concatenate([hi, mid], axis=1)


def _ssd_kernel(xbc_ref, z_ref, tail_ref, cw_ref, cb_ref, alog_ref, e2_ref,
                dx_ref, nw_ref, out_ref, carry, state, ybuf):
    L = CHUNK
    hpg = SSD_HEADS // SSD_GROUPS
    gw = hpg * SSD_HEAD_DIM

    nshift = CONV_W - 1
    carry_rows = nshift * SUBLANES

    @pl.when(pl.program_id(1) == 0)
    def _():
        carry[...] = jnp.zeros_like(carry)
        state[...] = jnp.zeros_like(state)

    u = xbc_ref[0]
    sub8 = lax.broadcasted_iota(I32, (SUBLANES, XBC_DIM), 0)
    wrapped = []
    for k in range(nshift):
        rows = slice(L - carry_rows + k * SUBLANES, L - carry_rows + (k + 1) * SUBLANES)
        merged = jnp.where(sub8 == SUBLANES - 1, carry[k * SUBLANES:(k + 1) * SUBLANES, :], u[rows, :])
        wrapped.append(pltpu.roll(merged, 1, 0))
    carry[...] = u[L - carry_rows:, :]
    conv = cb_ref[...] + cw_ref[CONV_W - 1:CONV_W, :] * u
    for d in range(1, CONV_W):
        shifted = jnp.concatenate(wrapped[nshift - d:] + [u[:L - d * SUBLANES, :]], axis=0)
        conv = conv + cw_ref[CONV_W - 1 - d:CONV_W - d, :] * shifted
    act = _silu(conv)
    xs = act[:, :D_INNER]

    row = lax.broadcasted_iota(I32, (L, L), 0)
    col = lax.broadcasted_iota(I32, (L, L), 1)
    time_of = lambda r: PERM_STRIDE * (r & (SUBLANES - 1)) + (r >> (SUBLANES.bit_length() - 1))
    lower = time_of(row) >= time_of(col)
    tail = tail_ref[0]
    adt = tail * (-jnp.exp(alog_ref[...]) * LOG2_E)
    hi = adt.astype(BF16)
    r1 = adt - hi.astype(F32)
    mid = r1.astype(BF16)
    lo = (r1 - mid.astype(F32)).astype(BF16)
    ones_l = jnp.where(lower, 1.0, 0.0).astype(BF16)
    a_cum = jnp.dot(jnp.concatenate([ones_l, ones_l, ones_l], axis=1),
                    jnp.concatenate([hi, mid, lo], axis=0), preferred_element_type=F32)
    a_cum_t = a_cum.T
    e2 = e2_ref[...]
    a_cum_x = jnp.dot(_split2(a_cum), e2, preferred_element_type=F32)
    dt_x = jnp.dot(_split2(tail), e2, preferred_element_type=F32)
    a_last_x = a_cum_x[L - 1:L, :]
    xf = xs * dt_x
    xf_b = xf.astype(BF16)
    xd_b = (xf * jnp.exp2(a_last_x - a_cum_x)).astype(BF16)
    ea_x = jnp.exp2(a_cum_x)
    chunk_decay_x = jnp.exp2(a_last_x)

    lane = lax.broadcasted_iota(I32, (L, LANES), 1)
    for g in range(SSD_GROUPS):
        b_f = act[:, D_INNER + g * D_STATE:D_INNER + (g + 1) * D_STATE]
        bg = b_f.astype(BF16)
        bg_t = b_f.T.astype(BF16)
        cg = act[:, D_INNER + BC_DIM + g * D_STATE:D_INNER + BC_DIM + (g + 1) * D_STATE].astype(BF16)
        gs = slice(g * gw, (g + 1) * gw)
        cb = lax.dot_general(cg, bg, (((1,), (1,)), ((), ())), preferred_element_type=F32)
        st_prev = state[:, gs]
        y_off = jnp.dot(cg, st_prev.astype(BF16), preferred_element_type=F32) * ea_x[:, gs]
        st_new = jnp.dot(bg_t, xd_b[:, gs], preferred_element_type=F32)
        state[:, gs] = st_prev * chunk_decay_x[:, gs] + st_new
        for pr in range(hpg // 2):
            h0 = g * hpg + 2 * pr
            xs_pair = xf_b[:, h0 * SSD_HEAD_DIM:(h0 + 2) * SSD_HEAD_DIM]
            ys = []
            for h in (h0, h0 + 1):
                seg = a_cum[:, DT_OFF + h:DT_OFF + h + 1] - a_cum_t[DT_OFF + h:DT_OFF + h + 1, :]
                lmat = jnp.exp2(jnp.where(lower, seg, -jnp.inf))
                ys.append(jnp.dot((cb * lmat).astype(BF16), xs_pair, preferred_element_type=F32))
            ybuf[h0 // 2] = (jnp.where(lane < SSD_HEAD_DIM, ys[0], ys[1])
                             + y_off[:, 2 * pr * SSD_HEAD_DIM:(2 * pr + 2) * SSD_HEAD_DIM])

    ntile = D_INNER // LANES
    y = jnp.concatenate([ybuf[j] for j in range(ntile)], axis=1) + dx_ref[...] * xs
    zf = z_ref[0].astype(F32)
    yz = y * _silu(zf)
    for g in range(SSD_GROUPS):
        gs = slice(g * gw, (g + 1) * gw)
        yg = yz[:, gs]
        msq = jnp.mean(yg * yg, axis=-1, keepdims=True)
        og = yg * lax.rsqrt(msq + LN_EPS) * nw_ref[:, gs]
        for j in range(gw // LANES):
            ybuf[g * (gw // LANES) + j] = og[:, j * LANES:(j + 1) * LANES]
    half = L // 2
    out_ref[0] = jnp.concatenate(
        [jnp.concatenate([ybuf[j, pl.ds(half * (t % 2) + t // 2, SUBLANES, stride=SUBLANES), :]
                          for j in range(ntile)], axis=1)
         for t in range(PERM_STRIDE)], axis=0).astype(BF16)


def _ssd_call(xbc, z, tail, cw, cb, alog, e2, dx, nw):
    bsz, seq, _ = xbc.shape
    L = CHUNK
    blk = lambda n: pl.BlockSpec((1, L, n), lambda b, c: (b, c, 0))
    consts = (cw, cb, alog, e2, dx, nw)
    return pl.pallas_call(
        _ssd_kernel,
        grid=(bsz, seq // L),
        in_specs=[blk(XBC_DIM), blk(D_INNER), blk(LANES)] + [_const_spec(c.shape) for c in consts],
        out_specs=blk(D_INNER),
        out_shape=jax.ShapeDtypeStruct((bsz, seq, D_INNER), BF16),
        scratch_shapes=[pltpu.VMEM(((CONV_W - 1) * SUBLANES, XBC_DIM), F32),
                        pltpu.VMEM((D_STATE, D_INNER), F32),
                        pltpu.VMEM((D_INNER // LANES, L, LANES), F32)],
        compiler_params=pltpu.CompilerParams(dimension_semantics=("parallel", "arbitrary"),
                                             vmem_limit_bytes=VMEM_LIMIT_BYTES),
        name="ssd",
    )(xbc, z, tail, *consts)


def _layer_norm(v, g, b):
    mu = jnp.mean(v, axis=-1, keepdims=True)
    d = v - mu
    var = jnp.mean(d * d, axis=-1, keepdims=True)
    return d * lax.rsqrt(var + LN_EPS) * g + b


def _mix_kernel(att_ref, ssd_ref, g_ref, x_ref, wa_ref, ws_ref, wo_ref, l1g_ref, l1b_ref,
                wu_ref, wd_ref, l2g_ref, l2b_ref, o_ref, *, alpha, ff_chunk):
    tm = x_ref.shape[0]
    subs = [slice(s * MIX_SUB_ROWS, (s + 1) * MIX_SUB_ROWS) for s in range(tm // MIX_SUB_ROWS)]
    mixed = []
    for r in subs:
        ga = g_ref[r, :D_MODEL].astype(F32)
        gs = g_ref[r, D_MODEL:].astype(F32)
        mixed.append(
            _sigmoid(ga) * jnp.dot(att_ref[r, :], wa_ref[...], preferred_element_type=F32)
            + _sigmoid(gs) * jnp.dot(ssd_ref[r, :], ws_ref[...], preferred_element_type=F32))
    hs = []
    for r, mx in zip(subs, mixed):
        proj = jnp.dot(mx.astype(BF16), wo_ref[...], preferred_element_type=F32)
        hs.append(_layer_norm(alpha * x_ref[r, :] + proj, l1g_ref[...], l1b_ref[...]))
    hbs = [h.astype(BF16) for h in hs]
    ffs = [jnp.zeros_like(h) for h in hs]
    for c in range(D_FF // ff_chunk):
        cs = slice(c * ff_chunk, (c + 1) * ff_chunk)
        for s in range(len(subs)):
            u = jnp.maximum(jnp.dot(hbs[s], wu_ref[:, cs], preferred_element_type=F32), 0.0)
            ffs[s] = ffs[s] + jnp.dot((u * u).astype(BF16), wd_ref[cs, :],
                                      preferred_element_type=F32)
    for r, h, ff in zip(subs, hs, ffs):
        o_ref[r, :] = _layer_norm(alpha * h + ff, l2g_ref[...], l2b_ref[...])


def _mix_call(att, ssd, gates, x2, wa, ws, wo, l1g, l1b, wu, wd, l2g, l2b, alpha, tm):
    m = x2.shape[0]
    row = lambda n: pl.BlockSpec((tm, n), lambda i: (i, 0))
    consts = (wa, ws, wo, l1g, l1b, wu, wd, l2g, l2b)
    return pl.pallas_call(
        functools.partial(_mix_kernel, alpha=alpha, ff_chunk=1024),
        grid=(m // tm,),
        in_specs=[row(ATT_DIM), row(D_INNER), row(2 * D_MODEL), row(D_MODEL)]
                 + [_const_spec(c.shape) for c in consts],
        out_specs=row(D_MODEL),
        out_shape=jax.ShapeDtypeStruct((m, D_MODEL), F32),
        compiler_params=pltpu.CompilerParams(dimension_semantics=("parallel",),
                                             vmem_limit_bytes=VMEM_LIMIT_BYTES),
        name="mix",
    )(att, ssd, gates, x2, *consts)


def _rope_tables(seq):
    half = ROT_DIM // 2
    inv = ROPE_THETA ** (-jnp.arange(0, ROT_DIM, 2, dtype=F32) / ROT_DIM)
    ang = jnp.arange(seq, dtype=F32)[:, None] * inv[None, :]
    cos, sin = jnp.cos(ang), jnp.sin(ang)
    pad = jnp.zeros((seq, ATT_HEAD_DIM - ROT_DIM), F32)
    zero = jnp.zeros((seq, half), F32)
    rc = jnp.concatenate([cos, cos, pad + 1.0], -1)
    rs1 = jnp.concatenate([-sin, zero, pad], -1)
    rs2 = jnp.concatenate([zero, sin, pad], -1)
    rep = LANES // ATT_HEAD_DIM
    return tuple(jnp.tile(t, (1, rep)) for t in (rc, rs1, rs2))


def _lane_row(v, off):
    return jnp.zeros((1, LANES), F32).at[0, off:off + v.shape[0]].set(v)


def _layer(x, tables, topk, alpha, w_in, q_norm_w, w_uq, w_iq, k_idx_norm_g, k_idx_norm_b,
           conv_w, conv_b, dt_bias, a_log, d_skip, ssd_norm_w, w_attn_branch, w_ssd_branch,
           w_out, ln1_g, ln1_b, w_up, w_down, ln2_g, ln2_b):
    bsz, seq, _ = x.shape
    m = bsz * seq

    w_packed = _pack_call(w_in.T)

    qt, qit, k4, vt5, ki, tail, tailt, z, xbc, gates = _proj_call(
        x, w_packed, q_norm_w.reshape(1, Q_LORA), w_uq.astype(BF16), w_iq.astype(BF16),
        _lane_row(k_idx_norm_g, 0), _lane_row(k_idx_norm_b, 0), _lane_row(dt_bias, DT_OFF), *tables)

    att = _attn_call(ki, qit, tailt, k4, vt5, qt, topk)

    slot = jnp.arange(LANES)[:, None] - DT_OFF
    expand = (slot == jnp.arange(D_INNER)[None, :] // SSD_HEAD_DIM).astype(BF16)
    e2 = jnp.concatenate([expand, expand], axis=0)
    ssd = _ssd_call(xbc, z, tail, conv_w, conv_b.reshape(1, XBC_DIM), _lane_row(a_log, DT_OFF), e2,
                    jnp.repeat(d_skip, SSD_HEAD_DIM).reshape(1, D_INNER),
                    ssd_norm_w.reshape(1, D_INNER))

    r = lambda t: t.reshape(1, -1)
    out = _mix_call(att.reshape(m, ATT_DIM), ssd.reshape(m, D_INNER), gates.reshape(m, 2 * D_MODEL),
                    x.reshape(m, D_MODEL), w_attn_branch.astype(BF16), w_ssd_branch.astype(BF16),
                    w_out.astype(BF16), r(ln1_g), r(ln1_b), w_up.astype(BF16), w_down.astype(BF16),
                    r(ln2_g), r(ln2_b), alpha, tm=MIX_ROWS)
    return out.reshape(bsz, seq, D_MODEL)


def kernel(x, w_in, q_norm_w, w_uq, w_iq, k_idx_norm_g, k_idx_norm_b, conv_w, conv_b, dt_bias,
           a_log, d_skip, ssd_norm_w, w_attn_branch, w_ssd_branch, w_out, ln1_g, ln1_b, w_up,
           w_down, ln2_g, ln2_b):
    depth = w_in.shape[0]
    seq = x.shape[1]
    topk = min(TOPK_MAX, seq // 4)
    alpha = (2 * depth) ** 0.25
    tables = _rope_tables(seq)
    h = x
    for l in range(depth):
        h = _layer(h, tables, topk, alpha, w_in[l], q_norm_w[l], w_uq[l], w_iq[l], k_idx_norm_g[l],
                   k_idx_norm_b[l], conv_w[l], conv_b[l], dt_bias[l], a_log[l], d_skip[l],
                   ssd_norm_w[l], w_attn_branch[l], w_ssd_branch[l], w_out[l], ln1_g[l], ln1_b[l],
                   w_up[l], w_down[l], ln2_g[l], ln2_b[l])
    return h
```

```python
import functools

import jax
import jax.numpy as jnp
from jax import lax
from jax.experimental import pallas as pl
from jax.experimental.pallas import tpu as pltpu

F32 = jnp.float32
BF16 = jnp.bfloat16
I32 = jnp.int32

D_MODEL = 1024
N_ATT_HEADS = 16
ATT_HEAD_DIM = 64
N_KV_HEADS = 4
ATT_DIM = N_ATT_HEADS * ATT_HEAD_DIM
KV_DIM = N_KV_HEADS * ATT_HEAD_DIM
Q_LORA = 256
IDX_HEADS = 8
IDX_DIM = 64
TOPK_MAX = 256
ROT_DIM = ATT_HEAD_DIM // 4
ROPE_THETA = 500000.0
D_INNER = 2 * D_MODEL
SSD_HEAD_DIM = 64
SSD_HEADS = D_INNER // SSD_HEAD_DIM
SSD_GROUPS = 4
D_STATE = 128
CONV_W = 4
CHUNK = 128
BC_DIM = SSD_GROUPS * D_STATE
XBC_DIM = D_INNER + 2 * BC_DIM
D_FF = 4 * D_MODEL
IN_SIZES = (Q_LORA, KV_DIM, KV_DIM, IDX_DIM, IDX_HEADS, D_INNER, XBC_DIM, SSD_HEADS, 2 * D_MODEL)
LN_EPS = 1e-5

LANES = 128
SUBLANES = 8
VMEM_LIMIT_BYTES = 56 * 1024 * 1024

SEQ_TILE = 256
MIX_ROWS = 512
MIX_SUB_ROWS = 256

SMALL_COLS = Q_LORA + 2 * KV_DIM + LANES
TAIL_OFF = Q_LORA + 2 * KV_DIM
WI_OFF = IDX_DIM
WI_END = WI_OFF + IDX_HEADS
DT_OFF = 0
TAIL_END = DT_OFF + SSD_HEADS
WDT_OFF = SMALL_COLS
WZ_OFF = WDT_OFF + LANES
WX_OFF = WZ_OFF + D_INNER
WG_OFF = WX_OFF + XBC_DIM
PACKED_COLS = WG_OFF + 2 * D_MODEL

PERM_STRIDE = CHUNK // SUBLANES
BF16_SUBLANES = 16
V_ROWS = ATT_HEAD_DIM + BF16_SUBLANES
SUM_ROW = ATT_HEAD_DIM
SHIFT_MARGIN = 1.0 + 2.0 ** -6
MIN_SOFTMAX_SUM = 2.0 ** -40
LOG2_E = 1.4426950408889634
INT_MIN = -(2 ** 31)


def _sigmoid(v):
    return 0.5 + 0.5 * jnp.tanh(0.5 * v)


def _silu(v):
    h = 0.5 * v
    return h + h * jnp.tanh(h)


def _const_spec(shape):
    zeros = (0,) * len(shape)
    return pl.BlockSpec(shape, lambda *_: zeros, pipeline_mode=pl.Buffered(1))


PACK_TILES = 4
ROW_GROUPS = LANES // SUBLANES


def _pack_tile_source(t):
    offs = [0]
    for s in IN_SIZES:
        offs.append(offs[-1] + s)
    assert all(o % SUBLANES == 0 for o in offs)
    first = lambda seg_index, tile0: offs[seg_index] // SUBLANES + ROW_GROUPS * (t - tile0)
    return jnp.where(t < WDT_OFF // LANES, ROW_GROUPS * t,
                     jnp.where(t < WZ_OFF // LANES, first(7, WDT_OFF // LANES),
                               jnp.where(t < WX_OFF // LANES, first(5, WZ_OFF // LANES),
                                         jnp.where(t < WG_OFF // LANES, first(6, WX_OFF // LANES),
                                                   first(8, WG_OFF // LANES)))))


def _pack_kernel(*refs):
    o_ref = refs[-1]
    for i, w_ref in enumerate(refs[:-1]):
        rows = w_ref[...].reshape(LANES, w_ref.shape[-1])
        o_ref[:, i * LANES:(i + 1) * LANES] = rows.T.astype(BF16)


def _pack_call(w_t):
    cols, rows = w_t.shape
    w3 = w_t.reshape(cols // SUBLANES, SUBLANES, rows)
    tile_spec = lambda i: pl.BlockSpec(
        (pl.Element(ROW_GROUPS), pl.Element(SUBLANES), pl.Element(rows)),
        lambda j: (_pack_tile_source(PACK_TILES * j + i), 0, 0))
    return pl.pallas_call(
        _pack_kernel,
        grid=(PACKED_COLS // (PACK_TILES * LANES),),
        in_specs=[tile_spec(i) for i in range(PACK_TILES)],
        out_specs=pl.BlockSpec((rows, PACK_TILES * LANES), lambda j: (0, j)),
        out_shape=jax.ShapeDtypeStruct((rows, PACKED_COLS), BF16),
        compiler_params=pltpu.CompilerParams(dimension_semantics=("parallel",),
                                             vmem_limit_bytes=VMEM_LIMIT_BYTES),
        name="pack",
    )(*([w3] * PACK_TILES))


def _proj_kernel(x_ref, w_ref, perm_ref, qnw_ref, wuq_ref, wiq_ref,
                 kng_ref, knb_ref, dtb_ref, rc_ref, rs1_ref, rs2_ref,
                 qt_ref, qit_ref, k4_ref, vt_ref, ki_ref, tail_ref, tailt_ref,
                 z_ref, xbc_ref, g_ref):
    xb = x_ref[0].astype(BF16)
    rc, rs1, rs2 = rc_ref[...], rs1_ref[...], rs2_ref[...]
    hd = ATT_HEAD_DIM
    w_cols = lambda off, n: w_ref[:, off:off + n]

    xp = jnp.dot(perm_ref[...], xb, preferred_element_type=F32).astype(BF16)

    def rope(t):
        half = ROT_DIM // 2
        return (t * rc + pltpu.roll(t, LANES - half, 1) * rs1 + pltpu.roll(t, half, 1) * rs2)

    pa = jnp.dot(xb, w_cols(0, SMALL_COLS), preferred_element_type=F32)

    lane = lax.broadcasted_iota(I32, (xb.shape[0], LANES), 1)
    dtr_p = jnp.dot(xp, w_cols(WDT_OFF, LANES), preferred_element_type=F32) + dtb_ref[...]
    dt_p = jnp.maximum(dtr_p, 0.0) + jnp.log1p(jnp.exp(-jnp.abs(dtr_p)))
    tail_ref[0] = jnp.where((lane >= DT_OFF) & (lane < TAIL_END), dt_p, 0.0)
    z_ref[0] = jnp.dot(xp, w_cols(WZ_OFF, D_INNER), preferred_element_type=F32).astype(BF16)
    xbc_ref[0] = jnp.dot(xp, w_cols(WX_OFF, XBC_DIM), preferred_element_type=F32)

    c_q = pa[:, :Q_LORA]
    ms = jnp.mean(c_q * c_q, axis=-1, keepdims=True)
    cq = (c_q * lax.rsqrt(ms + LN_EPS) * qnw_ref[...]).astype(BF16)

    scale = ATT_HEAD_DIM ** -0.5 * LOG2_E
    qf = jnp.dot(cq, wuq_ref[...], preferred_element_type=F32)
    for j in range(ATT_DIM // LANES):
        sl = slice(j * LANES, (j + 1) * LANES)
        qt_ref[0, sl, :] = (rope(qf[:, sl]) * scale).T.astype(BF16)
    qif = jnp.dot(cq, wiq_ref[...], preferred_element_type=F32)
    iscale = IDX_DIM ** -0.5
    for j in range(IDX_HEADS * IDX_DIM // LANES):
        sl = slice(j * LANES, (j + 1) * LANES)
        qit_ref[0, sl, :] = (rope(qif[:, sl]) * iscale).T.astype(BF16)
    for j in range(KV_DIM // LANES):
        kr = rope(pa[:, Q_LORA + j * LANES:Q_LORA + (j + 1) * LANES])
        for i, kh in enumerate((kr, pltpu.roll(kr, hd, 1))):
            k4_ref[0, 2 * j + i] = jnp.where(lane < hd, kh, jnp.where(lane == hd, 1.0, 0.0)).astype(BF16)
        vtr = pa[:, Q_LORA + KV_DIM + j * LANES:Q_LORA + KV_DIM + (j + 1) * LANES].T
        sum_rows = jnp.where(lax.broadcasted_iota(I32, (V_ROWS - hd, vtr.shape[1]), 0) == 0,
                             1.0, 0.0).astype(BF16)
        vt_ref[0, 2 * j, 0] = jnp.concatenate([vtr[:hd, :].astype(BF16), sum_rows], axis=0)
        vt_ref[0, 2 * j + 1, 0] = jnp.concatenate([vtr[hd:, :].astype(BF16), sum_rows], axis=0)

    tail = pa[:, TAIL_OFF:TAIL_OFF + LANES]
    is_ki = lane < IDX_DIM
    mu = jnp.sum(jnp.where(is_ki, tail, 0.0), axis=-1, keepdims=True) * (1.0 / IDX_DIM)
    dv = jnp.where(is_ki, tail - mu, 0.0)
    var = jnp.sum(dv * dv, axis=-1, keepdims=True) * (1.0 / IDX_DIM)
    ki = rope(dv * lax.rsqrt(var + LN_EPS) * kng_ref[...] + knb_ref[...])
    wi = tail * (IDX_HEADS ** -0.5)
    ki_ref[0] = ki[:, :IDX_DIM].astype(BF16)
    tailt_ref[0] = jnp.where(is_ki, ki, jnp.where(lane < WI_END, wi, 0.0)).T

    g_ref[0] = jnp.dot(xb, w_cols(WG_OFF, 2 * D_MODEL), preferred_element_type=F32).astype(BF16)


def _proj_call(x, w_packed, qnw, wuq, wiq, kng, knb, dtb, rc, rs1, rs2):
    bsz, seq, _ = x.shape
    tm = SEQ_TILE
    nsb = seq // tm
    r = jnp.arange(tm)
    src = (r // CHUNK) * CHUNK + PERM_STRIDE * (r % SUBLANES) + (r % CHUNK) // SUBLANES
    perm = (jnp.arange(tm)[None, :] == src[:, None]).astype(BF16)
    row = lambda n: pl.BlockSpec((1, tm, n), lambda b, i: (b, i, 0))
    col = lambda n: pl.BlockSpec((1, n, tm), lambda b, i: (b, 0, i))
    tab = pl.BlockSpec((tm, LANES), lambda b, i: (i, 0))
    sds = jax.ShapeDtypeStruct
    out_shapes = (
        sds((bsz, ATT_DIM, seq), BF16),
        sds((bsz, IDX_HEADS * IDX_DIM, seq), BF16),
        sds((bsz, N_KV_HEADS, seq, LANES), BF16),
        sds((bsz, N_KV_HEADS, nsb, V_ROWS, tm), BF16),
        sds((bsz, seq, IDX_DIM), BF16),
        sds((bsz, seq, LANES), F32),
        sds((bsz, LANES, seq), F32),
        sds((bsz, seq, D_INNER), BF16),
        sds((bsz, seq, XBC_DIM), F32),
        sds((bsz, seq, 2 * D_MODEL), BF16),
    )
    out_specs = [
        col(ATT_DIM), col(IDX_HEADS * IDX_DIM),
        pl.BlockSpec((1, N_KV_HEADS, tm, LANES), lambda b, i: (b, 0, i, 0)),
        pl.BlockSpec((1, N_KV_HEADS, 1, V_ROWS, tm), lambda b, i: (b, 0, i, 0, 0)),
        row(IDX_DIM), row(LANES), col(LANES), row(D_INNER), row(XBC_DIM), row(2 * D_MODEL),
    ]
    consts = (w_packed, perm, qnw, wuq, wiq, kng, knb, dtb)
    return pl.pallas_call(
        _proj_kernel,
        grid=(bsz, nsb),
        in_specs=[row(D_MODEL)] + [_const_spec(c.shape) for c in consts] + [tab, tab, tab],
        out_specs=out_specs,
        out_shape=out_shapes,
        compiler_params=pltpu.CompilerParams(dimension_semantics=("parallel", "parallel"),
                                             vmem_limit_bytes=VMEM_LIMIT_BYTES),
        name="proj",
    )(x, *consts, rc, rs1, rs2)


COUNT_ROWS = 32


def _fold_rows(v, op, rows=32):
    while v.shape[0] > rows:
        half = v.shape[0] // 2
        v = op(v[:half], v[half:])
    return v


def _attn_kernel(ki_ref, qit_ref, wit_ref, k_ref, vt_ref, qt_ref, o_ref,
                 score_ref, coarse_ref, mask_ref, ot_ref, m_ref, acc_ref, qa_ref, kmax_ref,
                 *, topk):
    tq = qt_ref.shape[2]
    tk = tq
    qblk = pl.program_id(1)
    nch = qblk + 1
    sub = lax.broadcasted_iota(I32, (tk, tq), 0)
    tpos = qblk * tq + lax.broadcasted_iota(I32, (tk, tq), 1)

    def score_chunk(c, carry):
        off = pl.multiple_of(c * tk, tk)
        kic = ki_ref[0, pl.ds(off, tk), :]
        acc = jnp.zeros((tk, tq), F32)
        for hh in range(IDX_HEADS):
            sc = jnp.dot(kic, qit_ref[0, hh * IDX_DIM:(hh + 1) * IDX_DIM, :],
                         preferred_element_type=F32)
            acc = acc + wit_ref[0, WI_OFF + hh:WI_OFF + hh + 1, :] * jnp.maximum(sc, 0.0)
        masked = jnp.where(off + sub <= tpos, acc, -jnp.inf)
        score_ref[c] = masked
        coarse_ref[c] = masked.astype(BF16)
        return carry

    lax.fori_loop(0, nch // 2, lambda n, c: score_chunk(2 * n + 1, score_chunk(2 * n, c)), 0)

    @pl.when(nch % 2 == 1)
    def _():
        score_chunk(nch - 1, 0)

    def count(pred, ref=score_ref, dtype=I32):
        def body(c, part):
            hit = jnp.where(pred(ref[c], c), jnp.ones((), dtype), jnp.zeros((), dtype))
            return part + _fold_rows(hit, jnp.add, COUNT_ROWS)

        part = lax.fori_loop(0, nch, body, jnp.zeros((COUNT_ROWS, tq), dtype))
        return jnp.sum(part.astype(I32), axis=0, keepdims=True)

    def ordered_float(u):
        key = u ^ jnp.int32(INT_MIN)
        return pltpu.bitcast(jnp.where(key < 0, key ^ jnp.int32(0x7FFFFFFF), key), F32)

    half_bits = 16
    neg_fill = jnp.int32(2 ** half_bits - 1)

    def coarse_key(u16):
        return lax.shift_left(u16, half_bits) | jnp.where(u16 >= 2 ** (half_bits - 1), 0, neg_fill)

    def coarse_search(i, u16):
        cand = u16 | lax.shift_left(jnp.int32(1), half_bits - 1 - i)
        cand_b = ordered_float(coarse_key(cand)).astype(BF16)
        n = count(lambda sc, c: sc >= cand_b, coarse_ref, BF16)
        return jnp.where(n >= topk, cand, u16)

    u16 = lax.fori_loop(0, half_bits, coarse_search, jnp.zeros((1, tq), I32))
    base_u = coarse_key(u16) - jnp.int32(2 ** half_bits)

    def fine_search(i, d):
        cand = d | lax.shift_left(jnp.int32(1), half_bits - i)
        cand_f = ordered_float(base_u + cand)
        return jnp.where(count(lambda sc, c: sc >= cand_f) >= topk, cand, d)

    thr = ordered_float(base_u + lax.fori_loop(0, half_bits + 1, fine_search,
                                               jnp.zeros((1, tq), I32)))
    thr = jnp.where(tpos[0:1, :] < topk, -jnp.inf, thr)

    need = (topk - count(lambda sc, c: sc > thr)).astype(F32)
    tri = jnp.where(sub >= lax.broadcasted_iota(I32, (tk, tk), 1), 1.0, 0.0).astype(BF16)

    def mask_chunk(c, ties_before):
        sc = score_ref[c]
        tie = jnp.where(sc == thr, 1.0, 0.0)
        tie_rank = jnp.dot(tri, tie.astype(BF16), preferred_element_type=F32) + ties_before
        sel = jnp.where(sc > thr, 1.0, jnp.where(tie_rank <= need, tie, 0.0))
        mask_ref[c] = jnp.where(c * tk + sub <= tpos, sel, 0.0).astype(BF16)
        return tie_rank[tk - 1:tk, :]

    grp = N_ATT_HEADS // N_KV_HEADS
    hd = ATT_HEAD_DIM
    klane = lax.broadcasted_iota(I32, (k_ref.shape[2], LANES), 1)

    @pl.when(qblk == 0)
    def _():
        for g in range(N_KV_HEADS):
            kf = k_ref[0, g].astype(F32)
            norm2 = jnp.sum(jnp.where(klane < hd, kf * kf, 0.0), axis=1, keepdims=True)
            kmax_ref[g:g + 1, :] = jnp.broadcast_to(jnp.max(norm2, axis=0, keepdims=True), (1, tq))

    shift_row = lax.broadcasted_iota(I32, (LANES - hd, tq), 0) == 0
    for h in range(N_ATT_HEADS):
        qs = qt_ref[0, h * hd:(h + 1) * hd, :]
        qf = qs.astype(F32)
        r = jnp.sqrt(jnp.sum(qf * qf, axis=0, keepdims=True) * kmax_ref[h // grp:h // grp + 1, :])
        r = r * SHIFT_MARGIN
        qa_ref[h] = jnp.concatenate([qs, jnp.where(shift_row, -r, 0.0).astype(BF16)], axis=0)

    def scores_of(c, g):
        kc = k_ref[0, g, pl.ds(pl.multiple_of(c * tk, tk), tk), :]
        return [jnp.dot(kc, qa_ref[g * grp + i], preferred_element_type=F32) for i in range(grp)]

    def probs_of(c, g, scores):
        mk = mask_ref[c]
        return ([jnp.exp2(s).astype(BF16) * mk for s in scores],)

    def pv_of(c, g, probs):
        vc = vt_ref[0, g, c]
        for i in range(grp):
            h = g * grp + i
            acc_ref[h] = acc_ref[h] + jnp.dot(vc, probs[i], preferred_element_type=F32)

    def online_probs_of(c, g, scores):
        bias = jnp.where(mask_ref[c].astype(F32) > 0.0, 0.0, -jnp.inf)
        probs, alphas = [], []
        for i in range(grp):
            h = g * grp + i
            m = m_ref[h:h + 1, :]
            s = scores[i] + bias
            m_new = jnp.maximum(m, jnp.max(_fold_rows(s, jnp.maximum), axis=0, keepdims=True))
            m_safe = jnp.where(m_new == -jnp.inf, 0.0, m_new)
            alphas.append(jnp.exp2(m - m_safe))
            probs.append(jnp.exp2(s - m_safe).astype(BF16))
            m_ref[h:h + 1, :] = m_new
        return probs, alphas

    def online_pv_of(c, g, probs, alphas):
        vc = vt_ref[0, g, c]
        for i in range(grp):
            h = g * grp + i
            acc_ref[h] = alphas[i] * acc_ref[h] + jnp.dot(vc, probs[i], preferred_element_type=F32)

    def attend(chunks, probs_fn, pv_fn):
        units = [(c, g) for c in chunks for g in range(N_KV_HEADS)]
        scores = scores_of(*units[0])
        pending = None
        for n, unit in enumerate(units):
            nxt = scores_of(*units[n + 1]) if n + 1 < len(units) else None
            cur = probs_fn(*unit, scores)
            if pending is not None:
                pv_fn(*units[n - 1], *pending)
            scores, pending = nxt, cur
        pv_fn(*units[-1], *pending)

    acc_ref[...] = jnp.zeros(acc_ref.shape, F32)

    def chunk_pair(n, ties):
        ties = mask_chunk(2 * n + 1, mask_chunk(2 * n, ties))
        attend([2 * n, 2 * n + 1], probs_of, pv_of)
        return ties

    ties = lax.fori_loop(0, nch // 2, chunk_pair, jnp.zeros((1, tq), F32))

    @pl.when(nch % 2 == 1)
    def _():
        mask_chunk(nch - 1, ties)
        attend([nch - 1], probs_of, pv_of)

    sums = jnp.concatenate([acc_ref[h, SUM_ROW:SUM_ROW + 1, :] for h in range(N_ATT_HEADS)], axis=0)
    underflow = jnp.where(sums >= MIN_SOFTMAX_SUM, 0, 1)

    @pl.when(jnp.max(underflow) > 0)
    def _():
        m_ref[...] = jnp.full(m_ref.shape, -jnp.inf, F32)
        acc_ref[...] = jnp.zeros(acc_ref.shape, F32)

        def chunk_online(c, carry):
            attend([c], online_probs_of, online_pv_of)
            return carry

        lax.fori_loop(0, nch, chunk_online, 0)

    for h in range(N_ATT_HEADS):
        ot_ref[h * hd:(h + 1) * hd, :] = acc_ref[h, :hd, :] / acc_ref[h, SUM_ROW:SUM_ROW + 1, :]
    o_ref[0] = ot_ref[...].T.astype(BF16)


def _attn_call(ki, qit, tailt, k4, vt5, qt, topk):
    bsz, seq, _ = ki.shape
    tq = vt5.shape[-1]
    nch = seq // tq
    assert tq >= topk and vt5.shape[2] == nch
    assert seq // COUNT_ROWS <= 256
    return pl.pallas_call(
        functools.partial(_attn_kernel, topk=topk),
        grid=(bsz, nch),
        in_specs=[
            pl.BlockSpec((1, seq, IDX_DIM), lambda b, j: (b, 0, 0)),
            pl.BlockSpec((1, IDX_HEADS * IDX_DIM, tq), lambda b, j: (b, 0, j)),
            pl.BlockSpec((1, LANES, tq), lambda b, j: (b, 0, j)),
            pl.BlockSpec((1, N_KV_HEADS, seq, LANES), lambda b, j: (b, 0, 0, 0)),
            pl.BlockSpec((1, N_KV_HEADS, nch, V_ROWS, tq), lambda b, j: (b, 0, 0, 0, 0)),
            pl.BlockSpec((1, ATT_DIM, tq), lambda b, j: (b, 0, j)),
        ],
        out_specs=pl.BlockSpec((1, tq, ATT_DIM), lambda b, j: (b, j, 0)),
        out_shape=jax.ShapeDtypeStruct((bsz, seq, ATT_DIM), BF16),
        scratch_shapes=[pltpu.VMEM((nch, tq, tq), F32), pltpu.VMEM((nch, tq, tq), BF16),
                        pltpu.VMEM((nch, tq, tq), BF16),
                        pltpu.VMEM((ATT_DIM, tq), F32),
                        pltpu.VMEM((N_ATT_HEADS, tq), F32),
                        pltpu.VMEM((N_ATT_HEADS, V_ROWS, tq), F32),
                        pltpu.VMEM((N_ATT_HEADS, LANES, tq), BF16),
                        pltpu.VMEM((N_KV_HEADS, tq), F32)],
        compiler_params=pltpu.CompilerParams(dimension_semantics=("parallel", "arbitrary"),
                                             vmem_limit_bytes=VMEM_LIMIT_BYTES),
        name="attn",
    )(ki, qit, tailt, k4, vt5, qt)


def _split2(v):
    hi = v.astype(BF16)
    mid = (v - hi.astype(F32)).astype(BF16)
    return jnp.concatenate([hi, mid], axis=1)


def _ssd_kernel(xbc_ref, z_ref, tail_ref, cw_ref, cb_ref, alog_ref, e2_ref,
                dx_ref, nw_ref, out_ref, carry, state, ybuf):
    L = CHUNK
    hpg = SSD_HEADS // SSD_GROUPS
    gw = hpg * SSD_HEAD_DIM

    nshift = CONV_W - 1
    carry_rows = nshift * SUBLANES

    @pl.when(pl.program_id(1) == 0)
    def _():
        carry[...] = jnp.zeros_like(carry)
        state[...] = jnp.zeros_like(state)

    u = xbc_ref[0]
    sub8 = lax.broadcasted_iota(I32, (SUBLANES, XBC_DIM), 0)
    wrapped = []
    for k in range(nshift):
        rows = slice(L - carry_rows + k * SUBLANES, L - carry_rows + (k + 1) * SUBLANES)
        merged = jnp.where(sub8 == SUBLANES - 1, carry[k * SUBLANES:(k + 1) * SUBLANES, :], u[rows, :])
        wrapped.append(pltpu.roll(merged, 1, 0))
    carry[...] = u[L - carry_rows:, :]
    conv = cb_ref[...] + cw_ref[CONV_W - 1:CONV_W, :] * u
    for d in range(1, CONV_W):
        shifted = jnp.concatenate(wrapped[nshift - d:] + [u[:L - d * SUBLANES, :]], axis=0)
        conv = conv + cw_ref[CONV_W - 1 - d:CONV_W - d, :] * shifted
    act = _silu(conv)
    xs = act[:, :D_INNER]

    row = lax.broadcasted_iota(I32, (L, L), 0)
    col = lax.broadcasted_iota(I32, (L, L), 1)
    time_of = lambda r: PERM_STRIDE * (r & (SUBLANES - 1)) + (r >> (SUBLANES.bit_length() - 1))
    lower = time_of(row) >= time_of(col)
    tail = tail_ref[0]
    adt = tail * (-jnp.exp(alog_ref[...]) * LOG2_E)
    hi = adt.astype(BF16)
    r1 = adt - hi.astype(F32)
    mid = r1.astype(BF16)
    lo = (r1 - mid.astype(F32)).astype(BF16)
    ones_l = jnp.where(lower, 1.0, 0.0).astype(BF16)
    a_cum = jnp.dot(jnp.concatenate([ones_l, ones_l, ones_l], axis=1),
                    jnp.concatenate([hi, mid, lo], axis=0), preferred_element_type=F32)
    a_cum_t = a_cum.T
    e2 = e2_ref[...]
    a_cum_x = jnp.dot(_split2(a_cum), e2, preferred_element_type=F32)
    dt_x = jnp.dot(_split2(tail), e2, preferred_element_type=F32)
    a_last_x = a_cum_x[L - 1:L, :]
    xf = xs * dt_x
    xf_b = xf.astype(BF16)
    xd_b = (xf * jnp.exp2(a_last_x - a_cum_x)).astype(BF16)
    ea_x = jnp.exp2(a_cum_x)
    chunk_decay_x = jnp.exp2(a_last_x)

    lane = lax.broadcasted_iota(I32, (L, LANES), 1)
    for g in range(SSD_GROUPS):
        b_f = act[:, D_INNER + g * D_STATE:D_INNER + (g + 1) * D_STATE]
        bg = b_f.astype(BF16)
        bg_t = b_f.T.astype(BF16)
        cg = act[:, D_INNER + BC_DIM + g * D_STATE:D_INNER + BC_DIM + (g + 1) * D_STATE].astype(BF16)
        gs = slice(g * gw, (g + 1) * gw)
        cb = lax.dot_general(cg, bg, (((1,), (1,)), ((), ())), preferred_element_type=F32)
        st_prev = state[:, gs]
        y_off = jnp.dot(cg, st_prev.astype(BF16), preferred_element_type=F32) * ea_x[:, gs]
        st_new = jnp.dot(bg_t, xd_b[:, gs], preferred_element_type=F32)
        state[:, gs] = st_prev * chunk_decay_x[:, gs] + st_new
        for pr in range(hpg // 2):
            h0 = g * hpg + 2 * pr
            xs_pair = xf_b[:, h0 * SSD_HEAD_DIM:(h0 + 2) * SSD_HEAD_DIM]
            ys = []
            for h in (h0, h0 + 1):
                seg = a_cum[:, DT_OFF + h:DT_OFF + h + 1] - a_cum_t[DT_OFF + h:DT_OFF + h + 1, :]
                lmat = jnp.exp2(jnp.where(lower, seg, -jnp.inf))
                ys.append(jnp.dot((cb * lmat).astype(BF16), xs_pair, preferred_element_type=F32))
            ybuf[h0 // 2] = (jnp.where(lane < SSD_HEAD_DIM, ys[0], ys[1])
                             + y_off[:, 2 * pr * SSD_HEAD_DIM:(2 * pr + 2) * SSD_HEAD_DIM])

    ntile = D_INNER // LANES
    y = jnp.concatenate([ybuf[j] for j in range(ntile)], axis=1) + dx_ref[...] * xs
    zf = z_ref[0].astype(F32)
    yz = y * _silu(zf)
    for g in range(SSD_GROUPS):
        gs = slice(g * gw, (g + 1) * gw)
        yg = yz[:, gs]
        msq = jnp.mean(yg * yg, axis=-1, keepdims=True)
        og = yg * lax.rsqrt(msq + LN_EPS) * nw_ref[:, gs]
        for j in range(gw // LANES):
            ybuf[g * (gw // LANES) + j] = og[:, j * LANES:(j + 1) * LANES]
    half = L // 2
    out_ref[0] = jnp.concatenate(
        [jnp.concatenate([ybuf[j, pl.ds(half * (t % 2) + t // 2, SUBLANES, stride=SUBLANES), :]
                          for j in range(ntile)], axis=1)
         for t in range(PERM_STRIDE)], axis=0).astype(BF16)


def _ssd_call(xbc, z, tail, cw, cb, alog, e2, dx, nw):
    bsz, seq, _ = xbc.shape
    L = CHUNK
    blk = lambda n: pl.BlockSpec((1, L, n), lambda b, c: (b, c, 0))
    consts = (cw, cb, alog, e2, dx, nw)
    return pl.pallas_call(
        _ssd_kernel,
        grid=(bsz, seq // L),
        in_specs=[blk(XBC_DIM), blk(D_INNER), blk(LANES)] + [_const_spec(c.shape) for c in consts],
        out_specs=blk(D_INNER),
        out_shape=jax.ShapeDtypeStruct((bsz, seq, D_INNER), BF16),
        scratch_shapes=[pltpu.VMEM(((CONV_W - 1) * SUBLANES, XBC_DIM), F32),
                        pltpu.VMEM((D_STATE, D_INNER), F32),
                        pltpu.VMEM((D_INNER // LANES, L, LANES), F32)],
        compiler_params=pltpu.CompilerParams(dimension_semantics=("parallel", "arbitrary"),
                                             vmem_limit_bytes=VMEM_LIMIT_BYTES),
        name="ssd",
    )(xbc, z, tail, *consts)


def _layer_norm(v, g, b):
    mu = jnp.mean(v, axis=-1, keepdims=True)
    d = v - mu
    var = jnp.mean(d * d, axis=-1, keepdims=True)
    return d * lax.rsqrt(var + LN_EPS) * g + b


def _mix_kernel(att_ref, ssd_ref, g_ref, x_ref, wa_ref, ws_ref, wo_ref, l1g_ref, l1b_ref,
                wu_ref, wd_ref, l2g_ref, l2b_ref, o_ref, *, alpha, ff_chunk):
    tm = x_ref.shape[0]
    subs = [slice(s * MIX_SUB_ROWS, (s + 1) * MIX_SUB_ROWS) for s in range(tm // MIX_SUB_ROWS)]
    mixed = []
    for r in subs:
        ga = g_ref[r, :D_MODEL].astype(F32)
        gs = g_ref[r, D_MODEL:].astype(F32)
        mixed.append(
            _sigmoid(ga) * jnp.dot(att_ref[r, :], wa_ref[...], preferred_element_type=F32)
            + _sigmoid(gs) * jnp.dot(ssd_ref[r, :], ws_ref[...], preferred_element_type=F32))
    hs = []
    for r, mx in zip(subs, mixed):
        proj = jnp.dot(mx.astype(BF16), wo_ref[...], preferred_element_type=F32)
        hs.append(_layer_norm(alpha * x_ref[r, :] + proj, l1g_ref[...], l1b_ref[...]))
    hbs = [h.astype(BF16) for h in hs]
    ffs = [jnp.zeros_like(h) for h in hs]
    for c in range(D_FF // ff_chunk):
        cs = slice(c * ff_chunk, (c + 1) * ff_chunk)
        for s in range(len(subs)):
            u = jnp.maximum(jnp.dot(hbs[s], wu_ref[:, cs], preferred_element_type=F32), 0.0)
            ffs[s] = ffs[s] + jnp.dot((u * u).astype(BF16), wd_ref[cs, :],
                                      preferred_element_type=F32)
    for r, h, ff in zip(subs, hs, ffs):
        o_ref[r, :] = _layer_norm(alpha * h + ff, l2g_ref[...], l2b_ref[...])


def _mix_call(att, ssd, gates, x2, wa, ws, wo, l1g, l1b, wu, wd, l2g, l2b, alpha, tm):
    m = x2.shape[0]
    row = lambda n: pl.BlockSpec((tm, n), lambda i: (i, 0))
    consts = (wa, ws, wo, l1g, l1b, wu, wd, l2g, l2b)
    return pl.pallas_call(
        functools.partial(_mix_kernel, alpha=alpha, ff_chunk=1024),
        grid=(m // tm,),
        in_specs=[row(ATT_DIM), row(D_INNER), row(2 * D_MODEL), row(D_MODEL)]
                 + [_const_spec(c.shape) for c in consts],
        out_specs=row(D_MODEL),
        out_shape=jax.ShapeDtypeStruct((m, D_MODEL), F32),
        compiler_params=pltpu.CompilerParams(dimension_semantics=("parallel",),
                                             vmem_limit_bytes=VMEM_LIMIT_BYTES),
        name="mix",
    )(att, ssd, gates, x2, *consts)


def _rope_tables(seq):
    half = ROT_DIM // 2
    inv = ROPE_THETA ** (-jnp.arange(0, ROT_DIM, 2, dtype=F32) / ROT_DIM)
    ang = jnp.arange(seq, dtype=F32)[:, None] * inv[None, :]
    cos, sin = jnp.cos(ang), jnp.sin(ang)
    pad = jnp.zeros((seq, ATT_HEAD_DIM - ROT_DIM), F32)
    zero = jnp.zeros((seq, half), F32)
    rc = jnp.concatenate([cos, cos, pad + 1.0], -1)
    rs1 = jnp.concatenate([-sin, zero, pad], -1)
    rs2 = jnp.concatenate([zero, sin, pad], -1)
    rep = LANES // ATT_HEAD_DIM
    return tuple(jnp.tile(t, (1, rep)) for t in (rc, rs1, rs2))


def _lane_row(v, off):
    return jnp.zeros((1, LANES), F32).at[0, off:off + v.shape[0]].set(v)


def _layer(x, tables, topk, alpha, w_in, q_norm_w, w_uq, w_iq, k_idx_norm_g, k_idx_norm_b,
           conv_w, conv_b, dt_bias, a_log, d_skip, ssd_norm_w, w_attn_branch, w_ssd_branch,
           w_out, ln1_g, ln1_b, w_up, w_down, ln2_g, ln2_b):
    bsz, seq, _ = x.shape
    m = bsz * seq

    w_packed = _pack_call(w_in.T)

    qt, qit, k4, vt5, ki, tail, tailt, z, xbc, gates = _proj_call(
        x, w_packed, q_norm_w.reshape(1, Q_LORA), w_uq.astype(BF16), w_iq.astype(BF16),
        _lane_row(k_idx_norm_g, 0), _lane_row(k_idx_norm_b, 0), _lane_row(dt_bias, DT_OFF), *tables)

    att = _attn_call(ki, qit, tailt, k4, vt5, qt, topk)

    slot = jnp.arange(LANES)[:, None] - DT_OFF
    expand = (slot == jnp.arange(D_INNER)[None, :] // SSD_HEAD_DIM).astype(BF16)
    e2 = jnp.concatenate([expand, expand], axis=0)
    ssd = _ssd_call(xbc, z, tail, conv_w, conv_b.reshape(1, XBC_DIM), _lane_row(a_log, DT_OFF), e2,
                    jnp.repeat(d_skip, SSD_HEAD_DIM).reshape(1, D_INNER),
                    ssd_norm_w.reshape(1, D_INNER))

    r = lambda t: t.reshape(1, -1)
    out = _mix_call(att.reshape(m, ATT_DIM), ssd.reshape(m, D_INNER), gates.reshape(m, 2 * D_MODEL),
                    x.reshape(m, D_MODEL), w_attn_branch.astype(BF16), w_ssd_branch.astype(BF16),
                    w_out.astype(BF16), r(ln1_g), r(ln1_b), w_up.astype(BF16), w_down.astype(BF16),
                    r(ln2_g), r(ln2_b), alpha, tm=MIX_ROWS)
    return out.reshape(bsz, seq, D_MODEL)


def kernel(x, w_in, q_norm_w, w_uq, w_iq, k_idx_norm_g, k_idx_norm_b, conv_w, conv_b, dt_bias,
           a_log, d_skip, ssd_norm_w, w_attn_branch, w_ssd_branch, w_out, ln1_g, ln1_b, w_up,
           w_down, ln2_g, ln2_b):
    depth = w_in.shape[0]
    seq = x.shape[1]
    topk = min(TOPK_MAX, seq // 4)
    alpha = (2 * depth) ** 0.25
    tables = _rope_tables(seq)
    h = x
    for l in range(depth):
        h = _layer(h, tables, topk, alpha, w_in[l], q_norm_w[l], w_uq[l], w_iq[l], k_idx_norm_g[l],
                   k_idx_norm_b[l], conv_w[l], conv_b[l], dt_bias[l], a_log[l], d_skip[l],
                   ssd_norm_w[l], w_attn_branch[l], w_ssd_branch[l], w_out[l], ln1_g[l], ln1_b[l],
                   w_up[l], w_down[l], ln2_g[l], ln2_b[l])
    return h
```

```python
import functools

import jax
import jax.numpy as jnp
from jax import lax
from jax.experimental import pallas as pl
from jax.experimental.pallas import tpu as pltpu

F32 = jnp.float32
BF16 = jnp.bfloat16
I32 = jnp.int32

D_MODEL = 1024
N_ATT_HEADS = 16
ATT_HEAD_DIM = 64
N_KV_HEADS = 4
ATT_DIM = N_ATT_HEADS * ATT_HEAD_DIM
KV_DIM = N_KV_HEADS * ATT_HEAD_DIM
Q_LORA = 256
IDX_HEADS = 8
IDX_DIM = 64
TOPK_MAX = 256
ROT_DIM = ATT_HEAD_DIM // 4
ROPE_THETA = 500000.0
D_INNER = 2 * D_MODEL
SSD_HEAD_DIM = 64
SSD_HEADS = D_INNER // SSD_HEAD_DIM
SSD_GROUPS = 4
D_STATE = 128
CONV_W = 4
CHUNK = 128
BC_DIM = SSD_GROUPS * D_STATE
XBC_DIM = D_INNER + 2 * BC_DIM
D_FF = 4 * D_MODEL
IN_SIZES = (Q_LORA, KV_DIM, KV_DIM, IDX_DIM, IDX_HEADS, D_INNER, XBC_DIM, SSD_HEADS, 2 * D_MODEL)
LN_EPS = 1e-5

LANES = 128
SUBLANES = 8
VMEM_LIMIT_BYTES = 56 * 1024 * 1024

SEQ_TILE = 256
MIX_ROWS = 512
MIX_SUB_ROWS = 256

SMALL_COLS = Q_LORA + 2 * KV_DIM + LANES
TAIL_OFF = Q_LORA + 2 * KV_DIM
WI_OFF = IDX_DIM
WI_END = WI_OFF + IDX_HEADS
DT_OFF = 0
TAIL_END = DT_OFF + SSD_HEADS
WDT_OFF = SMALL_COLS
WZ_OFF = WDT_OFF + LANES
WX_OFF = WZ_OFF + D_INNER
WG_OFF = WX_OFF + XBC_DIM
PACKED_COLS = WG_OFF + 2 * D_MODEL

PERM_STRIDE = CHUNK // SUBLANES
BF16_SUBLANES = 16
V_ROWS = ATT_HEAD_DIM + BF16_SUBLANES
SUM_ROW = ATT_HEAD_DIM
SHIFT_MARGIN = 1.0 + 2.0 ** -6
MIN_SOFTMAX_SUM = 2.0 ** -40
LOG2_E = 1.4426950408889634
INT_MIN = -(2 ** 31)


def _sigmoid(v):
    return 0.5 + 0.5 * jnp.tanh(0.5 * v)


def _silu(v):
    h = 0.5 * v
    return h + h * jnp.tanh(h)


def _const_spec(shape):
    zeros = (0,) * len(shape)
    return pl.BlockSpec(shape, lambda *_: zeros, pipeline_mode=pl.Buffered(1))


PACK_TILES = 4
ROW_GROUPS = LANES // SUBLANES


def _pack_tile_source(t):
    offs = [0]
    for s in IN_SIZES:
        offs.append(offs[-1] + s)
    assert all(o % SUBLANES == 0 for o in offs)
    first = lambda seg_index, tile0: offs[seg_index] // SUBLANES + ROW_GROUPS * (t - tile0)
    return jnp.where(t < WDT_OFF // LANES, ROW_GROUPS * t,
                     jnp.where(t < WZ_OFF // LANES, first(7, WDT_OFF // LANES),
                               jnp.where(t < WX_OFF // LANES, first(5, WZ_OFF // LANES),
                                         jnp.where(t < WG_OFF // LANES, first(6, WX_OFF // LANES),
                                                   first(8, WG_OFF // LANES)))))


def _pack_kernel(*refs):
    o_ref = refs[-1]
    for i, w_ref in enumerate(refs[:-1]):
        rows = w_ref[...].reshape(LANES, w_ref.shape[-1])
        o_ref[:, i * LANES:(i + 1) * LANES] = rows.T.astype(BF16)


def _pack_call(w_t):
    cols, rows = w_t.shape
    w3 = w_t.reshape(cols // SUBLANES, SUBLANES, rows)
    tile_spec = lambda i: pl.BlockSpec(
        (pl.Element(ROW_GROUPS), pl.Element(SUBLANES), pl.Element(rows)),
        lambda j: (_pack_tile_source(PACK_TILES * j + i), 0, 0))
    return pl.pallas_call(
        _pack_kernel,
        grid=(PACKED_COLS // (PACK_TILES * LANES),),
        in_specs=[tile_spec(i) for i in range(PACK_TILES)],
        out_specs=pl.BlockSpec((rows, PACK_TILES * LANES), lambda j: (0, j)),
        out_shape=jax.ShapeDtypeStruct((rows, PACKED_COLS), BF16),
        compiler_params=pltpu.CompilerParams(dimension_semantics=("parallel",),
                                             vmem_limit_bytes=VMEM_LIMIT_BYTES),
        name="pack",
    )(*([w3] * PACK_TILES))


def _proj_kernel(x_ref, w_ref, perm_ref, qnw_ref, wuq_ref, wiq_ref,
                 kng_ref, knb_ref, dtb_ref, rc_ref, rs1_ref, rs2_ref,
                 qt_ref, qit_ref, k4_ref, vt_ref, ki_ref, tail_ref, tailt_ref,
                 z_ref, xbc_ref, g_ref):
    xb = x_ref[0].astype(BF16)
    rc, rs1, rs2 = rc_ref[...], rs1_ref[...], rs2_ref[...]
    hd = ATT_HEAD_DIM
    w_cols = lambda off, n: w_ref[:, off:off + n]

    xp = jnp.dot(perm_ref[...], xb, preferred_element_type=F32).astype(BF16)

    def rope(t):
        half = ROT_DIM // 2
        return (t * rc + pltpu.roll(t, LANES - half, 1) * rs1 + pltpu.roll(t, half, 1) * rs2)

    pa = jnp.dot(xb, w_cols(0, SMALL_COLS), preferred_element_type=F32)

    lane = lax.broadcasted_iota(I32, (xb.shape[0], LANES), 1)
    dtr_p = jnp.dot(xp, w_cols(WDT_OFF, LANES), preferred_element_type=F32) + dtb_ref[...]
    dt_p = jnp.maximum(dtr_p, 0.0) + jnp.log1p(jnp.exp(-jnp.abs(dtr_p)))
    tail_ref[0] = jnp.where((lane >= DT_OFF) & (lane < TAIL_END), dt_p, 0.0)
    z_ref[0] = jnp.dot(xp, w_cols(WZ_OFF, D_INNER), preferred_element_type=F32).astype(BF16)
    xbc_ref[0] = jnp.dot(xp, w_cols(WX_OFF, XBC_DIM), preferred_element_type=F32)

    c_q = pa[:, :Q_LORA]
    ms = jnp.mean(c_q * c_q, axis=-1, keepdims=True)
    cq = (c_q * lax.rsqrt(ms + LN_EPS) * qnw_ref[...]).astype(BF16)

    scale = ATT_HEAD_DIM ** -0.5 * LOG2_E
    qf = jnp.dot(cq, wuq_ref[...], preferred_element_type=F32)
    for j in range(ATT_DIM // LANES):
        sl = slice(j * LANES, (j + 1) * LANES)
        qt_ref[0, sl, :] = (rope(qf[:, sl]) * scale).T.astype(BF16)
    qif = jnp.dot(cq, wiq_ref[...], preferred_element_type=F32)
    iscale = IDX_DIM ** -0.5
    for j in range(IDX_HEADS * IDX_DIM // LANES):
        sl = slice(j * LANES, (j + 1) * LANES)
        qit_ref[0, sl, :] = (rope(qif[:, sl]) * iscale).T.astype(BF16)
    for j in range(KV_DIM // LANES):
        kr = rope(pa[:, Q_LORA + j * LANES:Q_LORA + (j + 1) * LANES])
        for i, kh in enumerate((kr, pltpu.roll(kr, hd, 1))):
            k4_ref[0, 2 * j + i] = jnp.where(lane < hd, kh, jnp.where(lane == hd, 1.0, 0.0)).astype(BF16)
        vtr = pa[:, Q_LORA + KV_DIM + j * LANES:Q_LORA + KV_DIM + (j + 1) * LANES].T
        sum_rows = jnp.where(lax.broadcasted_iota(I32, (V_ROWS - hd, vtr.shape[1]), 0) == 0,
                             1.0, 0.0).astype(BF16)
        vt_ref[0, 2 * j, 0] = jnp.concatenate([vtr[:hd, :].astype(BF16), sum_rows], axis=0)
        vt_ref[0, 2 * j + 1, 0] = jnp.concatenate([vtr[hd:, :].astype(BF16), sum_rows], axis=0)

    tail = pa[:, TAIL_OFF:TAIL_OFF + LANES]
    is_ki = lane < IDX_DIM
    mu = jnp.sum(jnp.where(is_ki, tail, 0.0), axis=-1, keepdims=True) * (1.0 / IDX_DIM)
    dv = jnp.where(is_ki, tail - mu, 0.0)
    var = jnp.sum(dv * dv, axis=-1, keepdims=True) * (1.0 / IDX_DIM)
    ki = rope(dv * lax.rsqrt(var + LN_EPS) * kng_ref[...] + knb_ref[...])
    wi = tail * (IDX_HEADS ** -0.5)
    ki_ref[0] = ki[:, :IDX_DIM].astype(BF16)
    tailt_ref[0] = jnp.where(is_ki, ki, jnp.where(lane < WI_END, wi, 0.0)).T

    g_ref[0] = jnp.dot(xb, w_cols(WG_OFF, 2 * D_MODEL), preferred_element_type=F32).astype(BF16)


def _proj_call(x, w_packed, qnw, wuq, wiq, kng, knb, dtb, rc, rs1, rs2):
    bsz, seq, _ = x.shape
    tm = SEQ_TILE
    nsb = seq // tm
    r = jnp.arange(tm)
    src = (r // CHUNK) * CHUNK + PERM_STRIDE * (r % SUBLANES) + (r % CHUNK) // SUBLANES
    perm = (jnp.arange(tm)[None, :] == src[:, None]).astype(BF16)
    row = lambda n: pl.BlockSpec((1, tm, n), lambda b, i: (b, i, 0))
    col = lambda n: pl.BlockSpec((1, n, tm), lambda b, i: (b, 0, i))
    tab = pl.BlockSpec((tm, LANES), lambda b, i: (i, 0))
    sds = jax.ShapeDtypeStruct
    out_shapes = (
        sds((bsz, ATT_DIM, seq), BF16),
        sds((bsz, IDX_HEADS * IDX_DIM, seq), BF16),
        sds((bsz, N_KV_HEADS, seq, LANES), BF16),
        sds((bsz, N_KV_HEADS, nsb, V_ROWS, tm), BF16),
        sds((bsz, seq, IDX_DIM), BF16),
        sds((bsz, seq, LANES), F32),
        sds((bsz, LANES, seq), F32),
        sds((bsz, seq, D_INNER), BF16),
        sds((bsz, seq, XBC_DIM), F32),
        sds((bsz, seq, 2 * D_MODEL), BF16),
    )
    out_specs = [
        col(ATT_DIM), col(IDX_HEADS * IDX_DIM),
        pl.BlockSpec((1, N_KV_HEADS, tm, LANES), lambda b, i: (b, 0, i, 0)),
        pl.BlockSpec((1, N_KV_HEADS, 1, V_ROWS, tm), lambda b, i: (b, 0, i, 0, 0)),
        row(IDX_DIM), row(LANES), col(LANES), row(D_INNER), row(XBC_DIM), row(2 * D_MODEL),
    ]
    consts = (w_packed, perm, qnw, wuq, wiq, kng, knb, dtb)
    return pl.pallas_call(
        _proj_kernel,
        grid=(bsz, nsb),
        in_specs=[row(D_MODEL)] + [_const_spec(c.shape) for c in consts] + [tab, tab, tab],
        out_specs=out_specs,
        out_shape=out_shapes,
        compiler_params=pltpu.CompilerParams(dimension_semantics=("parallel", "parallel"),
                                             vmem_limit_bytes=VMEM_LIMIT_BYTES),
        name="proj",
    )(x, *consts, rc, rs1, rs2)


COUNT_ROWS = 32


def _fold_rows(v, op, rows=32):
    while v.shape[0] > rows:
        half = v.shape[0] // 2
        v = op(v[:half], v[half:])
    return v


def _attn_kernel(ki_ref, qit_ref, wit_ref, k_ref, vt_ref, qt_ref, o_ref,
                 score_ref, coarse_ref, mask_ref, ot_ref, m_ref, acc_ref, qa_ref, kmax_ref, ties_ref,
                 *, topk):
    tq = qt_ref.shape[2]
    tk = tq
    qblk = pl.program_id(1)
    nch = qblk + 1
    sub = lax.broadcasted_iota(I32, (tk, tq), 0)
    tpos = qblk * tq + lax.broadcasted_iota(I32, (tk, tq), 1)

    def score_chunk(c, carry):
        off = pl.multiple_of(c * tk, tk)
        kic = ki_ref[0, pl.ds(off, tk), :]
        acc = jnp.zeros((tk, tq), F32)
        for hh in range(IDX_HEADS):
            sc = jnp.dot(kic, qit_ref[0, hh * IDX_DIM:(hh + 1) * IDX_DIM, :],
                         preferred_element_type=F32)
            acc = acc + wit_ref[0, WI_OFF + hh:WI_OFF + hh + 1, :] * jnp.maximum(sc, 0.0)
        masked = jnp.where(off + sub <= tpos, acc, -jnp.inf)
        score_ref[c] = masked
        coarse_ref[c] = masked.astype(BF16)
        return carry

    lax.fori_loop(0, nch // 2, lambda n, c: score_chunk(2 * n + 1, score_chunk(2 * n, c)), 0)

    @pl.when(nch % 2 == 1)
    def _():
        score_chunk(nch - 1, 0)

    def count(pred, ref=score_ref, dtype=I32):
        def body(c, part):
            hit = jnp.where(pred(ref[c], c), jnp.ones((), dtype), jnp.zeros((), dtype))
            return part + _fold_rows(hit, jnp.add, COUNT_ROWS)

        part = lax.fori_loop(0, nch, body, jnp.zeros((COUNT_ROWS, tq), dtype))
        return jnp.sum(part.astype(I32), axis=0, keepdims=True)

    def ordered_float(u):
        key = u ^ jnp.int32(INT_MIN)
        return pltpu.bitcast(jnp.where(key < 0, key ^ jnp.int32(0x7FFFFFFF), key), F32)

    half_bits = 16
    neg_fill = jnp.int32(2 ** half_bits - 1)

    def coarse_key(u16):
        return lax.shift_left(u16, half_bits) | jnp.where(u16 >= 2 ** (half_bits - 1), 0, neg_fill)

    def coarse_search(i, u16):
        cand = u16 | lax.shift_left(jnp.int32(1), half_bits - 1 - i)
        cand_b = ordered_float(coarse_key(cand)).astype(BF16)
        n = count(lambda sc, c: sc >= cand_b, coarse_ref, BF16)
        return jnp.where(n >= topk, cand, u16)

    u16 = lax.fori_loop(0, half_bits, coarse_search, jnp.zeros((1, tq), I32))
    base_u = coarse_key(u16) - jnp.int32(2 ** half_bits)

    def fine_search(i, d):
        cand = d | lax.shift_left(jnp.int32(1), half_bits - i)
        cand_f = ordered_float(base_u + cand)
        return jnp.where(count(lambda sc, c: sc >= cand_f) >= topk, cand, d)

    thr = ordered_float(base_u + lax.fori_loop(0, half_bits + 1, fine_search,
                                               jnp.zeros((1, tq), I32)))
    thr = jnp.where(tpos[0:1, :] < topk, -jnp.inf, thr)

    need = (topk - count(lambda sc, c: sc > thr)).astype(F32)
    tri = jnp.where(sub >= lax.broadcasted_iota(I32, (tk, tk), 1), 1.0, 0.0).astype(BF16)

    ties_ref[...] = jnp.zeros((1, tq), F32)

    def mask_chunk(c):
        sc = score_ref[c]
        tie = jnp.where(sc == thr, 1.0, 0.0)
        tie_rank = jnp.dot(tri, tie.astype(BF16), preferred_element_type=F32) + ties_ref[...]
        sel = jnp.where(sc > thr, 1.0, jnp.where(tie_rank <= need, tie, 0.0))
        mask_ref[c] = jnp.where(c * tk + sub <= tpos, sel, 0.0).astype(BF16)
        ties_ref[...] = tie_rank[tk - 1:tk, :]

    grp = N_ATT_HEADS // N_KV_HEADS
    hd = ATT_HEAD_DIM
    klane = lax.broadcasted_iota(I32, (k_ref.shape[2], LANES), 1)

    @pl.when(qblk == 0)
    def _():
        for g in range(N_KV_HEADS):
            kf = k_ref[0, g].astype(F32)
            norm2 = jnp.sum(jnp.where(klane < hd, kf * kf, 0.0), axis=1, keepdims=True)
            kmax_ref[g:g + 1, :] = jnp.broadcast_to(jnp.max(norm2, axis=0, keepdims=True), (1, tq))

    shift_row = lax.broadcasted_iota(I32, (LANES - hd, tq), 0) == 0
    for h in range(N_ATT_HEADS):
        qs = qt_ref[0, h * hd:(h + 1) * hd, :]
        qf = qs.astype(F32)
        r = jnp.sqrt(jnp.sum(qf * qf, axis=0, keepdims=True) * kmax_ref[h // grp:h // grp + 1, :])
        r = r * SHIFT_MARGIN
        qa_ref[h] = jnp.concatenate([qs, jnp.where(shift_row, -r, 0.0).astype(BF16)], axis=0)

    def scores_of(c, g):
        kc = k_ref[0, g, pl.ds(pl.multiple_of(c * tk, tk), tk), :]
        return [jnp.dot(kc, qa_ref[g * grp + i], preferred_element_type=F32) for i in range(grp)]

    def probs_of(c, g, scores):
        mk = mask_ref[c]
        return ([jnp.exp2(s).astype(BF16) * mk for s in scores],)

    def pv_of(c, g, probs):
        vc = vt_ref[0, g, c]
        for i in range(grp):
            h = g * grp + i
            acc_ref[h] = acc_ref[h] + jnp.dot(vc, probs[i], preferred_element_type=F32)

    def online_probs_of(c, g, scores):
        bias = jnp.where(mask_ref[c].astype(F32) > 0.0, 0.0, -jnp.inf)
        probs, alphas = [], []
        for i in range(grp):
            h = g * grp + i
            m = m_ref[h:h + 1, :]
            s = scores[i] + bias
            m_new = jnp.maximum(m, jnp.max(_fold_rows(s, jnp.maximum), axis=0, keepdims=True))
            m_safe = jnp.where(m_new == -jnp.inf, 0.0, m_new)
            alphas.append(jnp.exp2(m - m_safe))
            probs.append(jnp.exp2(s - m_safe).astype(BF16))
            m_ref[h:h + 1, :] = m_new
        return probs, alphas

    def online_pv_of(c, g, probs, alphas):
        vc = vt_ref[0, g, c]
        for i in range(grp):
            h = g * grp + i
            acc_ref[h] = alphas[i] * acc_ref[h] + jnp.dot(vc, probs[i], preferred_element_type=F32)

    def attend(chunks, probs_fn, pv_fn):
        units = [(c, g) for c in chunks for g in range(N_KV_HEADS)]
        scores = scores_of(*units[0])
        pending = None
        for n, unit in enumerate(units):
            nxt = scores_of(*units[n + 1]) if n + 1 < len(units) else None
            cur = probs_fn(*unit, scores)
            if pending is not None:
                pv_fn(*units[n - 1], *pending)
            scores, pending = nxt, cur
        pv_fn(*units[-1], *pending)

    acc_ref[...] = jnp.zeros(acc_ref.shape, F32)

    def attend_masked(chunks):
        for c in chunks:
            mask_chunk(c)
        attend(chunks, probs_of, pv_of)

    def chunk_quad(n, carry):
        attend_masked([4 * n + i for i in range(4)])
        return carry

    lax.fori_loop(0, nch // 4, chunk_quad, 0)
    done = (nch // 4) * 4

    @pl.when((nch & 2) != 0)
    def _():
        attend_masked([done, done + 1])

    @pl.when((nch & 1) != 0)
    def _():
        attend_masked([nch - 1])

    sums = jnp.concatenate([acc_ref[h, SUM_ROW:SUM_ROW + 1, :] for h in range(N_ATT_HEADS)], axis=0)
    underflow = jnp.where(sums >= MIN_SOFTMAX_SUM, 0, 1)

    @pl.when(jnp.max(underflow) > 0)
    def _():
        m_ref[...] = jnp.full(m_ref.shape, -jnp.inf, F32)
        acc_ref[...] = jnp.zeros(acc_ref.shape, F32)

        def chunk_online(c, carry):
            attend([c], online_probs_of, online_pv_of)
            return carry

        lax.fori_loop(0, nch, chunk_online, 0)

    for h in range(N_ATT_HEADS):
        ot_ref[h * hd:(h + 1) * hd, :] = acc_ref[h, :hd, :] / acc_ref[h, SUM_ROW:SUM_ROW + 1, :]
    o_ref[0] = ot_ref[...].T.astype(BF16)


def _attn_call(ki, qit, tailt, k4, vt5, qt, topk):
    bsz, seq, _ = ki.shape
    tq = vt5.shape[-1]
    nch = seq // tq
    assert tq >= topk and vt5.shape[2] == nch
    assert seq // COUNT_ROWS <= 256
    return pl.pallas_call(
        functools.partial(_attn_kernel, topk=topk),
        grid=(bsz, nch),
        in_specs=[
            pl.BlockSpec((1, seq, IDX_DIM), lambda b, j: (b, 0, 0)),
            pl.BlockSpec((1, IDX_HEADS * IDX_DIM, tq), lambda b, j: (b, 0, j)),
            pl.BlockSpec((1, LANES, tq), lambda b, j: (b, 0, j)),
            pl.BlockSpec((1, N_KV_HEADS, seq, LANES), lambda b, j: (b, 0, 0, 0)),
            pl.BlockSpec((1, N_KV_HEADS, nch, V_ROWS, tq), lambda b, j: (b, 0, 0, 0, 0)),
            pl.BlockSpec((1, ATT_DIM, tq), lambda b, j: (b, 0, j)),
        ],
        out_specs=pl.BlockSpec((1, tq, ATT_DIM), lambda b, j: (b, j, 0)),
        out_shape=jax.ShapeDtypeStruct((bsz, seq, ATT_DIM), BF16),
        scratch_shapes=[pltpu.VMEM((nch, tq, tq), F32), pltpu.VMEM((nch, tq, tq), BF16),
                        pltpu.VMEM((nch, tq, tq), BF16),
                        pltpu.VMEM((ATT_DIM, tq), F32),
                        pltpu.VMEM((N_ATT_HEADS, tq), F32),
                        pltpu.VMEM((N_ATT_HEADS, V_ROWS, tq), F32),
                        pltpu.VMEM((N_ATT_HEADS, LANES, tq), BF16),
                        pltpu.VMEM((N_KV_HEADS, tq), F32), pltpu.VMEM((1, tq), F32)],
        compiler_params=pltpu.CompilerParams(dimension_semantics=("parallel", "arbitrary"),
                                             vmem_limit_bytes=VMEM_LIMIT_BYTES),
        name="attn",
    )(ki, qit, tailt, k4, vt5, qt)


def _split2(v):
    hi = v.astype(BF16)
    mid = (v - hi.astype(F32)).astype(BF16)
    return jnp.concatenate([hi, mid], axis=1)


def _ssd_kernel(xbc_ref, z_ref, tail_ref, cw_ref, cb_ref, alog_ref, e2_ref,
                dx_ref, nw_ref, out_ref, carry, state, ybuf):
    L = CHUNK
    hpg = SSD_HEADS // SSD_GROUPS
    gw = hpg * SSD_HEAD_DIM

    nshift = CONV_W - 1
    carry_rows = nshift * SUBLANES

    @pl.when(pl.program_id(1) == 0)
    def _():
        carry[...] = jnp.zeros_like(carry)
        state[...] = jnp.zeros_like(state)

    u = xbc_ref[0]
    sub8 = lax.broadcasted_iota(I32, (SUBLANES, XBC_DIM), 0)
    wrapped = []
    for k in range(nshift):
        rows = slice(L - carry_rows + k * SUBLANES, L - carry_rows + (k + 1) * SUBLANES)
        merged = jnp.where(sub8 == SUBLANES - 1, carry[k * SUBLANES:(k + 1) * SUBLANES, :], u[rows, :])
        wrapped.append(pltpu.roll(merged, 1, 0))
    carry[...] = u[L - carry_rows:, :]
    conv = cb_ref[...] + cw_ref[CONV_W - 1:CONV_W, :] * u
    for d in range(1, CONV_W):
        shifted = jnp.concatenate(wrapped[nshift - d:] + [u[:L - d * SUBLANES, :]], axis=0)
        conv = conv + cw_ref[CONV_W - 1 - d:CONV_W - d, :] * shifted
    act = _silu(conv)
    xs = act[:, :D_INNER]

    row = lax.broadcasted_iota(I32, (L, L), 0)
    col = lax.broadcasted_iota(I32, (L, L), 1)
    time_of = lambda r: PERM_STRIDE * (r & (SUBLANES - 1)) + (r >> (SUBLANES.bit_length() - 1))
    lower = time_of(row) >= time_of(col)
    tail = tail_ref[0]
    adt = tail * (-jnp.exp(alog_ref[...]) * LOG2_E)
    hi = adt.astype(BF16)
    r1 = adt - hi.astype(F32)
    mid = r1.astype(BF16)
    lo = (r1 - mid.astype(F32)).astype(BF16)
    ones_l = jnp.where(lower, 1.0, 0.0).astype(BF16)
    a_cum = jnp.dot(jnp.concatenate([ones_l, ones_l, ones_l], axis=1),
                    jnp.concatenate([hi, mid, lo], axis=0), preferred_element_type=F32)
    a_cum_t = a_cum.T
    e2 = e2_ref[...]
    a_cum_x = jnp.dot(_split2(a_cum), e2, preferred_element_type=F32)
    dt_x = jnp.dot(_split2(tail), e2, preferred_element_type=F32)
    a_last_x = a_cum_x[L - 1:L, :]
    xf = xs * dt_x
    xf_b = xf.astype(BF16)
    xd_b = (xf * jnp.exp2(a_last_x - a_cum_x)).astype(BF16)
    ea_x = jnp.exp2(a_cum_x)
    chunk_decay_x = jnp.exp2(a_last_x)

    lane = lax.broadcasted_iota(I32, (L, LANES), 1)
    for g in range(SSD_GROUPS):
        b_f = act[:, D_INNER + g * D_STATE:D_INNER + (g + 1) * D_STATE]
        bg = b_f.astype(BF16)
        bg_t = b_f.T.astype(BF16)
        cg = act[:, D_INNER + BC_DIM + g * D_STATE:D_INNER + BC_DIM + (g + 1) * D_STATE].astype(BF16)
        gs = slice(g * gw, (g + 1) * gw)
        cb = lax.dot_general(cg, bg, (((1,), (1,)), ((), ())), preferred_element_type=F32)
        st_prev = state[:, gs]
        y_off = jnp.dot(cg, st_prev.astype(BF16), preferred_element_type=F32) * ea_x[:, gs]
        st_new = jnp.dot(bg_t, xd_b[:, gs], preferred_element_type=F32)
        state[:, gs] = st_prev * chunk_decay_x[:, gs] + st_new
        for pr in range(hpg // 2):
            h0 = g * hpg + 2 * pr
            xs_pair = xf_b[:, h0 * SSD_HEAD_DIM:(h0 + 2) * SSD_HEAD_DIM]
            ws = []
            for h in (h0, h0 + 1):
                seg = a_cum[:, DT_OFF + h:DT_OFF + h + 1] - a_cum_t[DT_OFF + h:DT_OFF + h + 1, :]
                lmat = jnp.exp2(jnp.where(lower, seg, -jnp.inf))
                ws.append((cb * lmat).astype(BF16))
            zero_b = jnp.zeros_like(xs_pair)
            stacked = jnp.concatenate([jnp.where(lane < SSD_HEAD_DIM, xs_pair, zero_b),
                                       jnp.where(lane < SSD_HEAD_DIM, zero_b, xs_pair)], axis=0)
            ybuf[h0 // 2] = (jnp.dot(jnp.concatenate(ws, axis=1), stacked, preferred_element_type=F32)
                             + y_off[:, 2 * pr * SSD_HEAD_DIM:(2 * pr + 2) * SSD_HEAD_DIM])

    ntile = D_INNER // LANES
    y = jnp.concatenate([ybuf[j] for j in range(ntile)], axis=1) + dx_ref[...] * xs
    zf = z_ref[0].astype(F32)
    yz = y * _silu(zf)
    for g in range(SSD_GROUPS):
        gs = slice(g * gw, (g + 1) * gw)
        yg = yz[:, gs]
        msq = jnp.mean(yg * yg, axis=-1, keepdims=True)
        og = yg * lax.rsqrt(msq + LN_EPS) * nw_ref[:, gs]
        for j in range(gw // LANES):
            ybuf[g * (gw // LANES) + j] = og[:, j * LANES:(j + 1) * LANES]
    half = L // 2
    out_ref[0] = jnp.concatenate(
        [jnp.concatenate([ybuf[j, pl.ds(half * (t % 2) + t // 2, SUBLANES, stride=SUBLANES), :]
                          for j in range(ntile)], axis=1)
         for t in range(PERM_STRIDE)], axis=0).astype(BF16)


def _ssd_call(xbc, z, tail, cw, cb, alog, e2, dx, nw):
    bsz, seq, _ = xbc.shape
    L = CHUNK
    blk = lambda n: pl.BlockSpec((1, L, n), lambda b, c: (b, c, 0))
    consts = (cw, cb, alog, e2, dx, nw)
    return pl.pallas_call(
        _ssd_kernel,
        grid=(bsz, seq // L),
        in_specs=[blk(XBC_DIM), blk(D_INNER), blk(LANES)] + [_const_spec(c.shape) for c in consts],
        out_specs=blk(D_INNER),
        out_shape=jax.ShapeDtypeStruct((bsz, seq, D_INNER), BF16),
        scratch_shapes=[pltpu.VMEM(((CONV_W - 1) * SUBLANES, XBC_DIM), F32),
                        pltpu.VMEM((D_STATE, D_INNER), F32),
                        pltpu.VMEM((D_INNER // LANES, L, LANES), F32)],
        compiler_params=pltpu.CompilerParams(dimension_semantics=("parallel", "arbitrary"),
                                             vmem_limit_bytes=VMEM_LIMIT_BYTES),
        name="ssd",
    )(xbc, z, tail, *consts)


def _layer_norm(v, g, b):
    mu = jnp.mean(v, axis=-1, keepdims=True)
    d = v - mu
    var = jnp.mean(d * d, axis=-1, keepdims=True)
    return d * lax.rsqrt(var + LN_EPS) * g + b


def _mix_kernel(att_ref, ssd_ref, g_ref, x_ref, wa_ref, ws_ref, wo_ref, l1g_ref, l1b_ref,
                wu_ref, wd_ref, l2g_ref, l2b_ref, o_ref, *, alpha, ff_chunk):
    tm = x_ref.shape[0]
    subs = [slice(s * MIX_SUB_ROWS, (s + 1) * MIX_SUB_ROWS) for s in range(tm // MIX_SUB_ROWS)]
    mixed = []
    for r in subs:
        ga = g_ref[r, :D_MODEL].astype(F32)
        gs = g_ref[r, D_MODEL:].astype(F32)
        mixed.append(
            _sigmoid(ga) * jnp.dot(att_ref[r, :], wa_ref[...], preferred_element_type=F32)
            + _sigmoid(gs) * jnp.dot(ssd_ref[r, :], ws_ref[...], preferred_element_type=F32))
    hs = []
    for r, mx in zip(subs, mixed):
        proj = jnp.dot(mx.astype(BF16), wo_ref[...], preferred_element_type=F32)
        hs.append(_layer_norm(alpha * x_ref[r, :] + proj, l1g_ref[...], l1b_ref[...]))
    hbs = [h.astype(BF16) for h in hs]
    ffs = [jnp.zeros_like(h) for h in hs]
    for c in range(D_FF // ff_chunk):
        cs = slice(c * ff_chunk, (c + 1) * ff_chunk)
        for s in range(len(subs)):
            u = jnp.maximum(jnp.dot(hbs[s], wu_ref[:, cs], preferred_element_type=F32), 0.0)
            ffs[s] = ffs[s] + jnp.dot((u * u).astype(BF16), wd_ref[cs, :],
                                      preferred_element_type=F32)
    for r, h, ff in zip(subs, hs, ffs):
        o_ref[r, :] = _layer_norm(alpha * h + ff, l2g_ref[...], l2b_ref[...])


def _mix_call(att, ssd, gates, x2, wa, ws, wo, l1g, l1b, wu, wd, l2g, l2b, alpha, tm):
    m = x2.shape[0]
    row = lambda n: pl.BlockSpec((tm, n), lambda i: (i, 0))
    consts = (wa, ws, wo, l1g, l1b, wu, wd, l2g, l2b)
    return pl.pallas_call(
        functools.partial(_mix_kernel, alpha=alpha, ff_chunk=1024),
        grid=(m // tm,),
        in_specs=[row(ATT_DIM), row(D_INNER), row(2 * D_MODEL), row(D_MODEL)]
                 + [_const_spec(c.shape) for c in consts],
        out_specs=row(D_MODEL),
        out_shape=jax.ShapeDtypeStruct((m, D_MODEL), F32),
        compiler_params=pltpu.CompilerParams(dimension_semantics=("parallel",),
                                             vmem_limit_bytes=VMEM_LIMIT_BYTES),
        name="mix",
    )(att, ssd, gates, x2, *consts)


def _rope_tables(seq):
    half = ROT_DIM // 2
    inv = ROPE_THETA ** (-jnp.arange(0, ROT_DIM, 2, dtype=F32) / ROT_DIM)
    ang = jnp.arange(seq, dtype=F32)[:, None] * inv[None, :]
    cos, sin = jnp.cos(ang), jnp.sin(ang)
    pad = jnp.zeros((seq, ATT_HEAD_DIM - ROT_DIM), F32)
    zero = jnp.zeros((seq, half), F32)
    rc = jnp.concatenate([cos, cos, pad + 1.0], -1)
    rs1 = jnp.concatenate([-sin, zero, pad], -1)
    rs2 = jnp.concatenate([zero, sin, pad], -1)
    rep = LANES // ATT_HEAD_DIM
    return tuple(jnp.tile(t, (1, rep)) for t in (rc, rs1, rs2))


def _lane_row(v, off):
    return jnp.zeros((1, LANES), F32).at[0, off:off + v.shape[0]].set(v)


def _layer(x, tables, topk, alpha, w_in, q_norm_w, w_uq, w_iq, k_idx_norm_g, k_idx_norm_b,
           conv_w, conv_b, dt_bias, a_log, d_skip, ssd_norm_w, w_attn_branch, w_ssd_branch,
           w_out, ln1_g, ln1_b, w_up, w_down, ln2_g, ln2_b):
    bsz, seq, _ = x.shape
    m = bsz * seq

    w_packed = _pack_call(w_in.T)

    qt, qit, k4, vt5, ki, tail, tailt, z, xbc, gates = _proj_call(
        x, w_packed, q_norm_w.reshape(1, Q_LORA), w_uq.astype(BF16), w_iq.astype(BF16),
        _lane_row(k_idx_norm_g, 0), _lane_row(k_idx_norm_b, 0), _lane_row(dt_bias, DT_OFF), *tables)

    att = _attn_call(ki, qit, tailt, k4, vt5, qt, topk)

    slot = jnp.arange(LANES)[:, None] - DT_OFF
    expand = (slot == jnp.arange(D_INNER)[None, :] // SSD_HEAD_DIM).astype(BF16)
    e2 = jnp.concatenate([expand, expand], axis=0)
    ssd = _ssd_call(xbc, z, tail, conv_w, conv_b.reshape(1, XBC_DIM), _lane_row(a_log, DT_OFF), e2,
                    jnp.repeat(d_skip, SSD_HEAD_DIM).reshape(1, D_INNER),
                    ssd_norm_w.reshape(1, D_INNER))

    r = lambda t: t.reshape(1, -1)
    out = _mix_call(att.reshape(m, ATT_DIM), ssd.reshape(m, D_INNER), gates.reshape(m, 2 * D_MODEL),
                    x.reshape(m, D_MODEL), w_attn_branch.astype(BF16), w_ssd_branch.astype(BF16),
                    w_out.astype(BF16), r(ln1_g), r(ln1_b), w_up.astype(BF16), w_down.astype(BF16),
                    r(ln2_g), r(ln2_b), alpha, tm=MIX_ROWS)
    return out.reshape(bsz, seq, D_MODEL)


def kernel(x, w_in, q_norm_w, w_uq, w_iq, k_idx_norm_g, k_idx_norm_b, conv_w, conv_b, dt_bias,
           a_log, d_skip, ssd_norm_w, w_attn_branch, w_ssd_branch, w_out, ln1_g, ln1_b, w_up,
           w_down, ln2_g, ln2_b):
    depth = w_in.shape[0]
    seq = x.shape[1]
    topk = min(TOPK_MAX, seq // 4)
    alpha = (2 * depth) ** 0.25
    tables = _rope_tables(seq)
    h = x
    for l in range(depth):
        h = _layer(h, tables, topk, alpha, w_in[l], q_norm_w[l], w_uq[l], w_iq[l], k_idx_norm_g[l],
                   k_idx_norm_b[l], conv_w[l], conv_b[l], dt_bias[l], a_log[l], d_skip[l],
                   ssd_norm_w[l], w_attn_branch[l], w_ssd_branch[l], w_out[l], ln1_g[l], ln1_b[l],
                   w_up[l], w_down[l], ln2_g[l], ln2_b[l])
    return h
```

```python
import functools

import jax
import jax.numpy as jnp
from jax import lax
from jax.experimental import pallas as pl
from jax.experimental.pallas import tpu as pltpu

F32 = jnp.float32
BF16 = jnp.bfloat16
I32 = jnp.int32

D_MODEL = 1024
N_ATT_HEADS = 16
ATT_HEAD_DIM = 64
N_KV_HEADS = 4
ATT_DIM = N_ATT_HEADS * ATT_HEAD_DIM
KV_DIM = N_KV_HEADS * ATT_HEAD_DIM
Q_LORA = 256
IDX_HEADS = 8
IDX_DIM = 64
TOPK_MAX = 256
ROT_DIM = ATT_HEAD_DIM // 4
ROPE_THETA = 500000.0
D_INNER = 2 * D_MODEL
SSD_HEAD_DIM = 64
SSD_HEADS = D_INNER // SSD_HEAD_DIM
SSD_GROUPS = 4
D_STATE = 128
CONV_W = 4
CHUNK = 128
BC_DIM = SSD_GROUPS * D_STATE
XBC_DIM = D_INNER + 2 * BC_DIM
D_FF = 4 * D_MODEL
IN_SIZES = (Q_LORA, KV_DIM, KV_DIM, IDX_DIM, IDX_HEADS, D_INNER, XBC_DIM, SSD_HEADS, 2 * D_MODEL)
LN_EPS = 1e-5

LANES = 128
SUBLANES = 8
VMEM_LIMIT_BYTES = 56 * 1024 * 1024

SEQ_TILE = 256
MIX_ROWS = 512
MIX_SUB_ROWS = 256

SMALL_COLS = Q_LORA + 2 * KV_DIM + LANES
TAIL_OFF = Q_LORA + 2 * KV_DIM
WI_OFF = IDX_DIM
WI_END = WI_OFF + IDX_HEADS
DT_OFF = 0
TAIL_END = DT_OFF + SSD_HEADS
WDT_OFF = SMALL_COLS
WZ_OFF = WDT_OFF + LANES
WX_OFF = WZ_OFF + D_INNER
WG_OFF = WX_OFF + XBC_DIM
PACKED_COLS = WG_OFF + 2 * D_MODEL

PERM_STRIDE = CHUNK // SUBLANES
BF16_SUBLANES = 16
V_ROWS = ATT_HEAD_DIM + BF16_SUBLANES
SUM_ROW = ATT_HEAD_DIM
SHIFT_MARGIN = 1.0 + 2.0 ** -6
MIN_SOFTMAX_SUM = 2.0 ** -40
LOG2_E = 1.4426950408889634
INT_MIN = -(2 ** 31)


def _sigmoid(v):
    return 0.5 + 0.5 * jnp.tanh(0.5 * v)


def _silu(v):
    h = 0.5 * v
    return h + h * jnp.tanh(h)


def _const_spec(shape):
    zeros = (0,) * len(shape)
    return pl.BlockSpec(shape, lambda *_: zeros, pipeline_mode=pl.Buffered(1))


PACK_TILES = 4
ROW_GROUPS = LANES // SUBLANES


def _pack_tile_source(t):
    offs = [0]
    for s in IN_SIZES:
        offs.append(offs[-1] + s)
    assert all(o % SUBLANES == 0 for o in offs)
    first = lambda seg_index, tile0: offs[seg_index] // SUBLANES + ROW_GROUPS * (t - tile0)
    return jnp.where(t < WDT_OFF // LANES, ROW_GROUPS * t,
                     jnp.where(t < WZ_OFF // LANES, first(7, WDT_OFF // LANES),
                               jnp.where(t < WX_OFF // LANES, first(5, WZ_OFF // LANES),
                                         jnp.where(t < WG_OFF // LANES, first(6, WX_OFF // LANES),
                                                   first(8, WG_OFF // LANES)))))


def _pack_kernel(*refs):
    o_ref = refs[-1]
    for i, w_ref in enumerate(refs[:-1]):
        rows = w_ref[...].reshape(LANES, w_ref.shape[-1])
        o_ref[:, i * LANES:(i + 1) * LANES] = rows.T.astype(BF16)


def _pack_call(w_t):
    cols, rows = w_t.shape
    w3 = w_t.reshape(cols // SUBLANES, SUBLANES, rows)
    tile_spec = lambda i: pl.BlockSpec(
        (pl.Element(ROW_GROUPS), pl.Element(SUBLANES), pl.Element(rows)),
        lambda j: (_pack_tile_source(PACK_TILES * j + i), 0, 0))
    return pl.pallas_call(
        _pack_kernel,
        grid=(PACKED_COLS // (PACK_TILES * LANES),),
        in_specs=[tile_spec(i) for i in range(PACK_TILES)],
        out_specs=pl.BlockSpec((rows, PACK_TILES * LANES), lambda j: (0, j)),
        out_shape=jax.ShapeDtypeStruct((rows, PACKED_COLS), BF16),
        compiler_params=pltpu.CompilerParams(dimension_semantics=("parallel",),
                                             vmem_limit_bytes=VMEM_LIMIT_BYTES),
        name="pack",
    )(*([w3] * PACK_TILES))


def _proj_kernel(x_ref, w_ref, perm_ref, qnw_ref, wuq_ref, wiq_ref,
                 kng_ref, knb_ref, dtb_ref, rc_ref, rs1_ref, rs2_ref,
                 qt_ref, qit_ref, k4_ref, vt_ref, ki_ref, tail_ref, tailt_ref,
                 z_ref, xbc_ref, g_ref):
    xb = x_ref[0].astype(BF16)
    rc, rs1, rs2 = rc_ref[...], rs1_ref[...], rs2_ref[...]
    hd = ATT_HEAD_DIM
    w_cols = lambda off, n: w_ref[:, off:off + n]

    xp = jnp.dot(perm_ref[...], xb, preferred_element_type=F32).astype(BF16)

    def rope(t):
        half = ROT_DIM // 2
        return (t * rc + pltpu.roll(t, LANES - half, 1) * rs1 + pltpu.roll(t, half, 1) * rs2)

    pa = jnp.dot(xb, w_cols(0, SMALL_COLS), preferred_element_type=F32)

    lane = lax.broadcasted_iota(I32, (xb.shape[0], LANES), 1)
    dtr_p = jnp.dot(xp, w_cols(WDT_OFF, LANES), preferred_element_type=F32) + dtb_ref[...]
    dt_p = jnp.maximum(dtr_p, 0.0) + jnp.log1p(jnp.exp(-jnp.abs(dtr_p)))
    tail_ref[0] = jnp.where((lane >= DT_OFF) & (lane < TAIL_END), dt_p, 0.0)
    z_ref[0] = jnp.dot(xp, w_cols(WZ_OFF, D_INNER), preferred_element_type=F32).astype(BF16)
    xbc_ref[0] = jnp.dot(xp, w_cols(WX_OFF, XBC_DIM), preferred_element_type=F32)

    c_q = pa[:, :Q_LORA]
    ms = jnp.mean(c_q * c_q, axis=-1, keepdims=True)
    cq = (c_q * lax.rsqrt(ms + LN_EPS) * qnw_ref[...]).astype(BF16)

    scale = ATT_HEAD_DIM ** -0.5 * LOG2_E
    qf = jnp.dot(cq, wuq_ref[...], preferred_element_type=F32)
    for j in range(ATT_DIM // LANES):
        sl = slice(j * LANES, (j + 1) * LANES)
        qt_ref[0, sl, :] = (rope(qf[:, sl]) * scale).T.astype(BF16)
    qif = jnp.dot(cq, wiq_ref[...], preferred_element_type=F32)
    iscale = IDX_DIM ** -0.5
    for j in range(IDX_HEADS * IDX_DIM // LANES):
        sl = slice(j * LANES, (j + 1) * LANES)
        qit_ref[0, sl, :] = (rope(qif[:, sl]) * iscale).T.astype(BF16)
    for j in range(KV_DIM // LANES):
        kr = rope(pa[:, Q_LORA + j * LANES:Q_LORA + (j + 1) * LANES])
        for i, kh in enumerate((kr, pltpu.roll(kr, hd, 1))):
            k4_ref[0, 2 * j + i] = jnp.where(lane < hd, kh, jnp.where(lane == hd, 1.0, 0.0)).astype(BF16)
        vtr = pa[:, Q_LORA + KV_DIM + j * LANES:Q_LORA + KV_DIM + (j + 1) * LANES].T
        sum_rows = jnp.where(lax.broadcasted_iota(I32, (V_ROWS - hd, vtr.shape[1]), 0) == 0,
                             1.0, 0.0).astype(BF16)
        vt_ref[0, 2 * j, 0] = jnp.concatenate([vtr[:hd, :].astype(BF16), sum_rows], axis=0)
        vt_ref[0, 2 * j + 1, 0] = jnp.concatenate([vtr[hd:, :].astype(BF16), sum_rows], axis=0)

    tail = pa[:, TAIL_OFF:TAIL_OFF + LANES]
    is_ki = lane < IDX_DIM
    mu = jnp.sum(jnp.where(is_ki, tail, 0.0), axis=-1, keepdims=True) * (1.0 / IDX_DIM)
    dv = jnp.where(is_ki, tail - mu, 0.0)
    var = jnp.sum(dv * dv, axis=-1, keepdims=True) * (1.0 / IDX_DIM)
    ki = rope(dv * lax.rsqrt(var + LN_EPS) * kng_ref[...] + knb_ref[...])
    wi = tail * (IDX_HEADS ** -0.5)
    ki_ref[0] = ki[:, :IDX_DIM].astype(BF16)
    tailt_ref[0] = jnp.where(is_ki, ki, jnp.where(lane < WI_END, wi, 0.0)).T

    g_ref[0] = jnp.dot(xb, w_cols(WG_OFF, 2 * D_MODEL), preferred_element_type=F32).astype(BF16)


def _proj_call(x, w_packed, qnw, wuq, wiq, kng, knb, dtb, rc, rs1, rs2):
    bsz, seq, _ = x.shape
    tm = SEQ_TILE
    nsb = seq // tm
    r = jnp.arange(tm)
    src = (r // CHUNK) * CHUNK + PERM_STRIDE * (r % SUBLANES) + (r % CHUNK) // SUBLANES
    perm = (jnp.arange(tm)[None, :] == src[:, None]).astype(BF16)
    row = lambda n: pl.BlockSpec((1, tm, n), lambda b, i: (b, i, 0))
    col = lambda n: pl.BlockSpec((1, n, tm), lambda b, i: (b, 0, i))
    tab = pl.BlockSpec((tm, LANES), lambda b, i: (i, 0))
    sds = jax.ShapeDtypeStruct
    out_shapes = (
        sds((bsz, ATT_DIM, seq), BF16),
        sds((bsz, IDX_HEADS * IDX_DIM, seq), BF16),
        sds((bsz, N_KV_HEADS, seq, LANES), BF16),
        sds((bsz, N_KV_HEADS, nsb, V_ROWS, tm), BF16),
        sds((bsz, seq, IDX_DIM), BF16),
        sds((bsz, seq, LANES), F32),
        sds((bsz, LANES, seq), F32),
        sds((bsz, seq, D_INNER), BF16),
        sds((bsz, seq, XBC_DIM), F32),
        sds((bsz, seq, 2 * D_MODEL), BF16),
    )
    out_specs = [
        col(ATT_DIM), col(IDX_HEADS * IDX_DIM),
        pl.BlockSpec((1, N_KV_HEADS, tm, LANES), lambda b, i: (b, 0, i, 0)),
        pl.BlockSpec((1, N_KV_HEADS, 1, V_ROWS, tm), lambda b, i: (b, 0, i, 0, 0)),
        row(IDX_DIM), row(LANES), col(LANES), row(D_INNER), row(XBC_DIM), row(2 * D_MODEL),
    ]
    consts = (w_packed, perm, qnw, wuq, wiq, kng, knb, dtb)
    return pl.pallas_call(
        _proj_kernel,
        grid=(bsz, nsb),
        in_specs=[row(D_MODEL)] + [_const_spec(c.shape) for c in consts] + [tab, tab, tab],
        out_specs=out_specs,
        out_shape=out_shapes,
        compiler_params=pltpu.CompilerParams(dimension_semantics=("parallel", "parallel"),
                                             vmem_limit_bytes=VMEM_LIMIT_BYTES),
        name="proj",
    )(x, *consts, rc, rs1, rs2)


COUNT_ROWS = 32


def _fold_rows(v, op, rows=32):
    while v.shape[0] > rows:
        half = v.shape[0] // 2
        v = op(v[:half], v[half:])
    return v


def _attn_kernel(ki_ref, qit_ref, wit_ref, k_ref, vt_ref, qt_ref, o_ref,
                 score_ref, coarse_ref, mask_ref, ot_ref, m_ref, acc_ref, qa_ref, kmax_ref,
                 *, topk):
    tq = qt_ref.shape[2]
    tk = tq
    qblk = pl.program_id(1)
    nch = qblk + 1
    sub = lax.broadcasted_iota(I32, (tk, tq), 0)
    tpos = qblk * tq + lax.broadcasted_iota(I32, (tk, tq), 1)

    def score_chunk(c, carry):
        off = pl.multiple_of(c * tk, tk)
        kic = ki_ref[0, pl.ds(off, tk), :]
        acc = jnp.zeros((tk, tq), F32)
        for hh in range(IDX_HEADS):
            sc = jnp.dot(kic, qit_ref[0, hh * IDX_DIM:(hh + 1) * IDX_DIM, :],
                         preferred_element_type=F32)
            acc = acc + wit_ref[0, WI_OFF + hh:WI_OFF + hh + 1, :] * jnp.maximum(sc, 0.0)
        masked = jnp.where(off + sub <= tpos, acc, -jnp.inf)
        score_ref[c] = masked
        coarse_ref[c] = masked.astype(BF16)
        return carry

    lax.fori_loop(0, nch // 2, lambda n, c: score_chunk(2 * n + 1, score_chunk(2 * n, c)), 0)

    @pl.when(nch % 2 == 1)
    def _():
        score_chunk(nch - 1, 0)

    def count(pred, ref=score_ref, dtype=I32):
        def body(c, part):
            hit = jnp.where(pred(ref[c], c), jnp.ones((), dtype), jnp.zeros((), dtype))
            return part + _fold_rows(hit, jnp.add, COUNT_ROWS)

        part = lax.fori_loop(0, nch, body, jnp.zeros((COUNT_ROWS, tq), dtype))
        return jnp.sum(part.astype(I32), axis=0, keepdims=True)

    def ordered_float(u):
        key = u ^ jnp.int32(INT_MIN)
        return pltpu.bitcast(jnp.where(key < 0, key ^ jnp.int32(0x7FFFFFFF), key), F32)

    half_bits = 16
    neg_fill = jnp.int32(2 ** half_bits - 1)

    def coarse_key(u16):
        return lax.shift_left(u16, half_bits) | jnp.where(u16 >= 2 ** (half_bits - 1), 0, neg_fill)

    def coarse_search(i, u16):
        cand = u16 | lax.shift_left(jnp.int32(1), half_bits - 1 - i)
        cand_b = ordered_float(coarse_key(cand)).astype(BF16)
        n = count(lambda sc, c: sc >= cand_b, coarse_ref, BF16)
        return jnp.where(n >= topk, cand, u16)

    u16 = lax.fori_loop(0, half_bits, coarse_search, jnp.zeros((1, tq), I32))
    base_u = coarse_key(u16) - jnp.int32(2 ** half_bits)

    def fine_search(i, d):
        cand = d | lax.shift_left(jnp.int32(1), half_bits - i)
        cand_f = ordered_float(base_u + cand)
        return jnp.where(count(lambda sc, c: sc >= cand_f) >= topk, cand, d)

    thr = ordered_float(base_u + lax.fori_loop(0, half_bits + 1, fine_search,
                                               jnp.zeros((1, tq), I32)))
    thr = jnp.where(tpos[0:1, :] < topk, -jnp.inf, thr)

    need = (topk - count(lambda sc, c: sc > thr)).astype(F32)
    tri = jnp.where(sub >= lax.broadcasted_iota(I32, (tk, tk), 1), 1.0, 0.0).astype(BF16)

    def mask_chunk(c, ties_before):
        sc = score_ref[c]
        tie = jnp.where(sc == thr, 1.0, 0.0)
        tie_rank = jnp.dot(tri, tie.astype(BF16), preferred_element_type=F32) + ties_before
        sel = jnp.where(sc > thr, 1.0, jnp.where(tie_rank <= need, tie, 0.0))
        mask_ref[c] = jnp.where(c * tk + sub <= tpos, sel, 0.0).astype(BF16)
        return tie_rank[tk - 1:tk, :]

    grp = N_ATT_HEADS // N_KV_HEADS
    hd = ATT_HEAD_DIM
    klane = lax.broadcasted_iota(I32, (k_ref.shape[2], LANES), 1)

    @pl.when(qblk == 0)
    def _():
        for g in range(N_KV_HEADS):
            kf = k_ref[0, g].astype(F32)
            norm2 = jnp.sum(jnp.where(klane < hd, kf * kf, 0.0), axis=1, keepdims=True)
            kmax_ref[g:g + 1, :] = jnp.broadcast_to(jnp.max(norm2, axis=0, keepdims=True), (1, tq))

    shift_row = lax.broadcasted_iota(I32, (LANES - hd, tq), 0) == 0
    for h in range(N_ATT_HEADS):
        qs = qt_ref[0, h * hd:(h + 1) * hd, :]
        qf = qs.astype(F32)
        r = jnp.sqrt(jnp.sum(qf * qf, axis=0, keepdims=True) * kmax_ref[h // grp:h // grp + 1, :])
        r = r * SHIFT_MARGIN
        qa_ref[h] = jnp.concatenate([qs, jnp.where(shift_row, -r, 0.0).astype(BF16)], axis=0)

    def scores_of(c, g):
        kc = k_ref[0, g, pl.ds(pl.multiple_of(c * tk, tk), tk), :]
        return [jnp.dot(kc, qa_ref[g * grp + i], preferred_element_type=F32) for i in range(grp)]

    def probs_of(c, g, scores):
        mk = mask_ref[c]
        return ([jnp.exp2(s).astype(BF16) * mk for s in scores],)

    def pv_of(c, g, probs):
        vc = vt_ref[0, g, c]
        for i in range(grp):
            h = g * grp + i
            acc_ref[h] = acc_ref[h] + jnp.dot(vc, probs[i], preferred_element_type=F32)

    def online_probs_of(c, g, scores):
        bias = jnp.where(mask_ref[c].astype(F32) > 0.0, 0.0, -jnp.inf)
        probs, alphas = [], []
        for i in range(grp):
            h = g * grp + i
            m = m_ref[h:h + 1, :]
            s = scores[i] + bias
            m_new = jnp.maximum(m, jnp.max(_fold_rows(s, jnp.maximum), axis=0, keepdims=True))
            m_safe = jnp.where(m_new == -jnp.inf, 0.0, m_new)
            alphas.append(jnp.exp2(m - m_safe))
            probs.append(jnp.exp2(s - m_safe).astype(BF16))
            m_ref[h:h + 1, :] = m_new
        return probs, alphas

    def online_pv_of(c, g, probs, alphas):
        vc = vt_ref[0, g, c]
        for i in range(grp):
            h = g * grp + i
            acc_ref[h] = alphas[i] * acc_ref[h] + jnp.dot(vc, probs[i], preferred_element_type=F32)

    def attend(chunks, probs_fn, pv_fn):
        units = [(c, g) for c in chunks for g in range(N_KV_HEADS)]
        scores = scores_of(*units[0])
        pending = None
        for n, unit in enumerate(units):
            nxt = scores_of(*units[n + 1]) if n + 1 < len(units) else None
            cur = probs_fn(*unit, scores)
            if pending is not None:
                pv_fn(*units[n - 1], *pending)
            scores, pending = nxt, cur
        pv_fn(*units[-1], *pending)

    acc_ref[...] = jnp.zeros(acc_ref.shape, F32)

    def chunk_pair(n, ties):
        ties = mask_chunk(2 * n + 1, mask_chunk(2 * n, ties))
        attend([2 * n, 2 * n + 1], probs_of, pv_of)
        return ties

    ties = lax.fori_loop(0, nch // 2, chunk_pair, jnp.zeros((1, tq), F32))

    @pl.when(nch % 2 == 1)
    def _():
        mask_chunk(nch - 1, ties)
        attend([nch - 1], probs_of, pv_of)

    sums = jnp.concatenate([acc_ref[h, SUM_ROW:SUM_ROW + 1, :] for h in range(N_ATT_HEADS)], axis=0)
    underflow = jnp.where(sums >= MIN_SOFTMAX_SUM, 0, 1)

    @pl.when(jnp.max(underflow) > 0)
    def _():
        m_ref[...] = jnp.full(m_ref.shape, -jnp.inf, F32)
        acc_ref[...] = jnp.zeros(acc_ref.shape, F32)

        def chunk_online(c, carry):
            attend([c], online_probs_of, online_pv_of)
            return carry

        lax.fori_loop(0, nch, chunk_online, 0)

    for h in range(N_ATT_HEADS):
        ot_ref[h * hd:(h + 1) * hd, :] = acc_ref[h, :hd, :] / acc_ref[h, SUM_ROW:SUM_ROW + 1, :]
    o_ref[0] = ot_ref[...].T.astype(BF16)


def _attn_call(ki, qit, tailt, k4, vt5, qt, topk):
    bsz, seq, _ = ki.shape
    tq = vt5.shape[-1]
    nch = seq // tq
    assert tq >= topk and vt5.shape[2] == nch
    assert seq // COUNT_ROWS <= 256
    return pl.pallas_call(
        functools.partial(_attn_kernel, topk=topk),
        grid=(bsz, nch),
        in_specs=[
            pl.BlockSpec((1, seq, IDX_DIM), lambda b, j: (b, 0, 0)),
            pl.BlockSpec((1, IDX_HEADS * IDX_DIM, tq), lambda b, j: (b, 0, j)),
            pl.BlockSpec((1, LANES, tq), lambda b, j: (b, 0, j)),
            pl.BlockSpec((1, N_KV_HEADS, seq, LANES), lambda b, j: (b, 0, 0, 0)),
            pl.BlockSpec((1, N_KV_HEADS, nch, V_ROWS, tq), lambda b, j: (b, 0, 0, 0, 0)),
            pl.BlockSpec((1, ATT_DIM, tq), lambda b, j: (b, 0, j)),
        ],
        out_specs=pl.BlockSpec((1, tq, ATT_DIM), lambda b, j: (b, j, 0)),
        out_shape=jax.ShapeDtypeStruct((bsz, seq, ATT_DIM), BF16),
        scratch_shapes=[pltpu.VMEM((nch, tq, tq), F32), pltpu.VMEM((nch, tq, tq), BF16),
                        pltpu.VMEM((nch, tq, tq), BF16),
                        pltpu.VMEM((ATT_DIM, tq), F32),
                        pltpu.VMEM((N_ATT_HEADS, tq), F32),
                        pltpu.VMEM((N_ATT_HEADS, V_ROWS, tq), F32),
                        pltpu.VMEM((N_ATT_HEADS, LANES, tq), BF16),
                        pltpu.VMEM((N_KV_HEADS, tq), F32)],
        compiler_params=pltpu.CompilerParams(dimension_semantics=("parallel", "arbitrary"),
                                             vmem_limit_bytes=VMEM_LIMIT_BYTES),
        name="attn",
    )(ki, qit, tailt, k4, vt5, qt)


def _split2(v):
    hi = v.astype(BF16)
    mid = (v - hi.astype(F32)).astype(BF16)
    return jnp.concatenate([hi, mid], axis=1)


def _ssd_kernel(xbc_ref, z_ref, tail_ref, cw_ref, cb_ref, alog_ref, e2_ref,
                dx_ref, nw_ref, out_ref, carry, state, ybuf):
    L = CHUNK
    hpg = SSD_HEADS // SSD_GROUPS
    gw = hpg * SSD_HEAD_DIM

    nshift = CONV_W - 1
    carry_rows = nshift * SUBLANES

    @pl.when(pl.program_id(1) == 0)
    def _():
        carry[...] = jnp.zeros_like(carry)
        state[...] = jnp.zeros_like(state)

    u = xbc_ref[0]
    sub8 = lax.broadcasted_iota(I32, (SUBLANES, XBC_DIM), 0)
    wrapped = []
    for k in range(nshift):
        rows = slice(L - carry_rows + k * SUBLANES, L - carry_rows + (k + 1) * SUBLANES)
        merged = jnp.where(sub8 == SUBLANES - 1, carry[k * SUBLANES:(k + 1) * SUBLANES, :], u[rows, :])
        wrapped.append(pltpu.roll(merged, 1, 0))
    carry[...] = u[L - carry_rows:, :]
    conv = cb_ref[...] + cw_ref[CONV_W - 1:CONV_W, :] * u
    for d in range(1, CONV_W):
        shifted = jnp.concatenate(wrapped[nshift - d:] + [u[:L - d * SUBLANES, :]], axis=0)
        conv = conv + cw_ref[CONV_W - 1 - d:CONV_W - d, :] * shifted
    act = _silu(conv)
    xs = act[:, :D_INNER]

    row = lax.broadcasted_iota(I32, (L, L), 0)
    col = lax.broadcasted_iota(I32, (L, L), 1)
    time_of = lambda r: PERM_STRIDE * (r & (SUBLANES - 1)) + (r >> (SUBLANES.bit_length() - 1))
    lower = time_of(row) >= time_of(col)
    tail = tail_ref[0]
    adt = tail * (-jnp.exp(alog_ref[...]) * LOG2_E)
    hi = adt.astype(BF16)
    r1 = adt - hi.astype(F32)
    mid = r1.astype(BF16)
    lo = (r1 - mid.astype(F32)).astype(BF16)
    ones_l = jnp.where(lower, 1.0, 0.0).astype(BF16)
    a_cum = jnp.dot(jnp.concatenate([ones_l, ones_l, ones_l], axis=1),
                    jnp.concatenate([hi, mid, lo], axis=0), preferred_element_type=F32)
    a_cum_t = a_cum.T
    e2 = e2_ref[...]
    a_cum_x = jnp.dot(_split2(a_cum), e2, preferred_element_type=F32)
    dt_x = jnp.dot(_split2(tail), e2, preferred_element_type=F32)
    a_last_x = a_cum_x[L - 1:L, :]
    xf = xs * dt_x
    xf_b = xf.astype(BF16)
    xd_b = (xf * jnp.exp2(a_last_x - a_cum_x)).astype(BF16)
    ea_x = jnp.exp2(a_cum_x)
    chunk_decay_x = jnp.exp2(a_last_x)

    lane = lax.broadcasted_iota(I32, (L, LANES), 1)
    for g in range(SSD_GROUPS):
        b_f = act[:, D_INNER + g * D_STATE:D_INNER + (g + 1) * D_STATE]
        bg = b_f.astype(BF16)
        bg_t = b_f.T.astype(BF16)
        cg = act[:, D_INNER + BC_DIM + g * D_STATE:D_INNER + BC_DIM + (g + 1) * D_STATE].astype(BF16)
        gs = slice(g * gw, (g + 1) * gw)
        cb = lax.dot_general(cg, bg, (((1,), (1,)), ((), ())), preferred_element_type=F32)
        st_prev = state[:, gs]
        y_off = jnp.dot(cg, st_prev.astype(BF16), preferred_element_type=F32) * ea_x[:, gs]
        st_new = jnp.dot(bg_t, xd_b[:, gs], preferred_element_type=F32)
        state[:, gs] = st_prev * chunk_decay_x[:, gs] + st_new
        for pr in range(hpg // 2):
            h0 = g * hpg + 2 * pr
            xs_pair = xf_b[:, h0 * SSD_HEAD_DIM:(h0 + 2) * SSD_HEAD_DIM]
            ws = []
            for h in (h0, h0 + 1):
                seg = a_cum[:, DT_OFF + h:DT_OFF + h + 1] - a_cum_t[DT_OFF + h:DT_OFF + h + 1, :]
                lmat = jnp.exp2(jnp.where(lower, seg, -jnp.inf))
                ws.append((cb * lmat).astype(BF16))
            zero_b = jnp.zeros_like(xs_pair)
            stacked = jnp.concatenate([jnp.where(lane < SSD_HEAD_DIM, xs_pair, zero_b),
                                       jnp.where(lane < SSD_HEAD_DIM, zero_b, xs_pair)], axis=0)
            ybuf[h0 // 2] = (jnp.dot(jnp.concatenate(ws, axis=1), stacked, preferred_element_type=F32)
                             + y_off[:, 2 * pr * SSD_HEAD_DIM:(2 * pr + 2) * SSD_HEAD_DIM])

    ntile = D_INNER // LANES
    y = jnp.concatenate([ybuf[j] for j in range(ntile)], axis=1) + dx_ref[...] * xs
    zf = z_ref[0].astype(F32)
    yz = y * _silu(zf)
    for g in range(SSD_GROUPS):
        gs = slice(g * gw, (g + 1) * gw)
        yg = yz[:, gs]
        msq = jnp.mean(yg * yg, axis=-1, keepdims=True)
        og = yg * lax.rsqrt(msq + LN_EPS) * nw_ref[:, gs]
        for j in range(gw // LANES):
            ybuf[g * (gw // LANES) + j] = og[:, j * LANES:(j + 1) * LANES]
    half = L // 2
    out_ref[0] = jnp.concatenate(
        [jnp.concatenate([ybuf[j, pl.ds(half * (t % 2) + t // 2, SUBLANES, stride=SUBLANES), :]
                          for j in range(ntile)], axis=1)
         for t in range(PERM_STRIDE)], axis=0).astype(BF16)


def _ssd_call(xbc, z, tail, cw, cb, alog, e2, dx, nw):
    bsz, seq, _ = xbc.shape
    L = CHUNK
    blk = lambda n: pl.BlockSpec((1, L, n), lambda b, c: (b, c, 0))
    consts = (cw, cb, alog, e2, dx, nw)
    return pl.pallas_call(
        _ssd_kernel,
        grid=(bsz, seq // L),
        in_specs=[blk(XBC_DIM), blk(D_INNER), blk(LANES)] + [_const_spec(c.shape) for c in consts],
        out_specs=blk(D_INNER),
        out_shape=jax.ShapeDtypeStruct((bsz, seq, D_INNER), BF16),
        scratch_shapes=[pltpu.VMEM(((CONV_W - 1) * SUBLANES, XBC_DIM), F32),
                        pltpu.VMEM((D_STATE, D_INNER), F32),
                        pltpu.VMEM((D_INNER // LANES, L, LANES), F32)],
        compiler_params=pltpu.CompilerParams(dimension_semantics=("parallel", "arbitrary"),
                                             vmem_limit_bytes=VMEM_LIMIT_BYTES),
        name="ssd",
    )(xbc, z, tail, *consts)


def _layer_norm(v, g, b):
    mu = jnp.mean(v, axis=-1, keepdims=True)
    d = v - mu
    var = jnp.mean(d * d, axis=-1, keepdims=True)
    return d * lax.rsqrt(var + LN_EPS) * g + b


def _mix_kernel(att_ref, ssd_ref, g_ref, x_ref, wa_ref, ws_ref, wo_ref, l1g_ref, l1b_ref,
                wu_ref, wd_ref, l2g_ref, l2b_ref, o_ref, *, alpha, ff_chunk):
    tm = x_ref.shape[0]
    subs = [slice(s * MIX_SUB_ROWS, (s + 1) * MIX_SUB_ROWS) for s in range(tm // MIX_SUB_ROWS)]
    mixed = []
    for r in subs:
        ga = g_ref[r, :D_MODEL].astype(F32)
        gs = g_ref[r, D_MODEL:].astype(F32)
        mixed.append(
            _sigmoid(ga) * jnp.dot(att_ref[r, :], wa_ref[...], preferred_element_type=F32)
            + _sigmoid(gs) * jnp.dot(ssd_ref[r, :], ws_ref[...], preferred_element_type=F32))
    hs = []
    for r, mx in zip(subs, mixed):
        proj = jnp.dot(mx.astype(BF16), wo_ref[...], preferred_element_type=F32)
        hs.append(_layer_norm(alpha * x_ref[r, :] + proj, l1g_ref[...], l1b_ref[...]))
    hbs = [h.astype(BF16) for h in hs]
    ffs = [jnp.zeros_like(h) for h in hs]
    for c in range(D_FF // ff_chunk):
        cs = slice(c * ff_chunk, (c + 1) * ff_chunk)
        for s in range(len(subs)):
            u = jnp.maximum(jnp.dot(hbs[s], wu_ref[:, cs], preferred_element_type=F32), 0.0)
            ffs[s] = ffs[s] + jnp.dot((u * u).astype(BF16), wd_ref[cs, :],
                                      preferred_element_type=F32)
    for r, h, ff in zip(subs, hs, ffs):
        o_ref[r, :] = _layer_norm(alpha * h + ff, l2g_ref[...], l2b_ref[...])


def _mix_call(att, ssd, gates, x2, wa, ws, wo, l1g, l1b, wu, wd, l2g, l2b, alpha, tm):
    m = x2.shape[0]
    row = lambda n: pl.BlockSpec((tm, n), lambda i: (i, 0))
    consts = (wa, ws, wo, l1g, l1b, wu, wd, l2g, l2b)
    return pl.pallas_call(
        functools.partial(_mix_kernel, alpha=alpha, ff_chunk=1024),
        grid=(m // tm,),
        in_specs=[row(ATT_DIM), row(D_INNER), row(2 * D_MODEL), row(D_MODEL)]
                 + [_const_spec(c.shape) for c in consts],
        out_specs=row(D_MODEL),
        out_shape=jax.ShapeDtypeStruct((m, D_MODEL), F32),
        compiler_params=pltpu.CompilerParams(dimension_semantics=("parallel",),
                                             vmem_limit_bytes=VMEM_LIMIT_BYTES),
        name="mix",
    )(att, ssd, gates, x2, *consts)


def _rope_tables(seq):
    half = ROT_DIM // 2
    inv = ROPE_THETA ** (-jnp.arange(0, ROT_DIM, 2, dtype=F32) / ROT_DIM)
    ang = jnp.arange(seq, dtype=F32)[:, None] * inv[None, :]
    cos, sin = jnp.cos(ang), jnp.sin(ang)
    pad = jnp.zeros((seq, ATT_HEAD_DIM - ROT_DIM), F32)
    zero = jnp.zeros((seq, half), F32)
    rc = jnp.concatenate([cos, cos, pad + 1.0], -1)
    rs1 = jnp.concatenate([-sin, zero, pad], -1)
    rs2 = jnp.concatenate([zero, sin, pad], -1)
    rep = LANES // ATT_HEAD_DIM
    return tuple(jnp.tile(t, (1, rep)) for t in (rc, rs1, rs2))


def _lane_row(v, off):
    return jnp.zeros((1, LANES), F32).at[0, off:off + v.shape[0]].set(v)


def _layer(x, tables, topk, alpha, w_in, q_norm_w, w_uq, w_iq, k_idx_norm_g, k_idx_norm_b,
           conv_w, conv_b, dt_bias, a_log, d_skip, ssd_norm_w, w_attn_branch, w_ssd_branch,
           w_out, ln1_g, ln1_b, w_up, w_down, ln2_g, ln2_b):
    bsz, seq, _ = x.shape
    m = bsz * seq

    w_packed = _pack_call(w_in.T)

    qt, qit, k4, vt5, ki, tail, tailt, z, xbc, gates = _proj_call(
        x, w_packed, q_norm_w.reshape(1, Q_LORA), w_uq.astype(BF16), w_iq.astype(BF16),
        _lane_row(k_idx_norm_g, 0), _lane_row(k_idx_norm_b, 0), _lane_row(dt_bias, DT_OFF), *tables)

    att = _attn_call(ki, qit, tailt, k4, vt5, qt, topk)

    slot = jnp.arange(LANES)[:, None] - DT_OFF
    expand = (slot == jnp.arange(D_INNER)[None, :] // SSD_HEAD_DIM).astype(BF16)
    e2 = jnp.concatenate([expand, expand], axis=0)
    ssd = _ssd_call(xbc, z, tail, conv_w, conv_b.reshape(1, XBC_DIM), _lane_row(a_log, DT_OFF), e2,
                    jnp.repeat(d_skip, SSD_HEAD_DIM).reshape(1, D_INNER),
                    ssd_norm_w.reshape(1, D_INNER))

    r = lambda t: t.reshape(1, -1)
    out = _mix_call(att.reshape(m, ATT_DIM), ssd.reshape(m, D_INNER), gates.reshape(m, 2 * D_MODEL),
                    x.reshape(m, D_MODEL), w_attn_branch.astype(BF16), w_ssd_branch.astype(BF16),
                    w_out.astype(BF16), r(ln1_g), r(ln1_b), w_up.astype(BF16), w_down.astype(BF16),
                    r(ln2_g), r(ln2_b), alpha, tm=MIX_ROWS)
    return out.reshape(bsz, seq, D_MODEL)


def kernel(x, w_in, q_norm_w, w_uq, w_iq, k_idx_norm_g, k_idx_norm_b, conv_w, conv_b, dt_bias,
           a_log, d_skip, ssd_norm_w, w_attn_branch, w_ssd_branch, w_out, ln1_g, ln1_b, w_up,
           w_down, ln2_g, ln2_b):
    depth = w_in.shape[0]
    seq = x.shape[1]
    topk = min(TOPK_MAX, seq // 4)
    alpha = (2 * depth) ** 0.25
    tables = _rope_tables(seq)
    h = x
    for l in range(depth):
        h = _layer(h, tables, topk, alpha, w_in[l], q_norm_w[l], w_uq[l], w_iq[l], k_idx_norm_g[l],
                   k_idx_norm_b[l], conv_w[l], conv_b[l], dt_bias[l], a_log[l], d_skip[l],
                   ssd_norm_w[l], w_attn_branch[l], w_ssd_branch[l], w_out[l], ln1_g[l], ln1_b[l],
                   w_up[l], w_down[l], ln2_g[l], ln2_b[l])
    return h
```
